```python
import math
import jax
import jax.numpy as jnp
from jax import lax
import numpy as np

D_MODEL = 1024
BATCH = 16
SEQ = 256
DEPTH = 4
DEC_BATCH = 4
DEC_SEQ = 1024
PAST_LEN = 512

GRID_W = 64
N_MIXERS = 3
N_ATTN = len(range(0, DEPTH, N_MIXERS))
N_REC = len(range(1, DEPTH, N_MIXERS))
N_FOUR = len(range(2, DEPTH, N_MIXERS))

ATTN_HEADS = 8
ATTN_DH = D_MODEL // ATTN_HEADS // 2
ATTN_DV = 2 * ATTN_DH
ROPE_THETA = 10000.0
Q_BLOCK = 128

REC_HEADS = 8
REC_DK = D_MODEL // REC_HEADS
REC_DV = D_MODEL // REC_HEADS
REC_CHUNK = 16

FOUR_GROUPS = 4
FOUR_DG = D_MODEL // FOUR_GROUPS

D_FF = ((8 * D_MODEL // 3 + 255) // 256) * 256
N_MOD = 6
EPS = 1e-6

kernel_name = 'hybrid_diffattn_hgrn2_fnet_prefix_dit_step'

F32 = jnp.float32


def rmsnorm(x, g):
    xf = x.astype(F32)
    y = xf * lax.rsqrt(jnp.mean(xf * xf, axis=-1, keepdims=True) + EPS)
    return (y * g.astype(F32)).astype(x.dtype)


def modulation(cond, w, b):
    m = jax.nn.silu(cond.reshape(-1, D_MODEL)) @ w + b
    return [t[:, None, :] for t in jnp.split(m, N_MOD, axis=-1)]


def adanorm(x, g, shift, scale):
    return rmsnorm(x, g) * (1.0 + scale) + shift


def grid_positions(n_tokens):
    rows = n_tokens // GRID_W
    row = jnp.repeat(jnp.arange(rows), GRID_W)
    col = jnp.tile(jnp.arange(GRID_W), rows)
    return row, col


def rope_axis(x, pos):
    half = x.shape[-1] // 2
    inv = ROPE_THETA ** (-jnp.arange(half, dtype=F32) / half)
    ang = pos.astype(F32)[:, None] * inv
    cos = jnp.cos(ang)[None, :, None, None, :]
    sin = jnp.sin(ang)[None, :, None, None, :]
    x1, x2 = x[..., :half], x[..., half:]
    return jnp.concatenate([x1 * cos - x2 * sin, x2 * cos + x1 * sin], axis=-1)


def rope_2d(x, row, col):
    xf = x.astype(F32)
    n = ATTN_DH // 2
    out = jnp.concatenate([rope_axis(xf[..., :n], row), rope_axis(xf[..., n:], col)], axis=-1)
    return out.astype(x.dtype)


def diff_attn_core(q, k, v, lam):
    bsz, lq = q.shape[0], q.shape[1]
    nb = lq // Q_BLOCK
    qb = q.reshape(bsz, nb, Q_BLOCK, ATTN_HEADS, 2, ATTN_DH).swapaxes(0, 1)
    scale = ATTN_DH ** -0.5
    vf = v.astype(F32)

    def one_block(qi):
        s = jnp.einsum('bqhcd,bkhcd->bhcqk', qi, k).astype(F32) * scale
        p = jax.nn.softmax(s, axis=-1)
        pd = p[:, :, 0] - lam * p[:, :, 1]
        return jnp.einsum('bhqk,bkhe->bqhe', pd, vf)

    o = lax.map(one_block, qb)
    return o.swapaxes(0, 1).reshape(bsz, lq, ATTN_HEADS, ATTN_DV)


def attn_project(h, w_qkv):
    bsz, n = h.shape[0], h.shape[1]
    q, k, v = jnp.split(h @ w_qkv, 3, axis=-1)
    q = q.reshape(bsz, n, ATTN_HEADS, 2, ATTN_DH)
    k = k.reshape(bsz, n, ATTN_HEADS, 2, ATTN_DH)
    v = v.reshape(bsz, n, ATTN_HEADS, ATTN_DV)
    return q, k, v


def attn_output(o, g_subln, lam_init, w_o, dtype):
    bsz, n = o.shape[0], o.shape[1]
    o = rmsnorm(o, g_subln) * (1.0 - lam_init)
    return o.reshape(bsz, n, D_MODEL).astype(dtype) @ w_o


def chunk_gla(q, k, v, lf, s0):
    bsz, n, nh, dk = q.shape
    dv = v.shape[-1]
    c = REC_CHUNK
    nc = n // c
    q = q.reshape(bsz, nc, c, nh, dk)
    k = k.reshape(bsz, nc, c, nh, dk)
    v = v.reshape(bsz, nc, c, nh, dv)
    b = jnp.cumsum(lf.reshape(bsz, nc, c, nh, dk), axis=2)
    mask = jnp.tril(jnp.ones((c, c), dtype=bool))[:, :, None, None]
    diff = b[:, :, :, None] - b[:, :, None, :]
    dec = jnp.where(mask, jnp.exp(jnp.where(mask, diff, 0.0)), 0.0)
    a = jnp.sum(q[:, :, :, None] * k[:, :, None, :] * dec, axis=-1)
    o_intra = jnp.einsum('bntsh,bnshv->bnthv', a, v)
    b_last = b[:, :, -1]
    kv = jnp.einsum('bnshk,bnshv->bnhkv', k * jnp.exp(b_last[:, :, None] - b), v)

    def step(s, inp):
        d, kv_c = inp
        return d[..., None] * s + kv_c, s

    s_fin, s_start = lax.scan(step, s0, (jnp.exp(b_last).swapaxes(0, 1), kv.swapaxes(0, 1)))
    o_inter = jnp.einsum('bnthk,nbhkv->bnthv', q * jnp.exp(b), s_start)
    return (o_intra + o_inter).reshape(bsz, n, nh, dv), s_fin


def hgrn_lower_bounds(lb_logits, layer):
    s = jax.nn.softmax(lb_logits.astype(F32), axis=1)
    lb = jnp.cumsum(s, axis=1) - s[:, :1]
    return lb[:, layer]


def hgrn_project(h, w_in, lb):
    bsz, n = h.shape[0], h.shape[1]
    q, vi, g, zf, zb = jnp.split(h @ w_in, 5, axis=-1)
    q = jax.nn.silu(q.astype(F32)).reshape(bsz, n, REC_HEADS, REC_DK)
    vi = vi.astype(F32).reshape(bsz, n, REC_HEADS, REC_DV)

    def gates(z, lbd):
        f = lbd + (1.0 - lbd) * jax.nn.sigmoid(z.astype(F32))
        return (1.0 - f).reshape(bsz, n, REC_HEADS, REC_DK), jnp.log(f).reshape(bsz, n, REC_HEADS, REC_DK)

    kf, lff = gates(zf, lb[0])
    kb, lfb = gates(zb, lb[1])
    return q, vi, g, kf, lff, kb, lfb


def hgrn_mix(q, vi, kf, lff, kb, lfb, s0f, s0b):
    of, sf = chunk_gla(q, kf, vi, lff, s0f)
    ob, sb = chunk_gla(q[:, ::-1], kb[:, ::-1], vi[:, ::-1], lfb[:, ::-1], s0b)
    return of + ob[:, ::-1], jnp.stack([sf, sb], axis=1)


def hgrn_output(o, g, g_out, w_o, dtype):
    bsz, n = o.shape[0], o.shape[1]
    gate = jax.nn.silu(g.astype(F32)).reshape(bsz, n, REC_HEADS, REC_DV)
    o = rmsnorm(o, g_out) * gate
    return o.reshape(bsz, n, D_MODEL).astype(dtype) @ w_o


def fourier_mix(h, w):
    bsz, n = h.shape[0], h.shape[1]
    hg = h.astype(F32).reshape(bsz, n, FOUR_GROUPS, FOUR_DG)
    f = jnp.fft.fftn(hg, axes=(1, 3), norm='ortho').real
    return f.reshape(bsz, n, D_MODEL).astype(h.dtype) @ w


def swiglu(h, w_in, w_out):
    gt, up = jnp.split(h @ w_in, 2, axis=-1)
    return (jax.nn.silu(gt) * up) @ w_out


def setup_inputs(seed: int = 0) -> dict:
    key = jax.random.key(seed)
    ks = jax.random.split(key, 24)
    nrm = jax.random.normal
    d = D_MODEL
    return {
        'x_prompt': nrm(ks[0], (BATCH, SEQ, d), F32),
        'x_sample': nrm(ks[1], (DEC_BATCH, DEC_SEQ, d), F32),
        'cache_attn_k': nrm(ks[2], (DEC_BATCH, N_ATTN, PAST_LEN, ATTN_HEADS, 2, ATTN_DH), F32),
        'cache_attn_v': nrm(ks[3], (DEC_BATCH, N_ATTN, PAST_LEN, ATTN_HEADS, ATTN_DV), F32),
        'state_hgrn': 0.5 * nrm(ks[4], (DEC_BATCH, N_REC, 2, REC_HEADS, REC_DK, REC_DV), F32),
        'c': nrm(ks[5], (DEC_BATCH, d), F32),
        'c_ctx': nrm(ks[6], (d,), F32),
        'w_ada': nrm(ks[7], (DEPTH, d, N_MOD * d), F32) * d ** -0.5,
        'b_ada': 0.02 * nrm(ks[8], (DEPTH, N_MOD * d), F32),
        'g_norm_mix': 1.0 + 0.02 * nrm(ks[9], (DEPTH, d), F32),
        'g_norm_ffn': 1.0 + 0.02 * nrm(ks[10], (DEPTH, d), F32),
        'w_qkv_attn': nrm(ks[11], (N_ATTN, d, 3 * d), F32) * d ** -0.5,
        'lam_attn': 0.1 * nrm(ks[12], (N_ATTN, 4, ATTN_DH), F32),
        'g_subln_attn': 1.0 + 0.02 * nrm(ks[13], (N_ATTN, ATTN_DV), F32),
        'w_o_attn': nrm(ks[14], (N_ATTN, d, d), F32) * d ** -0.5,
        'w_in_rec': nrm(ks[15], (N_REC, d, 5 * d), F32) * d ** -0.5,
        'lb_logits_rec': 0.5 * nrm(ks[16], (2, DEPTH, REC_HEADS * REC_DK), F32),
        'g_out_rec': 1.0 + 0.02 * nrm(ks[17], (N_REC, REC_DV), F32),
        'w_o_rec': nrm(ks[18], (N_REC, d, d), F32) * d ** -0.5,
        'w_four': nrm(ks[19], (N_FOUR, d, d), F32) * d ** -0.5,
        'w_ffn_in': nrm(ks[20], (DEPTH, d, 2 * D_FF), F32) * d ** -0.5,
        'w_ffn_out': nrm(ks[21], (DEPTH, D_FF, d), F32) * D_FF ** -0.5,
        'g_final': 1.0 + 0.02 * nrm(ks[22], (d,), F32),
    }


def reference(x_prompt, x_sample, cache_attn_k, cache_attn_v, state_hgrn, c, c_ctx,
              w_ada, b_ada, g_norm_mix, g_norm_ffn, w_qkv_attn, lam_attn, g_subln_attn, w_o_attn,
              w_in_rec, lb_logits_rec, g_out_rec, w_o_rec, w_four, w_ffn_in, w_ffn_out, g_final):
    n_lat = x_sample.shape[1]
    row, col = grid_positions(n_lat)
    xp, xs = x_prompt, x_sample
    new_k, new_v, new_s = [], [], []
    for i in range(DEPTH):
        kind, j = i % N_MIXERS, i // N_MIXERS
        mp = modulation(c_ctx, w_ada[i], b_ada[i])
        ms = modulation(c, w_ada[i], b_ada[i])
        hp = adanorm(xp, g_norm_mix[i], mp[0], mp[1])
        hs = adanorm(xs, g_norm_mix[i], ms[0], ms[1])
        if kind == 0:
            lam_init = 0.8 - 0.6 * math.exp(-0.3 * i)
            lp = lam_attn[j].astype(F32)
            lam = jnp.exp(jnp.sum(lp[0] * lp[1])) - jnp.exp(jnp.sum(lp[2] * lp[3])) + lam_init
            qp, kp, vp = attn_project(hp, w_qkv_attn[j])
            op = attn_output(diff_attn_core(qp, kp, vp, lam), g_subln_attn[j], lam_init, w_o_attn[j], xp.dtype)
            new_k.append(kp)
            new_v.append(vp)
            qs, ks_, vs = attn_project(hs, w_qkv_attn[j])
            qs, ks_ = rope_2d(qs, row, col), rope_2d(ks_, row, col)
            k_all = jnp.concatenate([cache_attn_k[:, j].astype(ks_.dtype), ks_], axis=1)
            v_all = jnp.concatenate([cache_attn_v[:, j].astype(vs.dtype), vs], axis=1)
            os_ = attn_output(diff_attn_core(qs, k_all, v_all, lam), g_subln_attn[j], lam_init, w_o_attn[j], xs.dtype)
        elif kind == 1:
            lb = hgrn_lower_bounds(lb_logits_rec, i)
            q, vi, g, kf, lff, kb, lfb = hgrn_project(hp, w_in_rec[j], lb)
            zero = jnp.zeros((xp.shape[0], REC_HEADS, REC_DK, REC_DV), F32)
            o, s_ctx = hgrn_mix(q, vi, kf, lff, kb, lfb, zero, zero)
            op = hgrn_output(o, g, g_out_rec[j], w_o_rec[j], xp.dtype)
            new_s.append(s_ctx.astype(xp.dtype))
            q, vi, g, kf, lff, kb, lfb = hgrn_project(hs, w_in_rec[j], lb)
            st = state_hgrn[:, j].astype(F32)
            o, _ = hgrn_mix(q, vi, kf, lff, kb, lfb, st[:, 0], st[:, 1])
            os_ = hgrn_output(o, g, g_out_rec[j], w_o_rec[j], xs.dtype)
        else:
            op = fourier_mix(hp, w_four[j])
            os_ = fourier_mix(hs, w_four[j])
        xp = xp + mp[2] * op
        xs = xs + ms[2] * os_
        xp = xp + mp[5] * swiglu(adanorm(xp, g_norm_ffn[i], mp[3], mp[4]), w_ffn_in[i], w_ffn_out[i])
        xs = xs + ms[5] * swiglu(adanorm(xs, g_norm_ffn[i], ms[3], ms[4]), w_ffn_in[i], w_ffn_out[i])
    y_prompt = rmsnorm(xp, g_final)
    y_sample = rmsnorm(xs, g_final)
    return (y_prompt, y_sample, jnp.stack(new_k, axis=1), jnp.stack(new_v, axis=1), jnp.stack(new_s, axis=1))
```

```python
import math

import numpy as np
import jax
import jax.numpy as jnp
from jax import lax
from jax.experimental import pallas as pl
from jax.experimental.pallas import tpu as pltpu

F32 = jnp.float32
BF16 = jnp.bfloat16

D = 1024
BATCH = 16
SEQ = 256
DEPTH = 4
DEC_BATCH = 4
DEC_SEQ = 1024
PAST = 512
GRID_W = 64
N_MIXERS = 3
HEADS = 8
DH = 64
HD = 128
ROPE_THETA = 10000.0
D_FF = 2816
N_MOD = 6
EPS = 1e-6
FOUR_GROUPS = 4
FOUR_DG = 256

N_P = BATCH * SEQ
N_S = DEC_BATCH * DEC_SEQ
N_TOK = N_P + N_S
TM = 512
NT_P = N_P // TM
NT_S = N_S // TM
TILES_PER_DEC_SEQ = DEC_SEQ // TM
MOD_ROWS = 8
FF_CHUNK = 256
GLA_GROUP = 256
GLA_BLK = 8
GLA_LEVELS = (8, 16, 32, 64, 128)
VMEM_LIMIT = 52 * 1024 * 1024


def _cparams(n_axes):
    return pltpu.CompilerParams(dimension_semantics=("arbitrary",) * n_axes,
                                vmem_limit_bytes=VMEM_LIMIT)


def _const_spec(shape):
    nd = len(shape)
    return pl.BlockSpec(shape, lambda *_: (0,) * nd, pipeline_mode=pl.Buffered(1))


def _sigmoid(x):
    return 1.0 / (1.0 + jnp.exp(-x))


def _silu(x):
    return x * _sigmoid(x)


def _adanorm(x, g, shift, scale):
    ms = jnp.mean(x * x, axis=-1, keepdims=True)
    return x * lax.rsqrt(ms + EPS) * (g * (1.0 + scale)) + shift


def _mod_row(tile):
    return jnp.where(tile < NT_P, 0, 1 + (tile - NT_P) // TILES_PER_DEC_SEQ)


def _dot(a, b):
    return jnp.dot(a, b, preferred_element_type=F32)


def _dot_nt(a, b):
    return lax.dot_general(a, b, (((1,), (1,)), ((), ())), preferred_element_type=F32)


def _dot_tn(a, b):
    return lax.dot_general(a, b, (((0,), (0,)), ((), ())), preferred_element_type=F32)


def _split3(x):
    hi = x.astype(BF16)
    r1 = x - hi.astype(F32)
    mid = r1.astype(BF16)
    lo = (r1 - mid.astype(F32)).astype(BF16)
    return hi, mid, lo


MOD_TN = 1536


def _mod_kernel(cond_ref, w_ref, b_ref, o_ref):
    s = _silu(cond_ref[...])
    s_hi = s.astype(BF16)
    s_lo = (s - s_hi.astype(F32)).astype(BF16)
    w = w_ref[0]
    w_hi = w.astype(BF16)
    w_lo = (w - w_hi.astype(F32)).astype(BF16)
    o_ref[0] = _dot(s_hi, w_hi) + _dot(s_lo, w_hi) + _dot(s_hi, w_lo) + b_ref[0]


def _modulation(cond, w_ada, b_ada):
    return pl.pallas_call(
        _mod_kernel,
        grid=(DEPTH, N_MOD * D // MOD_TN),
        in_specs=[
            pl.BlockSpec((MOD_ROWS, D), lambda l, n: (0, 0)),
            pl.BlockSpec((1, D, MOD_TN), lambda l, n: (l, 0, n)),
            pl.BlockSpec((1, 1, MOD_TN), lambda l, n: (l, 0, n)),
        ],
        out_specs=pl.BlockSpec((1, MOD_ROWS, MOD_TN), lambda l, n: (l, 0, n)),
        out_shape=jax.ShapeDtypeStruct((DEPTH, MOD_ROWS, N_MOD * D), F32),
        compiler_params=_cparams(2),
        name="modulation",
    )(cond, w_ada, b_ada.reshape(DEPTH, 1, N_MOD * D))


def _rope_tables():
    t = np.arange(DEC_SEQ)
    row, col = t // GRID_W, t % GRID_W
    lane = np.arange(HD) % DH
    use_row = lane < DH // 2
    idx = (lane % (DH // 2)) % (DH // 4)
    inv = ROPE_THETA ** (-idx.astype(np.float64) / (DH // 4))
    pos = np.where(use_row[None, :], row[:, None], col[:, None]).astype(np.float64)
    ang = pos * inv[None, :]
    first = (lane % (DH // 2)) < DH // 4
    cos = np.cos(ang)
    sin = np.sin(ang)
    sin_minus = np.where(first[None, :], -sin, 0.0)
    sin_plus = np.where(first[None, :], 0.0, sin)
    return (jnp.asarray(cos, F32), jnp.asarray(sin_minus, F32), jnp.asarray(sin_plus, F32))


def _rope_tile(t, cos, sin_minus, sin_plus):
    return (t * cos + pltpu.roll(t, HD - DH // 4, 1) * sin_minus
            + pltpu.roll(t, DH // 4, 1) * sin_plus)


def _qkv_prompt_kernel(x_ref, mod_ref, g_ref, w_ref, q_ref, k_ref, v_ref):
    m = mod_ref[pl.ds(0, 1), :]
    h = _adanorm(x_ref[...], g_ref[...], m[:, 0:D], m[:, D:2 * D]).astype(BF16)
    q_ref[...] = (_dot(h, w_ref[:, 0:D]) * DH ** -0.5).astype(BF16)
    k_ref[...] = _dot(h, w_ref[:, D:2 * D])
    v_ref[...] = _dot(h, w_ref[:, 2 * D:3 * D])


def _qkv_sample_kernel(x_ref, mod_ref, g_ref, w_ref, cos_ref, sm_ref, sp_ref, q_ref, k_ref, v_ref):
    i = pl.program_id(0)
    m = mod_ref[pl.ds(1 + i // TILES_PER_DEC_SEQ, 1), :]
    h = _adanorm(x_ref[...], g_ref[...], m[:, 0:D], m[:, D:2 * D]).astype(BF16)
    cos, sm, sp = cos_ref[...], sm_ref[...], sp_ref[...]
    for o_ref, c0, scale in ((q_ref, 0, DH ** -0.5), (k_ref, D, None)):
        y = _dot(h, w_ref[:, c0:c0 + D])
        if scale is not None:
            y = y * scale
        for hh in range(HEADS):
            t = _rope_tile(y[:, hh * HD:(hh + 1) * HD], cos, sm, sp)
            o_ref[:, hh * HD:(hh + 1) * HD] = t.astype(BF16)
    v_ref[...] = _dot(h, w_ref[:, 2 * D:3 * D]).astype(BF16)


def _qkv_prompt(x, mod_l, g, w):
    tile = pl.BlockSpec((TM, D), lambda i: (i, 0))
    return pl.pallas_call(
        _qkv_prompt_kernel,
        grid=(NT_P,),
        in_specs=[tile, _const_spec((MOD_ROWS, N_MOD * D)), _const_spec((1, D)), _const_spec((D, 3 * D))],
        out_specs=[tile, tile, tile],
        out_shape=[jax.ShapeDtypeStruct((N_P, D), BF16), jax.ShapeDtypeStruct((N_P, D), F32),
                   jax.ShapeDtypeStruct((N_P, D), F32)],
        compiler_params=_cparams(1),
        name="qkv_prompt",
    )(x, mod_l, g, w)


def _qkv_sample(x, mod_l, g, w, tabs):
    tile = pl.BlockSpec((TM, D), lambda i: (i, 0))
    tab = pl.BlockSpec((TM, HD), lambda i: (i % TILES_PER_DEC_SEQ, 0))
    return pl.pallas_call(
        _qkv_sample_kernel,
        grid=(NT_S,),
        in_specs=[pl.BlockSpec((TM, D), lambda i: (i + NT_P, 0)), _const_spec((MOD_ROWS, N_MOD * D)),
                  _const_spec((1, D)), _const_spec((D, 3 * D)), tab, tab, tab],
        out_specs=[tile, tile, tile],
        out_shape=[jax.ShapeDtypeStruct((N_S, D), BF16)] * 3,
        compiler_params=_cparams(1),
        name="qkv_sample",
    )(x, mod_l, g, w, *tabs)


def _attn_kernel(*refs, lam_init, has_cache):
    if has_cache:
        q_ref, kn_ref, vn_ref, kc_ref, vc_ref, lam_ref, g_ref, o_ref = refs
    else:
        q_ref, kn_ref, vn_ref, lam_ref, g_ref, o_ref = refs
    lp = lam_ref[...]
    lam = (jnp.exp(jnp.sum(lp[0:1] * lp[1:2], axis=-1, keepdims=True))
           - jnp.exp(jnp.sum(lp[2:3] * lp[3:4], axis=-1, keepdims=True)) + lam_init)
    q = q_ref[...]
    tq = q.shape[0]
    lane = lax.broadcasted_iota(jnp.int32, q.shape, 1)
    zero = jnp.zeros_like(q)
    qq = jnp.concatenate([jnp.where(lane < DH, q, zero), jnp.where(lane >= DH, q, zero)], axis=0)
    parts = [(kn_ref[...].astype(BF16), vn_ref[...].astype(BF16))]
    if has_cache:
        parts.append((kc_ref[...].astype(BF16), vc_ref[...].astype(BF16)))
    scores = [_dot_nt(qq, k) for k, _ in parts]
    mx = scores[0].max(axis=-1, keepdims=True)
    for s in scores[1:]:
        mx = jnp.maximum(mx, s.max(axis=-1, keepdims=True))
    es = [jnp.exp(s - mx) for s in scores]
    den = es[0].sum(axis=-1, keepdims=True)
    for e in es[1:]:
        den = den + e.sum(axis=-1, keepdims=True)
    inv = 1.0 / den
    w0 = inv[:tq]
    w1 = lam * inv[tq:]
    o = None
    for e, (_, v) in zip(es, parts):
        pd = (e[:tq] * w0 - e[tq:] * w1).astype(BF16)
        c = _dot(pd, v)
        o = c if o is None else o + c
    o = o * lax.rsqrt(jnp.mean(o * o, axis=-1, keepdims=True) + EPS) * (g_ref[...] * (1.0 - lam_init))
    o_ref[...] = o.astype(BF16)


def _attention(q, kn, vn, lam_p, g_sub, *, nb, lq, lkn, tq, lam_init, cache=None):
    nq = lq // tq
    in_specs = [
        pl.BlockSpec((tq, HD), lambda b, h, i: (b * nq + i, h)),
        pl.BlockSpec((lkn, HD), lambda b, h, i: (b, h)),
        pl.BlockSpec((lkn, HD), lambda b, h, i: (b, h)),
    ]
    args = [q, kn, vn]
    if cache is not None:
        ck, cv, j = cache
        in_specs += [pl.BlockSpec((None, None, PAST, HD), lambda b, h, i: (b, j, 0, h))] * 2
        args += [ck, cv]
    in_specs += [pl.BlockSpec((4, DH), lambda b, h, i: (0, 0)), pl.BlockSpec((1, HD), lambda b, h, i: (0, 0))]
    args += [lam_p, g_sub]
    kern = lambda *refs: _attn_kernel(*refs, lam_init=lam_init, has_cache=cache is not None)
    return pl.pallas_call(
        kern,
        grid=(nb, HEADS, nq),
        in_specs=in_specs,
        out_specs=pl.BlockSpec((tq, HD), lambda b, h, i: (b * nq + i, h)),
        out_shape=jax.ShapeDtypeStruct((nb * lq, D), BF16),
        compiler_params=_cparams(3),
        name="attention_sample" if cache is not None else "attention_prompt",
    )(*args)


def _rec_proj_kernel(x_ref, mod_ref, g_ref, w_ref, q_ref, v_ref, gate_ref, zf_ref, zb_ref):
    m = mod_ref[pl.ds(_mod_row(pl.program_id(0)), 1), :]
    h = _adanorm(x_ref[...], g_ref[...], m[:, 0:D], m[:, D:2 * D]).astype(BF16)
    q_ref[...] = _silu(_dot(h, w_ref[:, 0:D])).astype(BF16)
    v_ref[...] = _dot(h, w_ref[:, D:2 * D]).astype(BF16)
    gate_ref[...] = _silu(_dot(h, w_ref[:, 2 * D:3 * D])).astype(BF16)
    zf_ref[...] = _dot(h, w_ref[:, 3 * D:4 * D])
    zb_ref[...] = _dot(h, w_ref[:, 4 * D:5 * D])


def _rec_proj(x, mod_l, g, w):
    tile = pl.BlockSpec((TM, D), lambda i: (i, 0))
    return pl.pallas_call(
        _rec_proj_kernel,
        grid=(N_TOK // TM,),
        in_specs=[tile, _const_spec((MOD_ROWS, N_MOD * D)), _const_spec((1, D)), _const_spec((D, 5 * D))],
        out_specs=[tile] * 5,
        out_shape=[jax.ShapeDtypeStruct((N_TOK, D), BF16)] * 3 + [jax.ShapeDtypeStruct((N_TOK, D), F32)] * 2,
        compiler_params=_cparams(1),
        name="rec_proj",
    )(x, mod_l, g, w)


def _gla_tables():
    t = np.arange(GLA_GROUP)[:, None]
    s = np.arange(GLA_GROUP)[None, :]
    ids = np.full((GLA_GROUP, GLA_GROUP), -1, np.int32)
    for l, b in reversed(list(enumerate(GLA_LEVELS, 1))):
        ids = np.where((t // (2 * b) == s // (2 * b)) & (s < t), l, ids)
    ids = np.where((t // GLA_BLK == s // GLA_BLK) & (s <= t), 0, ids)
    ids = np.stack([ids, ids.T]).astype(np.int32)
    e = np.zeros((GLA_BLK * HD, GLA_GROUP), np.float32)
    for j in range(GLA_BLK):
        e[j * HD:(j + 1) * HD, j::GLA_BLK] = 1.0
    return jnp.asarray(ids), jnp.asarray(e)


def _bcast_row(x, blk, idx):
    n, w = x.shape
    r = x.reshape(n // blk, blk, w)[:, idx:idx + 1, :]
    return jnp.broadcast_to(r, (n // blk, blk, w)).reshape(n, w)


def _gla_group(q, v, z, lb, ids, e_mat, st, backward):
    n = GLA_GROUP
    sig = _sigmoid(z)
    f = lb + (1.0 - lb) * sig
    k = (1.0 - lb) * (1.0 - sig)
    lf = jnp.log(f)
    row = lax.broadcasted_iota(jnp.int32, (n, n), 0)
    col = lax.broadcasted_iota(jnp.int32, (n, n), 1)
    tri = jnp.where((col >= row) if backward else (col <= row), 1.0, 0.0).astype(BF16)
    c3 = _dot(tri, jnp.concatenate(_split3(lf), axis=1))
    cum = c3[:, 0:HD] + c3[:, HD:2 * HD] + c3[:, 2 * HD:3 * HD]

    prods = []
    for j in range(GLA_BLK):
        kj = _bcast_row(k, GLA_BLK, j)
        cj = _bcast_row(cum, GLA_BLK, j)
        prods.append((q * kj * jnp.exp(jnp.minimum(cum - cj, 0.0))).astype(BF16))
    a = jnp.where(ids == 0, _dot(jnp.concatenate(prods, axis=1), e_mat), 0.0)

    for l, b in enumerate(GLA_LEVELS, 1):
        edge = _bcast_row(cum, 2 * b, b if backward else b - 1)
        ql = (q * jnp.exp(jnp.minimum(cum - edge, 0.0))).astype(BF16)
        kl = (k * jnp.exp(jnp.minimum(edge - cum, 0.0))).astype(BF16)
        a = jnp.where(ids == l, _dot_nt(ql, kl), a)
    o = _dot(a.astype(BF16), v)

    total = cum[0:1, :] if backward else cum[n - 1:n, :]
    if st is not None:
        o = o + _dot_nt((q * jnp.exp(cum)).astype(BF16), st.astype(BF16))
    kdec = (k * jnp.exp(total - cum)).astype(BF16)
    upd = _dot_tn(v, kdec)
    st_new = upd if st is None else st * jnp.exp(total) + upd
    return o, st_new


def _lower_bound(x, layer):
    rows = [x[l:l + 1, :] for l in range(DEPTH)]
    mx = rows[0]
    for r in rows[1:]:
        mx = jnp.maximum(mx, r)
    ex = [jnp.exp(r - mx) for r in rows]
    tot = ex[0]
    for e in ex[1:]:
        tot = tot + e
    acc = ex[0] * 0.0
    for l in range(1, layer + 1):
        acc = acc + ex[l]
    return acc / tot


def _gla_kernel(*refs, n_groups, layer, has_state):
    if has_state:
        (q_ref, v_ref, gate_ref, zf_ref, zb_ref, lbl_ref, g_ref, ids_ref, e_ref, s0_ref,
         o_ref, acc_ref) = refs
    else:
        (q_ref, v_ref, gate_ref, zf_ref, zb_ref, lbl_ref, g_ref, ids_ref, e_ref,
         o_ref, sout_ref, acc_ref) = refs
    e_mat = e_ref[...]
    z_refs = (zf_ref, zb_ref)
    lbs = [_lower_bound(lbl_ref[d], layer) for d in range(2)]

    def step(gi, sts):
        new = []
        for d in range(2):
            g = gi if d == 0 else n_groups - 1 - gi
            start = g * GLA_GROUP
            rows = pl.ds(start if isinstance(start, int) else pl.multiple_of(start, GLA_GROUP), GLA_GROUP)
            o, st = _gla_group(q_ref[rows, :].astype(F32), v_ref[rows, :], z_refs[d][rows, :], lbs[d],
                               ids_ref[d], e_mat, sts[d], backward=(d == 1))
            acc_ref[rows, :] = acc_ref[rows, :] + o
            new.append(st)
        return tuple(new)

    acc_ref[...] = jnp.zeros_like(acc_ref)
    if has_state:
        init = tuple(s0_ref[d].T for d in range(2))
        lax.fori_loop(0, n_groups, step, init)
    else:
        sts = step(0, (None, None))
        for d in range(2):
            sout_ref[d] = sts[d].T
    o = acc_ref[...]
    y = o * lax.rsqrt(jnp.mean(o * o, axis=-1, keepdims=True) + EPS) * g_ref[...]
    o_ref[...] = (y * gate_ref[...].astype(F32)).astype(BF16)


def _gla(q, v, gate, zf, zb, lb_logits, g_out, tabs, *, row0, nb, ln, layer, s0=None):
    ids, e_mat = tabs
    assert ln % GLA_GROUP == 0 and (s0 is not None or ln == GLA_GROUP)
    b0 = row0 // ln
    seq = pl.BlockSpec((ln, HD), lambda b, h: (b + b0, h))
    in_specs = [seq] * 5 + [
        pl.BlockSpec((2, DEPTH, HD), lambda b, h: (0, 0, h)),
        pl.BlockSpec((1, HD), lambda b, h: (0, 0)),
        pl.BlockSpec((2, GLA_GROUP, GLA_GROUP), lambda b, h: (0, 0, 0)),
        pl.BlockSpec((GLA_BLK * HD, GLA_GROUP), lambda b, h: (0, 0)),
    ]
    args = [q, v, gate, zf, zb, lb_logits, g_out, ids, e_mat]
    out_specs = [pl.BlockSpec((ln, HD), lambda b, h: (b, h))]
    out_shape = [jax.ShapeDtypeStruct((nb * ln, D), BF16)]
    state_spec = pl.BlockSpec((None, 2, None, HD, HD), lambda b, h: (b, 0, h, 0, 0))
    if s0 is not None:
        in_specs.append(state_spec)
        args.append(s0)
    else:
        out_specs.append(state_spec)
        out_shape.append(jax.ShapeDtypeStruct((nb, 2, HEADS, HD, HD), F32))
    kern = lambda *refs: _gla_kernel(*refs, n_groups=ln // GLA_GROUP, layer=layer, has_state=s0 is not None)
    return pl.pallas_call(
        kern,
        grid=(nb, HEADS),
        in_specs=in_specs,
        out_specs=out_specs,
        out_shape=out_shape,
        scratch_shapes=[pltpu.VMEM((ln, HD), F32)],
        compiler_params=_cparams(2),
        name="gla_sample" if s0 is not None else "gla_prompt",
    )(*args)


def _dft_tables(n):
    c = np.arange(FOUR_DG)
    ang_c = 2.0 * np.pi * ((c[:, None] * c[None, :]) % FOUR_DG) / FOUR_DG
    cs = np.concatenate([np.cos(ang_c), np.sin(ang_c)], axis=1)
    t = np.arange(n)
    ang_n = 2.0 * np.pi * ((t[:, None] * t[None, :]) % n) / n
    cn = np.concatenate([np.cos(ang_n), -np.sin(ang_n)], axis=1) / math.sqrt(n * FOUR_DG)
    return jnp.asarray(cs, F32).astype(BF16), jnp.asarray(cn, F32).astype(BF16)


def _fourier_kernel(x_ref, mod_ref, g_ref, cs_ref, cn_ref, o_ref, *, row_of):
    m = mod_ref[pl.ds(row_of(pl.program_id(0)), 1), :]
    h = _adanorm(x_ref[...], g_ref[...], m[:, 0:D], m[:, D:2 * D]).astype(BF16)
    cs = cs_ref[...]
    cn = cn_ref[...]
    for g in range(FOUR_GROUPS):
        xcs = _dot(h[:, g * FOUR_DG:(g + 1) * FOUR_DG], cs)
        stacked = jnp.concatenate([xcs[:, 0:FOUR_DG], xcs[:, FOUR_DG:]], axis=0).astype(BF16)
        o_ref[:, g * FOUR_DG:(g + 1) * FOUR_DG] = _dot(cn, stacked).astype(BF16)


def _fourier(x, mod_l, g, *, row0, nb, ln, row_of):
    cs, cn = _dft_tables(ln)
    b0 = row0 // ln
    kern = lambda *refs: _fourier_kernel(*refs, row_of=row_of)
    return pl.pallas_call(
        kern,
        grid=(nb,),
        in_specs=[pl.BlockSpec((ln, D), lambda b: (b + b0, 0)), _const_spec((MOD_ROWS, N_MOD * D)),
                  _const_spec((1, D)), _const_spec((FOUR_DG, 2 * FOUR_DG)), _const_spec((ln, 2 * ln))],
        out_specs=pl.BlockSpec((ln, D), lambda b: (b, 0)),
        out_shape=jax.ShapeDtypeStruct((nb * ln, D), BF16),
        compiler_params=_cparams(1),
        name="fourier_%d" % ln,
    )(x, mod_l, g, cs, cn)


def _post_ffn_kernel(*refs, final):
    if final:
        x_ref, o_ref, mod_ref, wo_ref, g_ref, win_ref, wout_ref, gf_ref, y_ref = refs
    else:
        x_ref, o_ref, mod_ref, wo_ref, g_ref, win_ref, wout_ref, y_ref = refs
    m = mod_ref[pl.ds(_mod_row(pl.program_id(0)), 1), :]
    x = x_ref[...] + m[:, 2 * D:3 * D] * _dot(o_ref[...], wo_ref[...])
    h = _adanorm(x, g_ref[...], m[:, 3 * D:4 * D], m[:, 4 * D:5 * D]).astype(BF16)
    acc = None
    for c in range(D_FF // FF_CHUNK):
        gt = _dot(h, win_ref[:, c * FF_CHUNK:(c + 1) * FF_CHUNK])
        up = _dot(h, win_ref[:, D_FF + c * FF_CHUNK:D_FF + (c + 1) * FF_CHUNK])
        part = _dot((_silu(gt) * up).astype(BF16), wout_ref[c * FF_CHUNK:(c + 1) * FF_CHUNK, :])
        acc = part if acc is None else acc + part
    x = x + m[:, 5 * D:6 * D] * acc
    if final:
        x = x * lax.rsqrt(jnp.mean(x * x, axis=-1, keepdims=True) + EPS) * gf_ref[...]
    y_ref[...] = x


def _post_ffn(x, o, mod_l, w_o, g, w_in, w_out, g_final=None):
    tile = pl.BlockSpec((TM, D), lambda i: (i, 0))
    in_specs = [tile, tile, _const_spec((MOD_ROWS, N_MOD * D)), _const_spec((D, D)), _const_spec((1, D)),
                _const_spec((D, 2 * D_FF)), _const_spec((D_FF, D))]
    args = [x, o, mod_l, w_o, g, w_in, w_out]
    if g_final is not None:
        in_specs.append(_const_spec((1, D)))
        args.append(g_final)
    kern = lambda *refs: _post_ffn_kernel(*refs, final=g_final is not None)
    return pl.pallas_call(
        kern,
        grid=(N_TOK // TM,),
        in_specs=in_specs,
        out_specs=tile,
        out_shape=jax.ShapeDtypeStruct((N_TOK, D), F32),
        compiler_params=_cparams(1),
        name="post_ffn",
    )(*args)


def kernel(x_prompt, x_sample, cache_attn_k, cache_attn_v, state_hgrn, c, c_ctx, w_ada, b_ada, g_norm_mix,
           g_norm_ffn, w_qkv_attn, lam_attn, g_subln_attn, w_o_attn, w_in_rec, lb_logits_rec, g_out_rec,
           w_o_rec, w_four, w_ffn_in, w_ffn_out, g_final):
    x = jnp.concatenate([x_prompt.reshape(N_P, D), x_sample.reshape(N_S, D)], axis=0)
    cond = jnp.concatenate([c_ctx.reshape(1, D), c, jnp.zeros((MOD_ROWS - 1 - DEC_BATCH, D), F32)], axis=0)
    mod = _modulation(cond, w_ada, b_ada)
    cache_k = cache_attn_k.reshape(DEC_BATCH, -1, PAST, D)
    cache_v = cache_attn_v.reshape(DEC_BATCH, -1, PAST, D)
    rope_tabs = _rope_tables()
    gla_tabs = _gla_tables()
    gla_tabs = (gla_tabs[0], gla_tabs[1].astype(BF16))
    new_k, new_v, new_s = [], [], []
    for i in range(DEPTH):
        kind, j = i % N_MIXERS, i // N_MIXERS
        g_mix = g_norm_mix[i].reshape(1, D)
        if kind == 0:
            lam_init = 0.8 - 0.6 * math.exp(-0.3 * i)
            w = w_qkv_attn[j].astype(BF16)
            g_sub = g_subln_attn[j].reshape(1, HD)
            qp, kp, vp = _qkv_prompt(x, mod[i], g_mix, w)
            new_k.append(kp.reshape(BATCH, SEQ, HEADS, 2, DH))
            new_v.append(vp.reshape(BATCH, SEQ, HEADS, HD))
            op = _attention(qp, kp, vp, lam_attn[j], g_sub, nb=BATCH, lq=SEQ, lkn=SEQ, tq=SEQ,
                            lam_init=lam_init)
            qs, ks, vs = _qkv_sample(x, mod[i], g_mix, w, rope_tabs)
            os_ = _attention(qs, ks, vs, lam_attn[j], g_sub, nb=DEC_BATCH, lq=DEC_SEQ, lkn=DEC_SEQ, tq=256,
                             lam_init=lam_init, cache=(cache_k, cache_v, j))
            o = jnp.concatenate([op, os_], axis=0)
            w_o = w_o_attn[j]
        elif kind == 1:
            proj = _rec_proj(x, mod[i], g_mix, w_in_rec[j].astype(BF16))
            g_out = g_out_rec[j].reshape(1, HD)
            op, s_ctx = _gla(*proj, lb_logits_rec, g_out, gla_tabs, row0=0, nb=BATCH, ln=SEQ, layer=i)
            new_s.append(s_ctx)
            os_ = _gla(*proj, lb_logits_rec, g_out, gla_tabs, row0=N_P, nb=DEC_BATCH, ln=DEC_SEQ, layer=i,
                       s0=state_hgrn[:, j])[0]
            o = jnp.concatenate([op, os_], axis=0)
            w_o = w_o_rec[j]
        else:
            op = _fourier(x, mod[i], g_mix, row0=0, nb=BATCH, ln=SEQ, row_of=lambda b: 0)
            os_ = _fourier(x, mod[i], g_mix, row0=N_P, nb=DEC_BATCH, ln=DEC_SEQ, row_of=lambda b: 1 + b)
            o = jnp.concatenate([op, os_], axis=0)
            w_o = w_four[j]
        x = _post_ffn(x, o, mod[i], w_o.astype(BF16), g_norm_ffn[i].reshape(1, D), w_ffn_in[i].astype(BF16),
                      w_ffn_out[i].astype(BF16), g_final.reshape(1, D) if i == DEPTH - 1 else None)
    y_prompt = x[:N_P].reshape(BATCH, SEQ, D)
    y_sample = x[N_P:].reshape(DEC_BATCH, DEC_SEQ, D)
    return (y_prompt, y_sample, jnp.stack(new_k, axis=1), jnp.stack(new_v, axis=1), jnp.stack(new_s, axis=1))
```

```python
import math

import numpy as np
import jax
import jax.numpy as jnp
from jax import lax
from jax.experimental import pallas as pl
from jax.experimental.pallas import tpu as pltpu

F32 = jnp.float32
BF16 = jnp.bfloat16

D = 1024
BATCH = 16
SEQ = 256
DEPTH = 4
DEC_BATCH = 4
DEC_SEQ = 1024
PAST = 512
GRID_W = 64
N_MIXERS = 3
HEADS = 8
DH = 64
HD = 128
ROPE_THETA = 10000.0
D_FF = 2816
N_MOD = 6
EPS = 1e-6
FOUR_GROUPS = 4
FOUR_DG = 256

N_P = BATCH * SEQ
N_S = DEC_BATCH * DEC_SEQ
N_TOK = N_P + N_S
TM = 512
NT_P = N_P // TM
NT_S = N_S // TM
TILES_PER_DEC_SEQ = DEC_SEQ // TM
SEQ_PER_TILE = TM // SEQ
MOD_ROWS = 8
FF_CHUNK = 256
ATT_TQ = 256
ATT_HP = 2
GLA_GROUP = 256
GLA_BLK = 8
GLA_LEVELS = (8, 16, 32, 64, 128)
VMEM_LIMIT = 52 * 1024 * 1024


def _cparams(n_axes):
    return pltpu.CompilerParams(dimension_semantics=("arbitrary",) * n_axes,
                                vmem_limit_bytes=VMEM_LIMIT)


def _const_spec(shape):
    nd = len(shape)
    return pl.BlockSpec(shape, lambda *_: (0,) * nd, pipeline_mode=pl.Buffered(1))


def _sigmoid(x):
    return 1.0 / (1.0 + jnp.exp(-x))


def _silu(x):
    return x * _sigmoid(x)


def _adanorm(x, g, shift, scale):
    ms = jnp.mean(x * x, axis=-1, keepdims=True)
    return x * lax.rsqrt(ms + EPS) * (g * (1.0 + scale)) + shift


def _mod_row(tile):
    return jnp.where(tile < NT_P, 0, 1 + (tile - NT_P) // TILES_PER_DEC_SEQ)


def _dot(a, b):
    return jnp.dot(a, b, preferred_element_type=F32)


def _dot_nt(a, b):
    return lax.dot_general(a, b, (((1,), (1,)), ((), ())), preferred_element_type=F32)


def _dot_tn(a, b):
    return lax.dot_general(a, b, (((0,), (0,)), ((), ())), preferred_element_type=F32)


def _split3(x):
    hi = x.astype(BF16)
    r1 = x - hi.astype(F32)
    mid = r1.astype(BF16)
    lo = (r1 - mid.astype(F32)).astype(BF16)
    return hi, mid, lo


MOD_TN = 1536


def _mod_kernel(cond_ref, w_ref, b_ref, o_ref):
    s = _silu(cond_ref[...])
    s_hi = s.astype(BF16)
    s_lo = (s - s_hi.astype(F32)).astype(BF16)
    w = w_ref[0]
    w_hi = w.astype(BF16)
    w_lo = (w - w_hi.astype(F32)).astype(BF16)
    o_ref[0] = _dot(s_hi, w_hi) + _dot(s_lo, w_hi) + _dot(s_hi, w_lo) + b_ref[0]


def _modulation(cond, w_ada, b_ada):
    return pl.pallas_call(
        _mod_kernel,
        grid=(DEPTH, N_MOD * D // MOD_TN),
        in_specs=[
            pl.BlockSpec((MOD_ROWS, D), lambda l, n: (0, 0)),
            pl.BlockSpec((1, D, MOD_TN), lambda l, n: (l, 0, n)),
            pl.BlockSpec((1, 1, MOD_TN), lambda l, n: (l, 0, n)),
        ],
        out_specs=pl.BlockSpec((1, MOD_ROWS, MOD_TN), lambda l, n: (l, 0, n)),
        out_shape=jax.ShapeDtypeStruct((DEPTH, MOD_ROWS, N_MOD * D), F32),
        compiler_params=_cparams(2),
        name="modulation",
    )(cond, w_ada, b_ada.reshape(DEPTH, 1, N_MOD * D))


def _rope_tables():
    t = np.arange(DEC_SEQ)
    row, col = t // GRID_W, t % GRID_W
    lane = np.arange(HD) % DH
    use_row = lane < DH // 2
    idx = (lane % (DH // 2)) % (DH // 4)
    inv = ROPE_THETA ** (-idx.astype(np.float64) / (DH // 4))
    pos = np.where(use_row[None, :], row[:, None], col[:, None]).astype(np.float64)
    ang = pos * inv[None, :]
    first = (lane % (DH // 2)) < DH // 4
    cos = np.cos(ang)
    sin = np.sin(ang)
    sin_minus = np.where(first[None, :], -sin, 0.0)
    sin_plus = np.where(first[None, :], 0.0, sin)
    return (jnp.asarray(cos, F32), jnp.asarray(sin_minus, F32), jnp.asarray(sin_plus, F32))


def _rope_tile(t, cos, sin_minus, sin_plus):
    return (t * cos + pltpu.roll(t, HD - DH // 4, 1) * sin_minus
            + pltpu.roll(t, DH // 4, 1) * sin_plus)


def _lambda(lam_ref, lam_init):
    lp = lam_ref[...]
    return (jnp.exp(jnp.sum(lp[0:1] * lp[1:2], axis=-1, keepdims=True))
            - jnp.exp(jnp.sum(lp[2:3] * lp[3:4], axis=-1, keepdims=True)) + lam_init)


def _diff_attn_head(q, score_fns, values, lam, g_scaled):
    tq = q.shape[0]
    lane = lax.broadcasted_iota(jnp.int32, q.shape, 1)
    zero = jnp.zeros_like(q)
    qq = jnp.concatenate([jnp.where(lane < DH, q, zero), jnp.where(lane >= DH, q, zero)], axis=0)
    scores = [fn(qq) for fn in score_fns]
    mx = scores[0].max(axis=-1, keepdims=True)
    for s in scores[1:]:
        mx = jnp.maximum(mx, s.max(axis=-1, keepdims=True))
    es = [jnp.exp(s - mx) for s in scores]
    den = es[0].sum(axis=-1, keepdims=True)
    for e in es[1:]:
        den = den + e.sum(axis=-1, keepdims=True)
    inv = 1.0 / den
    w0 = inv[:tq]
    w1 = lam * inv[tq:]
    o = None
    for e, v in zip(es, values):
        c = _dot((e[:tq] * w0 - e[tq:] * w1).astype(BF16), v)
        o = c if o is None else o + c
    o = o * lax.rsqrt(jnp.mean(o * o, axis=-1, keepdims=True) + EPS) * g_scaled
    return o.astype(BF16)


def _attn_prompt_kernel(x_ref, mod_ref, g_ref, wq_ref, wkt_ref, wv_ref, lam_ref, gsub_ref,
                        kt_ref, v_ref, o_ref, *, lam_init):
    m = mod_ref[pl.ds(0, 1), :]
    h = _adanorm(x_ref[...], g_ref[...], m[:, 0:D], m[:, D:2 * D]).astype(BF16)
    lam = _lambda(lam_ref, lam_init)
    g_scaled = gsub_ref[...] * (1.0 - lam_init)
    q = (_dot(h, wq_ref[...]) * DH ** -0.5).astype(BF16)
    v = _dot(h, wv_ref[...])
    v_ref[...] = v
    vb = v.astype(BF16)
    for s in range(SEQ_PER_TILE):
        rows = slice(s * SEQ, (s + 1) * SEQ)
        kt = _dot_nt(wkt_ref[...], h[rows])
        kt_ref[s] = kt
        ktb = kt.astype(BF16)
        for hh in range(HEADS):
            cols = slice(hh * HD, (hh + 1) * HD)
            kth = ktb[cols, :]
            o_ref[rows, cols] = _diff_attn_head(q[rows, cols], [lambda qq, kth=kth: _dot(qq, kth)],
                                                [vb[rows, cols]], lam, g_scaled)


def _attn_prompt(x, mod_l, g, wq, wkt, wv, lam_p, g_sub, lam_init):
    tile = pl.BlockSpec((TM, D), lambda i: (i, 0))
    kern = lambda *refs: _attn_prompt_kernel(*refs, lam_init=lam_init)
    return pl.pallas_call(
        kern,
        grid=(NT_P,),
        in_specs=[tile, _const_spec((MOD_ROWS, N_MOD * D)), _const_spec((1, D)), _const_spec((D, D)),
                  _const_spec((D, D)), _const_spec((D, D)), _const_spec((4, DH)), _const_spec((1, HD))],
        out_specs=[pl.BlockSpec((SEQ_PER_TILE, D, SEQ), lambda i: (i, 0, 0)), tile, tile],
        out_shape=[jax.ShapeDtypeStruct((BATCH, D, SEQ), F32), jax.ShapeDtypeStruct((N_P, D), F32),
                   jax.ShapeDtypeStruct((N_P, D), BF16)],
        compiler_params=_cparams(1),
        name="attn_prompt",
    )(x, mod_l, g, wq, wkt, wv, lam_p, g_sub)


def _qkv_sample_kernel(x_ref, mod_ref, g_ref, w_ref, cos_ref, sm_ref, sp_ref, q_ref, k_ref, v_ref):
    i = pl.program_id(0)
    m = mod_ref[pl.ds(1 + i // TILES_PER_DEC_SEQ, 1), :]
    h = _adanorm(x_ref[...], g_ref[...], m[:, 0:D], m[:, D:2 * D]).astype(BF16)
    cos, sm, sp = cos_ref[...], sm_ref[...], sp_ref[...]
    for o_ref, c0, scale in ((q_ref, 0, DH ** -0.5), (k_ref, D, None)):
        y = _dot(h, w_ref[:, c0:c0 + D])
        if scale is not None:
            y = y * scale
        for hh in range(HEADS):
            t = _rope_tile(y[:, hh * HD:(hh + 1) * HD], cos, sm, sp)
            o_ref[:, hh * HD:(hh + 1) * HD] = t.astype(BF16)
    v_ref[...] = _dot(h, w_ref[:, 2 * D:3 * D]).astype(BF16)


def _qkv_sample(x, tile0, mod_l, g, w, tabs):
    tile = pl.BlockSpec((TM, D), lambda i: (i, 0))
    tab = pl.BlockSpec((TM, HD), lambda i: (i % TILES_PER_DEC_SEQ, 0))
    return pl.pallas_call(
        _qkv_sample_kernel,
        grid=(NT_S,),
        in_specs=[pl.BlockSpec((TM, D), lambda i: (i + tile0, 0)), _const_spec((MOD_ROWS, N_MOD * D)),
                  _const_spec((1, D)), _const_spec((D, 3 * D)), tab, tab, tab],
        out_specs=[tile, tile, tile],
        out_shape=[jax.ShapeDtypeStruct((N_S, D), BF16)] * 3,
        compiler_params=_cparams(1),
        name="qkv_sample",
    )(x, mod_l, g, w, *tabs)


def _attn_sample_kernel(*refs, lam_init):
    q_ref, kn_ref, vn_ref, kct_ref, vc_ref, lam_ref, gsub_ref, o_ref = refs
    lam = _lambda(lam_ref, lam_init)
    g_scaled = gsub_ref[...] * (1.0 - lam_init)
    head0 = pl.program_id(1) * ATT_HP
    for hh in range(ATT_HP):
        cols = slice(hh * HD, (hh + 1) * HD)
        kn = kn_ref[:, cols]
        kct = kct_ref[hh].astype(BF16)
        vc = vc_ref[:, pl.ds(head0 + hh, 1), :].reshape(PAST, HD).astype(BF16)
        o_ref[:, cols] = _diff_attn_head(
            q_ref[:, cols], [lambda qq, kn=kn: _dot_nt(qq, kn), lambda qq, kct=kct: _dot(qq, kct)],
            [vn_ref[:, cols], vc], lam, g_scaled)


def _attn_sample(q, kn, vn, cache_kt, cache_v, j, lam_p, g_sub, lam_init):
    nq = DEC_SEQ // ATT_TQ
    w = ATT_HP * HD
    in_specs = [
        pl.BlockSpec((ATT_TQ, w), lambda b, h, i: (b * nq + i, h)),
        pl.BlockSpec((DEC_SEQ, w), lambda b, h, i: (b, h)),
        pl.BlockSpec((DEC_SEQ, w), lambda b, h, i: (b, h)),
        pl.BlockSpec((None, None, ATT_HP, HD, PAST), lambda b, h, i: (b, j, h, 0, 0)),
    ]
    in_specs += [pl.BlockSpec((None, None, PAST, HEADS, HD), lambda b, h, i: (b, j, 0, 0, 0)),
                 pl.BlockSpec((4, DH), lambda b, h, i: (0, 0)), pl.BlockSpec((1, HD), lambda b, h, i: (0, 0))]
    kern = lambda *refs: _attn_sample_kernel(*refs, lam_init=lam_init)
    return pl.pallas_call(
        kern,
        grid=(DEC_BATCH, HEADS // ATT_HP, nq),
        in_specs=in_specs,
        out_specs=pl.BlockSpec((ATT_TQ, w), lambda b, h, i: (b * nq + i, h)),
        out_shape=jax.ShapeDtypeStruct((N_S, D), BF16),
        compiler_params=_cparams(3),
        name="attn_sample",
    )(q, kn, vn, cache_kt, cache_v, lam_p, g_sub)


def _rec_proj_kernel(x_ref, mod_ref, g_ref, w_ref, q_ref, v_ref, gate_ref, zf_ref, zb_ref):
    m = mod_ref[pl.ds(_mod_row(pl.program_id(0)), 1), :]
    h = _adanorm(x_ref[...], g_ref[...], m[:, 0:D], m[:, D:2 * D]).astype(BF16)
    q_ref[...] = _silu(_dot(h, w_ref[:, 0:D])).astype(BF16)
    v_ref[...] = _dot(h, w_ref[:, D:2 * D]).astype(BF16)
    gate_ref[...] = _silu(_dot(h, w_ref[:, 2 * D:3 * D])).astype(BF16)
    zf_ref[...] = _dot(h, w_ref[:, 3 * D:4 * D])
    zb_ref[...] = _dot(h, w_ref[:, 4 * D:5 * D])


def _rec_proj(x, mod_l, g, w):
    tile = pl.BlockSpec((TM, D), lambda i: (i, 0))
    return pl.pallas_call(
        _rec_proj_kernel,
        grid=(N_TOK // TM,),
        in_specs=[tile, _const_spec((MOD_ROWS, N_MOD * D)), _const_spec((1, D)), _const_spec((D, 5 * D))],
        out_specs=[tile] * 5,
        out_shape=[jax.ShapeDtypeStruct((N_TOK, D), BF16)] * 3 + [jax.ShapeDtypeStruct((N_TOK, D), F32)] * 2,
        compiler_params=_cparams(1),
        name="rec_proj",
    )(x, mod_l, g, w)


def _gla_tables():
    t = np.arange(GLA_GROUP)[:, None]
    s = np.arange(GLA_GROUP)[None, :]
    ids = np.full((GLA_GROUP, GLA_GROUP), -1, np.int32)
    for l, b in reversed(list(enumerate(GLA_LEVELS, 1))):
        ids = np.where((t // (2 * b) == s // (2 * b)) & (s < t), l, ids)
    ids = np.where((t // GLA_BLK == s // GLA_BLK) & (s <= t), 0, ids)
    ids = np.stack([ids, ids.T]).astype(np.int32)
    e = np.zeros((GLA_BLK * HD, GLA_GROUP), np.float32)
    for j in range(GLA_BLK):
        e[j * HD:(j + 1) * HD, j::GLA_BLK] = 1.0
    return jnp.asarray(ids), jnp.asarray(e)


def _bcast_row(x, blk, idx):
    n, w = x.shape
    r = x.reshape(n // blk, blk, w)[:, idx:idx + 1, :]
    return jnp.broadcast_to(r, (n // blk, blk, w)).reshape(n, w)


def _gla_group(q, v, z, lb, ids, e_mat, st, backward):
    n = GLA_GROUP
    sig = _sigmoid(z)
    f = lb + (1.0 - lb) * sig
    k = (1.0 - lb) * (1.0 - sig)
    lf = jnp.log(f)
    row = lax.broadcasted_iota(jnp.int32, (n, n), 0)
    col = lax.broadcasted_iota(jnp.int32, (n, n), 1)
    tri = jnp.where((col >= row) if backward else (col <= row), 1.0, 0.0).astype(BF16)
    c3 = _dot(tri, jnp.concatenate(_split3(lf), axis=1))
    cum = c3[:, 0:HD] + c3[:, HD:2 * HD] + c3[:, 2 * HD:3 * HD]

    prods = []
    for j in range(GLA_BLK):
        kj = _bcast_row(k, GLA_BLK, j)
        cj = _bcast_row(cum, GLA_BLK, j)
        prods.append((q * kj * jnp.exp(jnp.minimum(cum - cj, 0.0))).astype(BF16))
    a = jnp.where(ids == 0, _dot(jnp.concatenate(prods, axis=1), e_mat), 0.0)

    for l, b in enumerate(GLA_LEVELS, 1):
        edge = _bcast_row(cum, 2 * b, b if backward else b - 1)
        ql = (q * jnp.exp(jnp.minimum(cum - edge, 0.0))).astype(BF16)
        kl = (k * jnp.exp(jnp.minimum(edge - cum, 0.0))).astype(BF16)
        a = jnp.where(ids == l, _dot_nt(ql, kl), a)
    o = _dot(a.astype(BF16), v)

    total = cum[0:1, :] if backward else cum[n - 1:n, :]
    if st is not None:
        o = o + _dot_nt((q * jnp.exp(cum)).astype(BF16), st.astype(BF16))
    kdec = (k * jnp.exp(total - cum)).astype(BF16)
    upd = _dot_tn(v, kdec)
    st_new = upd if st is None else st * jnp.exp(total) + upd
    return o, st_new


def _lower_bound(x, layer):
    rows = [x[l:l + 1, :] for l in range(DEPTH)]
    mx = rows[0]
    for r in rows[1:]:
        mx = jnp.maximum(mx, r)
    ex = [jnp.exp(r - mx) for r in rows]
    tot = ex[0]
    for e in ex[1:]:
        tot = tot + e
    acc = ex[0] * 0.0
    for l in range(1, layer + 1):
        acc = acc + ex[l]
    return acc / tot


def _gla_kernel(*refs, n_groups, layer, has_state):
    if has_state:
        (q_ref, v_ref, gate_ref, zf_ref, zb_ref, lbl_ref, g_ref, ids_ref, e_ref, s0_ref,
         o_ref, acc_ref) = refs
    else:
        (q_ref, v_ref, gate_ref, zf_ref, zb_ref, lbl_ref, g_ref, ids_ref, e_ref,
         o_ref, sout_ref, acc_ref) = refs
    e_mat = e_ref[...]
    z_refs = (zf_ref, zb_ref)
    lbs = [_lower_bound(lbl_ref[d], layer) for d in range(2)]

    def step(gi, sts):
        new = []
        for d in range(2):
            g = gi if d == 0 else n_groups - 1 - gi
            start = g * GLA_GROUP
            rows = pl.ds(start if isinstance(start, int) else pl.multiple_of(start, GLA_GROUP), GLA_GROUP)
            o, st = _gla_group(q_ref[rows, :].astype(F32), v_ref[rows, :], z_refs[d][rows, :], lbs[d],
                               ids_ref[d], e_mat, sts[d], backward=(d == 1))
            acc_ref[rows, :] = acc_ref[rows, :] + o
            new.append(st)
        return tuple(new)

    acc_ref[...] = jnp.zeros_like(acc_ref)
    if has_state:
        init = tuple(s0_ref[d].T for d in range(2))
        lax.fori_loop(0, n_groups, step, init)
    else:
        sts = step(0, (None, None))
        for d in range(2):
            sout_ref[d] = sts[d].T
    o = acc_ref[...]
    y = o * lax.rsqrt(jnp.mean(o * o, axis=-1, keepdims=True) + EPS) * g_ref[...]
    o_ref[...] = (y * gate_ref[...].astype(F32)).astype(BF16)


def _gla(q, v, gate, zf, zb, lb_logits, g_out, tabs, *, row0, nb, ln, layer, s0=None):
    ids, e_mat = tabs
    assert ln % GLA_GROUP == 0 and (s0 is not None or ln == GLA_GROUP)
    b0 = row0 // ln
    seq = pl.BlockSpec((ln, HD), lambda b, h: (b + b0, h))
    in_specs = [seq] * 5 + [
        pl.BlockSpec((2, DEPTH, HD), lambda b, h: (0, 0, h)),
        pl.BlockSpec((1, HD), lambda b, h: (0, 0)),
        pl.BlockSpec((2, GLA_GROUP, GLA_GROUP), lambda b, h: (0, 0, 0)),
        pl.BlockSpec((GLA_BLK * HD, GLA_GROUP), lambda b, h: (0, 0)),
    ]
    args = [q, v, gate, zf, zb, lb_logits, g_out, ids, e_mat]
    out_specs = [pl.BlockSpec((ln, HD), lambda b, h: (b, h))]
    out_shape = [jax.ShapeDtypeStruct((nb * ln, D), BF16)]
    state_spec = pl.BlockSpec((None, 2, None, HD, HD), lambda b, h: (b, 0, h, 0, 0))
    if s0 is not None:
        in_specs.append(state_spec)
        args.append(s0)
    else:
        out_specs.append(state_spec)
        out_shape.append(jax.ShapeDtypeStruct((nb, 2, HEADS, HD, HD), F32))
    kern = lambda *refs: _gla_kernel(*refs, n_groups=ln // GLA_GROUP, layer=layer, has_state=s0 is not None)
    return pl.pallas_call(
        kern,
        grid=(nb, HEADS),
        in_specs=in_specs,
        out_specs=out_specs,
        out_shape=out_shape,
        scratch_shapes=[pltpu.VMEM((ln, HD), F32)],
        compiler_params=_cparams(2),
        name="gla_sample" if s0 is not None else "gla_prompt",
    )(*args)


def _dft_tables(n):
    c = np.arange(FOUR_DG)
    ang_c = 2.0 * np.pi * ((c[:, None] * c[None, :]) % FOUR_DG) / FOUR_DG
    cs = np.concatenate([np.cos(ang_c), np.sin(ang_c)], axis=1)
    t = np.arange(n)
    ang_n = 2.0 * np.pi * ((t[:, None] * t[None, :]) % n) / n
    cn = np.concatenate([np.cos(ang_n), -np.sin(ang_n)], axis=1) / math.sqrt(n * FOUR_DG)
    return jnp.asarray(cs, F32).astype(BF16), jnp.asarray(cn, F32).astype(BF16)


def _fourier_kernel(x_ref, mod_ref, g_ref, cs_ref, cn_ref, o_ref, *, row_of):
    m = mod_ref[pl.ds(row_of(pl.program_id(0)), 1), :]
    h = _adanorm(x_ref[...], g_ref[...], m[:, 0:D], m[:, D:2 * D]).astype(BF16)
    cs = cs_ref[...]
    cn = cn_ref[...]
    for g in range(FOUR_GROUPS):
        xcs = _dot(h[:, g * FOUR_DG:(g + 1) * FOUR_DG], cs)
        stacked = jnp.concatenate([xcs[:, 0:FOUR_DG], xcs[:, FOUR_DG:]], axis=0).astype(BF16)
        o_ref[:, g * FOUR_DG:(g + 1) * FOUR_DG] = _dot(cn, stacked).astype(BF16)


def _fourier(x, mod_l, g, *, row0, nb, ln, row_of):
    cs, cn = _dft_tables(ln)
    b0 = row0 // ln
    kern = lambda *refs: _fourier_kernel(*refs, row_of=row_of)
    return pl.pallas_call(
        kern,
        grid=(nb,),
        in_specs=[pl.BlockSpec((ln, D), lambda b: (b + b0, 0)), _const_spec((MOD_ROWS, N_MOD * D)),
                  _const_spec((1, D)), _const_spec((FOUR_DG, 2 * FOUR_DG)), _const_spec((ln, 2 * ln))],
        out_specs=pl.BlockSpec((ln, D), lambda b: (b, 0)),
        out_shape=jax.ShapeDtypeStruct((nb * ln, D), BF16),
        compiler_params=_cparams(1),
        name="fourier_%d" % ln,
    )(x, mod_l, g, cs, cn)


def _post_ffn_kernel(*refs, split_x, final):
    refs = list(refs)
    x_refs = [refs.pop(0) for _ in range(2 if split_x else 1)]
    op_ref, os_ref, mod_ref, wo_ref, g_ref, win_ref, wout_ref = refs[:7]
    rest = refs[7:]
    i = pl.program_id(0)
    first = i < NT_P
    m = mod_ref[pl.ds(_mod_row(i), 1), :]
    x = jnp.where(first, x_refs[0][...], x_refs[1][...]) if split_x else x_refs[0][...]
    o = jnp.where(first, op_ref[...], os_ref[...])
    x = x + m[:, 2 * D:3 * D] * _dot(o, wo_ref[...])
    h = _adanorm(x, g_ref[...], m[:, 3 * D:4 * D], m[:, 4 * D:5 * D]).astype(BF16)
    acc = None
    for c in range(D_FF // FF_CHUNK):
        gt = _dot(h, win_ref[:, c * FF_CHUNK:(c + 1) * FF_CHUNK])
        up = _dot(h, win_ref[:, D_FF + c * FF_CHUNK:D_FF + (c + 1) * FF_CHUNK])
        part = _dot((_silu(gt) * up).astype(BF16), wout_ref[c * FF_CHUNK:(c + 1) * FF_CHUNK, :])
        acc = part if acc is None else acc + part
    x = x + m[:, 5 * D:6 * D] * acc
    if not final:
        rest[0][...] = x
        return
    gf_ref, yp_ref, ys_ref = rest
    y = x * lax.rsqrt(jnp.mean(x * x, axis=-1, keepdims=True) + EPS) * gf_ref[...]

    @pl.when(first)
    def _():
        yp_ref[...] = y

    @pl.when(jnp.logical_not(first))
    def _():
        ys_ref[...] = y


def _post_ffn(xs, op, os_, mod_l, w_o, g, w_in, w_out, g_final=None):
    tile = pl.BlockSpec((TM, D), lambda i: (i, 0))
    p_tile = pl.BlockSpec((TM, D), lambda i: (jnp.minimum(i, NT_P - 1), 0))
    s_tile = pl.BlockSpec((TM, D), lambda i: (jnp.maximum(i - NT_P, 0), 0))
    split_x = len(xs) == 2
    final = g_final is not None
    in_specs = ([p_tile, s_tile] if split_x else [tile]) + [
        p_tile, s_tile, _const_spec((MOD_ROWS, N_MOD * D)), _const_spec((D, D)), _const_spec((1, D)),
        _const_spec((D, 2 * D_FF)), _const_spec((D_FF, D))]
    args = list(xs) + [op, os_, mod_l, w_o, g, w_in, w_out]
    if final:
        in_specs.append(_const_spec((1, D)))
        args.append(g_final)
        out_specs = [p_tile, s_tile]
        out_shape = [jax.ShapeDtypeStruct((N_P, D), F32), jax.ShapeDtypeStruct((N_S, D), F32)]
    else:
        out_specs = tile
        out_shape = jax.ShapeDtypeStruct((N_TOK, D), F32)
    kern = lambda *refs: _post_ffn_kernel(*refs, split_x=split_x, final=final)
    return pl.pallas_call(
        kern,
        grid=(N_TOK // TM,),
        in_specs=in_specs,
        out_specs=out_specs,
        out_shape=out_shape,
        compiler_params=_cparams(1),
        name="post_ffn",
    )(*args)


def kernel(x_prompt, x_sample, cache_attn_k, cache_attn_v, state_hgrn, c, c_ctx, w_ada, b_ada, g_norm_mix,
           g_norm_ffn, w_qkv_attn, lam_attn, g_subln_attn, w_o_attn, w_in_rec, lb_logits_rec, g_out_rec,
           w_o_rec, w_four, w_ffn_in, w_ffn_out, g_final):
    xs = (x_prompt.reshape(N_P, D), x_sample.reshape(N_S, D))
    cond = jnp.concatenate([c_ctx.reshape(1, D), c, jnp.zeros((MOD_ROWS - 1 - DEC_BATCH, D), F32)], axis=0)
    mod = _modulation(cond, w_ada, b_ada)
    cache_kt = jnp.transpose(cache_attn_k, (0, 1, 3, 4, 5, 2)).reshape(DEC_BATCH, -1, HEADS, HD, PAST)
    rope_tabs = _rope_tables()
    gla_tabs = _gla_tables()
    gla_tabs = (gla_tabs[0], gla_tabs[1].astype(BF16))
    new_kt, new_v, new_s = [], [], []
    for i in range(DEPTH):
        kind, j = i % N_MIXERS, i // N_MIXERS
        g_mix = g_norm_mix[i].reshape(1, D)
        x_p = xs[0]
        x_s, s_tile0 = (xs[1], 0) if len(xs) == 2 else (xs[0], NT_P)
        if kind == 0:
            lam_init = 0.8 - 0.6 * math.exp(-0.3 * i)
            w = w_qkv_attn[j]
            g_sub = g_subln_attn[j].reshape(1, HD)
            kt, vp, op = _attn_prompt(x_p, mod[i], g_mix, w[:, 0:D].astype(BF16), w[:, D:2 * D].T.astype(BF16),
                                      w[:, 2 * D:].astype(BF16), lam_attn[j], g_sub, lam_init)
            new_kt.append(kt)
            new_v.append(vp.reshape(BATCH, SEQ, D))
            qs, ks, vs = _qkv_sample(x_s, s_tile0, mod[i], g_mix, w.astype(BF16), rope_tabs)
            os_ = _attn_sample(qs, ks, vs, cache_kt, cache_attn_v, j, lam_attn[j], g_sub, lam_init)
            w_o = w_o_attn[j]
        elif kind == 1:
            proj = _rec_proj(xs[0], mod[i], g_mix, w_in_rec[j].astype(BF16))
            g_out = g_out_rec[j].reshape(1, HD)
            op, s_ctx = _gla(*proj, lb_logits_rec, g_out, gla_tabs, row0=0, nb=BATCH, ln=SEQ, layer=i)
            new_s.append(s_ctx)
            os_ = _gla(*proj, lb_logits_rec, g_out, gla_tabs, row0=N_P, nb=DEC_BATCH, ln=DEC_SEQ, layer=i,
                       s0=state_hgrn[:, j])[0]
            w_o = w_o_rec[j]
        else:
            op = _fourier(xs[0], mod[i], g_mix, row0=0, nb=BATCH, ln=SEQ, row_of=lambda b: 0)
            os_ = _fourier(xs[0], mod[i], g_mix, row0=N_P, nb=DEC_BATCH, ln=DEC_SEQ, row_of=lambda b: 1 + b)
            w_o = w_four[j]
        out = _post_ffn(xs, op, os_, mod[i], w_o.astype(BF16), g_norm_ffn[i].reshape(1, D),
                        w_ffn_in[i].astype(BF16), w_ffn_out[i].astype(BF16),
                        g_final.reshape(1, D) if i == DEPTH - 1 else None)
        xs = tuple(out) if i == DEPTH - 1 else (out,)
    y_prompt = xs[0].reshape(BATCH, SEQ, D)
    y_sample = xs[1].reshape(DEC_BATCH, DEC_SEQ, D)
    new_k = jnp.transpose(jnp.stack(new_kt, axis=1).reshape(BATCH, -1, HEADS, 2, DH, SEQ), (0, 1, 5, 2, 3, 4))
    new_v = jnp.stack(new_v, axis=1).reshape(BATCH, -1, SEQ, HEADS, HD)
    return (y_prompt, y_sample, new_k, new_v, jnp.stack(new_s, axis=1))
```

```python
import math

import numpy as np
import jax
import jax.numpy as jnp
from jax import lax
from jax.experimental import pallas as pl
from jax.experimental.pallas import tpu as pltpu

F32 = jnp.float32
BF16 = jnp.bfloat16

D = 1024
BATCH = 16
SEQ = 256
DEPTH = 4
DEC_BATCH = 4
DEC_SEQ = 1024
PAST = 512
GRID_W = 64
N_MIXERS = 3
HEADS = 8
DH = 64
HD = 128
ROPE_THETA = 10000.0
D_FF = 2816
N_MOD = 6
EPS = 1e-6
LOG2E = 1.0 / math.log(2.0)
FOUR_GROUPS = 4
FOUR_DG = 256

N_P = BATCH * SEQ
N_S = DEC_BATCH * DEC_SEQ
N_TOK = N_P + N_S
TM = 512
NT_P = N_P // TM
NT_S = N_S // TM
TILES_PER_DEC_SEQ = DEC_SEQ // TM
SEQ_PER_TILE = TM // SEQ
MOD_ROWS = 8
FF_CHUNK = 256
ATT_TQ = 256
ATT_HP = 2
GLA_GROUP = 256
GLA_BLK = 8
GLA_LEVELS = (8, 16, 32, 64, 128)
VMEM_LIMIT = 60 * 1024 * 1024


def _cparams(n_axes):
    return pltpu.CompilerParams(dimension_semantics=("arbitrary",) * n_axes,
                                vmem_limit_bytes=VMEM_LIMIT)


def _const_spec(shape):
    nd = len(shape)
    return pl.BlockSpec(shape, lambda *_: (0,) * nd, pipeline_mode=pl.Buffered(1))


def _sigmoid(x):
    return 1.0 / (1.0 + jnp.exp(-x))


def _silu(x):
    return x * _sigmoid(x)


def _adanorm(x, g, shift, scale):
    ms = jnp.mean(x * x, axis=-1, keepdims=True)
    return x * lax.rsqrt(ms + EPS) * (g * (1.0 + scale)) + shift


def _mod_row(tile):
    return jnp.where(tile < NT_P, 0, 1 + (tile - NT_P) // TILES_PER_DEC_SEQ)


def _dot(a, b):
    return jnp.dot(a, b, preferred_element_type=F32)


def _dot_nt(a, b):
    return lax.dot_general(a, b, (((1,), (1,)), ((), ())), preferred_element_type=F32)


def _dot_tn(a, b):
    return lax.dot_general(a, b, (((0,), (0,)), ((), ())), preferred_element_type=F32)


def _split3(x):
    hi = x.astype(BF16)
    r1 = x - hi.astype(F32)
    mid = r1.astype(BF16)
    lo = (r1 - mid.astype(F32)).astype(BF16)
    return hi, mid, lo


class _WeightStream:
    def __init__(self, jobs, stage_ref, sem_ref):
        self.jobs, self.stage, self.sem = jobs, stage_ref, sem_ref
        self.depth = stage_ref.shape[0]

    def _copy(self, k):
        slot = k % self.depth
        return pltpu.make_async_copy(self.jobs[k][0], self.stage.at[slot], self.sem.at[slot])

    def prime(self):
        for k in range(min(self.depth, len(self.jobs))):
            self._copy(k).start()

    def take(self, k):
        self._copy(k).wait()
        self.jobs[k][1][...] = self.stage[k % self.depth].astype(BF16)
        if k + self.depth < len(self.jobs):
            self._copy(k + self.depth).start()


def _stream_scratch(piece_shape, depth):
    return [pltpu.VMEM((depth,) + tuple(piece_shape), F32), pltpu.SemaphoreType.DMA((depth,))]


def _column_jobs(w_hbm, index, w_bf, width):
    return [(w_hbm.at[index, :, pl.ds(c, width)], w_bf.at[:, pl.ds(c, width)])
            for c in range(0, w_bf.shape[1], width)]


def _first_step_or_not(body):
    first = pl.program_id(0) == 0
    pl.when(first)(lambda: body(True))
    pl.when(jnp.logical_not(first))(lambda: body(False))


HBM_SPEC = pl.BlockSpec(memory_space=pl.ANY)
W_PIECE = 512
W_DEPTH = 3


MOD_TN = 1536


def _mod_kernel(cond_ref, w_ref, b_ref, o_ref):
    s = _silu(cond_ref[...])
    s_hi = s.astype(BF16)
    s_lo = (s - s_hi.astype(F32)).astype(BF16)
    w = w_ref[0]
    w_hi = w.astype(BF16)
    w_lo = (w - w_hi.astype(F32)).astype(BF16)
    o_ref[0] = _dot(s_hi, w_hi) + _dot(s_lo, w_hi) + _dot(s_hi, w_lo) + b_ref[0]


def _modulation(cond, w_ada, b_ada):
    return pl.pallas_call(
        _mod_kernel,
        grid=(DEPTH, N_MOD * D // MOD_TN),
        in_specs=[
            pl.BlockSpec((MOD_ROWS, D), lambda l, n: (0, 0)),
            pl.BlockSpec((1, D, MOD_TN), lambda l, n: (l, 0, n)),
            pl.BlockSpec((1, 1, MOD_TN), lambda l, n: (l, 0, n)),
        ],
        out_specs=pl.BlockSpec((1, MOD_ROWS, MOD_TN), lambda l, n: (l, 0, n)),
        out_shape=jax.ShapeDtypeStruct((DEPTH, MOD_ROWS, N_MOD * D), F32),
        compiler_params=_cparams(2),
        name="modulation",
    )(cond, w_ada, b_ada.reshape(DEPTH, 1, N_MOD * D))


def _rope_tables():
    t = np.arange(DEC_SEQ)
    row, col = t // GRID_W, t % GRID_W
    lane = np.arange(HD) % DH
    use_row = lane < DH // 2
    idx = (lane % (DH // 2)) % (DH // 4)
    inv = ROPE_THETA ** (-idx.astype(np.float64) / (DH // 4))
    pos = np.where(use_row[None, :], row[:, None], col[:, None]).astype(np.float64)
    ang = pos * inv[None, :]
    first = (lane % (DH // 2)) < DH // 4
    cos = np.cos(ang)
    sin = np.sin(ang)
    sin_minus = np.where(first[None, :], -sin, 0.0)
    sin_plus = np.where(first[None, :], 0.0, sin)
    return (jnp.asarray(cos, F32), jnp.asarray(sin_minus, F32), jnp.asarray(sin_plus, F32))


def _rope_tile(t, cos, sin_minus, sin_plus):
    return (t * cos + pltpu.roll(t, HD - DH // 4, 1) * sin_minus
            + pltpu.roll(t, DH // 4, 1) * sin_plus)


def _lambda(lam_ref, lam_init):
    lp = lam_ref[...]
    return (jnp.exp(jnp.sum(lp[0:1] * lp[1:2], axis=-1, keepdims=True))
            - jnp.exp(jnp.sum(lp[2:3] * lp[3:4], axis=-1, keepdims=True)) + lam_init)


def _diff_attn_head(q, score_fns, values, lam, g_scaled):
    tq = q.shape[0]
    lane = lax.broadcasted_iota(jnp.int32, q.shape, 1)
    zero = jnp.zeros_like(q)
    qq = jnp.concatenate([jnp.where(lane < DH, q, zero), jnp.where(lane >= DH, q, zero)], axis=0)
    scores = [fn(qq) for fn in score_fns]
    mx = scores[0].max(axis=-1, keepdims=True)
    for s in scores[1:]:
        mx = jnp.maximum(mx, s.max(axis=-1, keepdims=True))
    es = [jnp.exp(s - mx) for s in scores]
    den = es[0].sum(axis=-1, keepdims=True)
    for e in es[1:]:
        den = den + e.sum(axis=-1, keepdims=True)
    inv = 1.0 / den
    w0 = inv[:tq]
    w1 = lam * inv[tq:]
    o = None
    for e, v in zip(es, values):
        c = _dot((e[:tq] * w0 - e[tq:] * w1).astype(BF16), v)
        o = c if o is None else o + c
    o = o * lax.rsqrt(jnp.mean(o * o, axis=-1, keepdims=True) + EPS) * g_scaled
    return o.astype(BF16)


def _attn_prompt_kernel(x_ref, mod_ref, g_ref, w_hbm, lam_ref, gsub_ref, kt_ref, v_ref, o_ref,
                        w_bf, stage, sem, *, j, lam_init):
    stream = _WeightStream(_column_jobs(w_hbm, j, w_bf, W_PIECE), stage, sem)
    per_mat = D // W_PIECE

    def body(first):
        if first:
            stream.prime()
        m = mod_ref[pl.ds(0, 1), :]
        h = _adanorm(x_ref[...], g_ref[...], m[:, 0:D], m[:, D:2 * D]).astype(BF16)
        lam = _lambda(lam_ref, lam_init)
        g_scaled = gsub_ref[...] * (1.0 - lam_init)
        mats = []
        for n in range(3):
            if first:
                for p in range(per_mat):
                    stream.take(n * per_mat + p)
            mats.append(_dot(h, w_bf[:, n * D:(n + 1) * D]))
        q = (mats[0] * DH ** -0.5).astype(BF16)
        k, v = mats[1], mats[2]
        v_ref[...] = v
        kb, vb = k.astype(BF16), v.astype(BF16)
        for s in range(SEQ_PER_TILE):
            rows = slice(s * SEQ, (s + 1) * SEQ)
            kt_ref[s] = k[rows].T
            for hh in range(HEADS):
                cols = slice(hh * HD, (hh + 1) * HD)
                kh = kb[rows, cols]
                o_ref[rows, cols] = _diff_attn_head(q[rows, cols], [lambda qq, kh=kh: _dot_nt(qq, kh)],
                                                    [vb[rows, cols]], lam, g_scaled)

    _first_step_or_not(body)


def _attn_prompt(x, mod_l, g, w_qkv, j, lam_p, g_sub, lam_init):
    tile = pl.BlockSpec((TM, D), lambda i: (i, 0))
    kern = lambda *refs: _attn_prompt_kernel(*refs, j=j, lam_init=lam_init)
    return pl.pallas_call(
        kern,
        grid=(NT_P,),
        in_specs=[tile, _const_spec((MOD_ROWS, N_MOD * D)), _const_spec((1, D)), HBM_SPEC,
                  _const_spec((4, DH)), _const_spec((1, HD))],
        out_specs=[pl.BlockSpec((SEQ_PER_TILE, D, SEQ), lambda i: (i, 0, 0)), tile, tile],
        out_shape=[jax.ShapeDtypeStruct((BATCH, D, SEQ), F32), jax.ShapeDtypeStruct((N_P, D), F32),
                   jax.ShapeDtypeStruct((N_P, D), BF16)],
        scratch_shapes=[pltpu.VMEM((D, 3 * D), BF16)] + _stream_scratch((D, W_PIECE), W_DEPTH),
        compiler_params=_cparams(1),
        name="attn_prompt",
    )(x, mod_l, g, w_qkv, lam_p, g_sub)


def _qkv_sample_kernel(x_ref, mod_ref, g_ref, w_hbm, cos_ref, sm_ref, sp_ref, q_ref, k_ref, v_ref,
                       w_bf, stage, sem, *, j):
    stream = _WeightStream(_column_jobs(w_hbm, j, w_bf, W_PIECE), stage, sem)
    per_mat = D // W_PIECE

    def body(first):
        if first:
            stream.prime()
        i = pl.program_id(0)
        m = mod_ref[pl.ds(1 + i // TILES_PER_DEC_SEQ, 1), :]
        h = _adanorm(x_ref[...], g_ref[...], m[:, 0:D], m[:, D:2 * D]).astype(BF16)
        cos, sm, sp = cos_ref[...], sm_ref[...], sp_ref[...]
        for n, (o_ref, scale) in enumerate(((q_ref, DH ** -0.5), (k_ref, None), (v_ref, None))):
            if first:
                for p in range(per_mat):
                    stream.take(n * per_mat + p)
            y = _dot(h, w_bf[:, n * D:(n + 1) * D])
            if o_ref is v_ref:
                o_ref[...] = y.astype(BF16)
                continue
            if scale is not None:
                y = y * scale
            for hh in range(HEADS):
                t = _rope_tile(y[:, hh * HD:(hh + 1) * HD], cos, sm, sp)
                o_ref[:, hh * HD:(hh + 1) * HD] = t.astype(BF16)

    _first_step_or_not(body)


def _qkv_sample(x, tile0, mod_l, g, w_qkv, j, tabs):
    tile = pl.BlockSpec((TM, D), lambda i: (i, 0))
    tab = pl.BlockSpec((TM, HD), lambda i: (i % TILES_PER_DEC_SEQ, 0))
    kern = lambda *refs: _qkv_sample_kernel(*refs, j=j)
    return pl.pallas_call(
        kern,
        grid=(NT_S,),
        in_specs=[pl.BlockSpec((TM, D), lambda i: (i + tile0, 0)), _const_spec((MOD_ROWS, N_MOD * D)),
                  _const_spec((1, D)), HBM_SPEC, tab, tab, tab],
        out_specs=[tile, tile, tile],
        out_shape=[jax.ShapeDtypeStruct((N_S, D), BF16)] * 3,
        scratch_shapes=[pltpu.VMEM((D, 3 * D), BF16)] + _stream_scratch((D, W_PIECE), W_DEPTH),
        compiler_params=_cparams(1),
        name="qkv_sample",
    )(x, mod_l, g, w_qkv, *tabs)


def _attn_sample_kernel(*refs, lam_init):
    q_ref, kn_ref, vn_ref, kct_ref, vc_ref, lam_ref, gsub_ref, o_ref = refs
    lam = _lambda(lam_ref, lam_init)
    g_scaled = gsub_ref[...] * (1.0 - lam_init)
    head0 = pl.program_id(1) * ATT_HP
    for hh in range(ATT_HP):
        cols = slice(hh * HD, (hh + 1) * HD)
        kn = kn_ref[:, cols]
        kct = kct_ref[hh].astype(BF16)
        vc = vc_ref[:, pl.ds(head0 + hh, 1), :].reshape(PAST, HD).astype(BF16)
        o_ref[:, cols] = _diff_attn_head(
            q_ref[:, cols], [lambda qq, kn=kn: _dot_nt(qq, kn), lambda qq, kct=kct: _dot(qq, kct)],
            [vn_ref[:, cols], vc], lam, g_scaled)


def _attn_sample(q, kn, vn, cache_kt, cache_v, j, lam_p, g_sub, lam_init):
    nq = DEC_SEQ // ATT_TQ
    w = ATT_HP * HD
    in_specs = [
        pl.BlockSpec((ATT_TQ, w), lambda b, h, i: (b * nq + i, h)),
        pl.BlockSpec((DEC_SEQ, w), lambda b, h, i: (b, h)),
        pl.BlockSpec((DEC_SEQ, w), lambda b, h, i: (b, h)),
        pl.BlockSpec((None, None, ATT_HP, HD, PAST), lambda b, h, i: (b, j, h, 0, 0)),
    ]
    in_specs += [pl.BlockSpec((None, None, PAST, HEADS, HD), lambda b, h, i: (b, j, 0, 0, 0)),
                 pl.BlockSpec((4, DH), lambda b, h, i: (0, 0)), pl.BlockSpec((1, HD), lambda b, h, i: (0, 0))]
    kern = lambda *refs: _attn_sample_kernel(*refs, lam_init=lam_init)
    return pl.pallas_call(
        kern,
        grid=(DEC_BATCH, HEADS // ATT_HP, nq),
        in_specs=in_specs,
        out_specs=pl.BlockSpec((ATT_TQ, w), lambda b, h, i: (b * nq + i, h)),
        out_shape=jax.ShapeDtypeStruct((N_S, D), BF16),
        compiler_params=_cparams(3),
        name="attn_sample",
    )(q, kn, vn, cache_kt, cache_v, lam_p, g_sub)


def _rec_proj_kernel(x_ref, mod_ref, g_ref, w_hbm, q_ref, v_ref, gate_ref, zf_ref, zb_ref,
                     w_bf, stage, sem, *, j):
    stream = _WeightStream(_column_jobs(w_hbm, j, w_bf, W_PIECE), stage, sem)
    per_mat = D // W_PIECE

    def body(first):
        if first:
            stream.prime()
        m = mod_ref[pl.ds(_mod_row(pl.program_id(0)), 1), :]
        h = _adanorm(x_ref[...], g_ref[...], m[:, 0:D], m[:, D:2 * D]).astype(BF16)
        for n, o_ref in enumerate((q_ref, v_ref, gate_ref, zf_ref, zb_ref)):
            if first:
                for p in range(per_mat):
                    stream.take(n * per_mat + p)
            y = _dot(h, w_bf[:, n * D:(n + 1) * D])
            if o_ref is q_ref or o_ref is gate_ref:
                y = _silu(y)
            o_ref[...] = y.astype(o_ref.dtype)

    _first_step_or_not(body)


def _rec_proj(x, mod_l, g, w_in, j):
    tile = pl.BlockSpec((TM, D), lambda i: (i, 0))
    kern = lambda *refs: _rec_proj_kernel(*refs, j=j)
    return pl.pallas_call(
        kern,
        grid=(N_TOK // TM,),
        in_specs=[tile, _const_spec((MOD_ROWS, N_MOD * D)), _const_spec((1, D)), HBM_SPEC],
        out_specs=[tile] * 5,
        out_shape=[jax.ShapeDtypeStruct((N_TOK, D), BF16)] * 3 + [jax.ShapeDtypeStruct((N_TOK, D), F32)] * 2,
        scratch_shapes=[pltpu.VMEM((D, 5 * D), BF16)] + _stream_scratch((D, W_PIECE), W_DEPTH),
        compiler_params=_cparams(1),
        name="rec_proj",
    )(x, mod_l, g, w_in)


def _gla_tables():
    t = np.arange(GLA_GROUP)[:, None]
    s = np.arange(GLA_GROUP)[None, :]
    ids = np.full((GLA_GROUP, GLA_GROUP), -1, np.int32)
    for l, b in reversed(list(enumerate(GLA_LEVELS, 1))):
        ids = np.where((t // (2 * b) == s // (2 * b)) & (s < t), l, ids)
    ids = np.where((t // GLA_BLK == s // GLA_BLK) & (s <= t), 0, ids)
    ids = np.stack([ids, ids.T]).astype(np.int32)
    e = np.zeros((GLA_BLK * HD, GLA_GROUP), np.float32)
    for j in range(GLA_BLK):
        e[j * HD:(j + 1) * HD, j::GLA_BLK] = 1.0
    return jnp.asarray(ids), jnp.asarray(e)


def _bcast_row(x, blk, idx):
    n, w = x.shape
    r = x.reshape(n // blk, blk, w)[:, idx:idx + 1, :]
    return jnp.broadcast_to(r, (n // blk, blk, w)).reshape(n, w)


def _gla_group(q, v, z, lb, ids, e_mat, st, backward):
    n = GLA_GROUP
    sig = _sigmoid(z)
    f = lb + (1.0 - lb) * sig
    k = (1.0 - lb) * (1.0 - sig)
    lf = jnp.log(f) * LOG2E
    row = lax.broadcasted_iota(jnp.int32, (n, n), 0)
    col = lax.broadcasted_iota(jnp.int32, (n, n), 1)
    tri = jnp.where((col >= row) if backward else (col <= row), 1.0, 0.0).astype(BF16)
    c3 = _dot(tri, jnp.concatenate(_split3(lf), axis=1))
    cum = c3[:, 0:HD] + c3[:, HD:2 * HD] + c3[:, 2 * HD:3 * HD]

    prods = []
    for j in range(GLA_BLK):
        kj = _bcast_row(k, GLA_BLK, j)
        cj = _bcast_row(cum, GLA_BLK, j)
        prods.append((q * kj * jnp.exp2(jnp.minimum(cum - cj, 0.0))).astype(BF16))
    a = jnp.where(ids == 0, _dot(jnp.concatenate(prods, axis=1), e_mat), 0.0)

    for l, b in enumerate(GLA_LEVELS, 1):
        edge = _bcast_row(cum, 2 * b, b if backward else b - 1)
        ql = (q * jnp.exp2(cum - edge)).astype(BF16)
        kl = (k * jnp.exp2(edge - cum)).astype(BF16)
        a = jnp.where(ids == l, _dot_nt(ql, kl), a)
    o = _dot(a.astype(BF16), v)

    total = cum[0:1, :] if backward else cum[n - 1:n, :]
    if st is not None:
        o = o + _dot_nt((q * jnp.exp2(cum)).astype(BF16), st.astype(BF16))
    kdec = (k * jnp.exp2(total - cum)).astype(BF16)
    upd = _dot_tn(v, kdec)
    st_new = upd if st is None else st * jnp.exp2(total) + upd
    return o, st_new


def _lower_bound(x, layer):
    rows = [x[l:l + 1, :] for l in range(DEPTH)]
    mx = rows[0]
    for r in rows[1:]:
        mx = jnp.maximum(mx, r)
    ex = [jnp.exp(r - mx) for r in rows]
    tot = ex[0]
    for e in ex[1:]:
        tot = tot + e
    acc = ex[0] * 0.0
    for l in range(1, layer + 1):
        acc = acc + ex[l]
    return acc / tot


def _gla_kernel(*refs, n_groups, layer, has_state):
    if has_state:
        (q_ref, v_ref, gate_ref, zf_ref, zb_ref, lbl_ref, g_ref, ids_ref, e_ref, s0_ref,
         o_ref, acc_ref) = refs
    else:
        (q_ref, v_ref, gate_ref, zf_ref, zb_ref, lbl_ref, g_ref, ids_ref, e_ref,
         o_ref, sout_ref, acc_ref) = refs
    e_mat = e_ref[...]
    z_refs = (zf_ref, zb_ref)
    lbs = [_lower_bound(lbl_ref[d], layer) for d in range(2)]

    def step(gi, sts):
        new = []
        for d in range(2):
            g = gi if d == 0 else n_groups - 1 - gi
            start = g * GLA_GROUP
            rows = pl.ds(start if isinstance(start, int) else pl.multiple_of(start, GLA_GROUP), GLA_GROUP)
            o, st = _gla_group(q_ref[rows, :].astype(F32), v_ref[rows, :], z_refs[d][rows, :], lbs[d],
                               ids_ref[d], e_mat, sts[d], backward=(d == 1))
            acc_ref[rows, :] = acc_ref[rows, :] + o
            new.append(st)
        return tuple(new)

    acc_ref[...] = jnp.zeros_like(acc_ref)
    if has_state:
        init = tuple(s0_ref[d].T for d in range(2))
        lax.fori_loop(0, n_groups, step, init)
    else:
        sts = step(0, (None, None))
        for d in range(2):
            sout_ref[d] = sts[d].T
    o = acc_ref[...]
    y = o * lax.rsqrt(jnp.mean(o * o, axis=-1, keepdims=True) + EPS) * g_ref[...]
    o_ref[...] = (y * gate_ref[...].astype(F32)).astype(BF16)


def _gla(q, v, gate, zf, zb, lb_logits, g_out, tabs, *, row0, nb, ln, layer, s0=None):
    ids, e_mat = tabs
    assert ln % GLA_GROUP == 0 and (s0 is not None or ln == GLA_GROUP)
    b0 = row0 // ln
    seq = pl.BlockSpec((ln, HD), lambda b, h: (b + b0, h))
    in_specs = [seq] * 5 + [
        pl.BlockSpec((2, DEPTH, HD), lambda b, h: (0, 0, h)),
        pl.BlockSpec((1, HD), lambda b, h: (0, 0)),
        pl.BlockSpec((2, GLA_GROUP, GLA_GROUP), lambda b, h: (0, 0, 0)),
        pl.BlockSpec((GLA_BLK * HD, GLA_GROUP), lambda b, h: (0, 0)),
    ]
    args = [q, v, gate, zf, zb, lb_logits, g_out, ids, e_mat]
    out_specs = [pl.BlockSpec((ln, HD), lambda b, h: (b, h))]
    out_shape = [jax.ShapeDtypeStruct((nb * ln, D), BF16)]
    state_spec = pl.BlockSpec((None, 2, None, HD, HD), lambda b, h: (b, 0, h, 0, 0))
    if s0 is not None:
        in_specs.append(state_spec)
        args.append(s0)
    else:
        out_specs.append(state_spec)
        out_shape.append(jax.ShapeDtypeStruct((nb, 2, HEADS, HD, HD), F32))
    kern = lambda *refs: _gla_kernel(*refs, n_groups=ln // GLA_GROUP, layer=layer, has_state=s0 is not None)
    return pl.pallas_call(
        kern,
        grid=(nb, HEADS),
        in_specs=in_specs,
        out_specs=out_specs,
        out_shape=out_shape,
        scratch_shapes=[pltpu.VMEM((ln, HD), F32)],
        compiler_params=_cparams(2),
        name="gla_sample" if s0 is not None else "gla_prompt",
    )(*args)


def _dft_tables(n):
    c = np.arange(FOUR_DG)
    ang_c = 2.0 * np.pi * ((c[:, None] * c[None, :]) % FOUR_DG) / FOUR_DG
    cs = np.concatenate([np.cos(ang_c), np.sin(ang_c)], axis=1)
    t = np.arange(n)
    ang_n = 2.0 * np.pi * ((t[:, None] * t[None, :]) % n) / n
    cn = np.concatenate([np.cos(ang_n), -np.sin(ang_n)], axis=1) / math.sqrt(n * FOUR_DG)
    return jnp.asarray(cs, F32).astype(BF16), jnp.asarray(cn, F32).astype(BF16)


def _fourier_kernel(x_ref, mod_ref, g_ref, cs_ref, cn_ref, o_ref, *, row_of):
    m = mod_ref[pl.ds(row_of(pl.program_id(0)), 1), :]
    h = _adanorm(x_ref[...], g_ref[...], m[:, 0:D], m[:, D:2 * D]).astype(BF16)
    cs = cs_ref[...]
    cn = cn_ref[...]
    for g in range(FOUR_GROUPS):
        xcs = _dot(h[:, g * FOUR_DG:(g + 1) * FOUR_DG], cs)
        stacked = jnp.concatenate([xcs[:, 0:FOUR_DG], xcs[:, FOUR_DG:]], axis=0).astype(BF16)
        o_ref[:, g * FOUR_DG:(g + 1) * FOUR_DG] = _dot(cn, stacked).astype(BF16)


def _fourier(x, mod_l, g, *, row0, nb, ln, row_of):
    cs, cn = _dft_tables(ln)
    b0 = row0 // ln
    kern = lambda *refs: _fourier_kernel(*refs, row_of=row_of)
    return pl.pallas_call(
        kern,
        grid=(nb,),
        in_specs=[pl.BlockSpec((ln, D), lambda b: (b + b0, 0)), _const_spec((MOD_ROWS, N_MOD * D)),
                  _const_spec((1, D)), _const_spec((FOUR_DG, 2 * FOUR_DG)), _const_spec((ln, 2 * ln))],
        out_specs=pl.BlockSpec((ln, D), lambda b: (b, 0)),
        out_shape=jax.ShapeDtypeStruct((nb * ln, D), BF16),
        compiler_params=_cparams(1),
        name="fourier_%d" % ln,
    )(x, mod_l, g, cs, cn)


FF_DEPTH_COLS = 4
FF_DEPTH_ROWS = 2


def _post_ffn_kernel(*refs, layer, wo_index, split_x, final):
    refs = list(refs)
    x_refs = [refs.pop(0) for _ in range(2 if split_x else 1)]
    op_ref, os_ref, mod_ref, wo_hbm, g_ref, win_hbm, wout_hbm = refs[:7]
    refs = refs[7:]
    gf_ref = refs.pop(0) if final else None
    out_refs = [refs.pop(0) for _ in range(2 if final else 1)]
    wo_bf, win_bf, wout_bf, stage_c, sem_c, stage_r, sem_r = refs
    n_chunks = D_FF // FF_CHUNK
    wo_jobs = _column_jobs(wo_hbm, wo_index, wo_bf, FF_CHUNK)
    in_jobs = []
    for c in range(n_chunks):
        for c0 in (c * FF_CHUNK, D_FF + c * FF_CHUNK):
            in_jobs.append((win_hbm.at[layer, :, pl.ds(c0, FF_CHUNK)], win_bf.at[:, pl.ds(c0, FF_CHUNK)]))
    cols = _WeightStream(wo_jobs + in_jobs, stage_c, sem_c)
    rows = _WeightStream([(wout_hbm.at[layer, pl.ds(c * FF_CHUNK, FF_CHUNK), :],
                           wout_bf.at[pl.ds(c * FF_CHUNK, FF_CHUNK), :]) for c in range(n_chunks)],
                         stage_r, sem_r)

    def body(first):
        if first:
            cols.prime()
            rows.prime()
        i = pl.program_id(0)
        ctx = i < NT_P
        m = mod_ref[pl.ds(_mod_row(i), 1), :]
        x = jnp.where(ctx, x_refs[0][...], x_refs[1][...]) if split_x else x_refs[0][...]
        o = jnp.where(ctx, op_ref[...], os_ref[...])
        if first:
            for k in range(len(wo_jobs)):
                cols.take(k)
        x = x + m[:, 2 * D:3 * D] * _dot(o, wo_bf[...])
        h = _adanorm(x, g_ref[...], m[:, 3 * D:4 * D], m[:, 4 * D:5 * D]).astype(BF16)
        acc = None
        for c in range(n_chunks):
            if first:
                cols.take(len(wo_jobs) + 2 * c)
                cols.take(len(wo_jobs) + 2 * c + 1)
                rows.take(c)
            gt = _dot(h, win_bf[:, c * FF_CHUNK:(c + 1) * FF_CHUNK])
            up = _dot(h, win_bf[:, D_FF + c * FF_CHUNK:D_FF + (c + 1) * FF_CHUNK])
            part = _dot((_silu(gt) * up).astype(BF16), wout_bf[c * FF_CHUNK:(c + 1) * FF_CHUNK, :])
            acc = part if acc is None else acc + part
        x = x + m[:, 5 * D:6 * D] * acc
        if not final:
            out_refs[0][...] = x
            return
        y = x * lax.rsqrt(jnp.mean(x * x, axis=-1, keepdims=True) + EPS) * gf_ref[...]
        if first:
            out_refs[0][...] = y
            return

        @pl.when(ctx)
        def _():
            out_refs[0][...] = y

        @pl.when(jnp.logical_not(ctx))
        def _():
            out_refs[1][...] = y

    _first_step_or_not(body)


def _post_ffn(xs, op, os_, mod_l, w_o, wo_index, g, w_in, w_out, layer, g_final=None):
    tile = pl.BlockSpec((TM, D), lambda i: (i, 0))
    p_tile = pl.BlockSpec((TM, D), lambda i: (jnp.minimum(i, NT_P - 1), 0))
    s_tile = pl.BlockSpec((TM, D), lambda i: (jnp.maximum(i - NT_P, 0), 0))
    split_x = len(xs) == 2
    final = g_final is not None
    in_specs = ([p_tile, s_tile] if split_x else [tile]) + [
        p_tile, s_tile, _const_spec((MOD_ROWS, N_MOD * D)), HBM_SPEC, _const_spec((1, D)), HBM_SPEC, HBM_SPEC]
    args = list(xs) + [op, os_, mod_l, w_o, g, w_in, w_out]
    if final:
        in_specs.append(_const_spec((1, D)))
        args.append(g_final)
        out_specs = [p_tile, s_tile]
        out_shape = [jax.ShapeDtypeStruct((N_P, D), F32), jax.ShapeDtypeStruct((N_S, D), F32)]
    else:
        out_specs = tile
        out_shape = jax.ShapeDtypeStruct((N_TOK, D), F32)
    kern = lambda *refs: _post_ffn_kernel(*refs, layer=layer, wo_index=wo_index, split_x=split_x, final=final)
    return pl.pallas_call(
        kern,
        grid=(N_TOK // TM,),
        in_specs=in_specs,
        out_specs=out_specs,
        out_shape=out_shape,
        scratch_shapes=([pltpu.VMEM((D, D), BF16), pltpu.VMEM((D, 2 * D_FF), BF16), pltpu.VMEM((D_FF, D), BF16)]
                        + _stream_scratch((D, FF_CHUNK), FF_DEPTH_COLS)
                        + _stream_scratch((FF_CHUNK, D), FF_DEPTH_ROWS)),
        compiler_params=_cparams(1),
        name="post_ffn",
    )(*args)


def kernel(x_prompt, x_sample, cache_attn_k, cache_attn_v, state_hgrn, c, c_ctx, w_ada, b_ada, g_norm_mix,
           g_norm_ffn, w_qkv_attn, lam_attn, g_subln_attn, w_o_attn, w_in_rec, lb_logits_rec, g_out_rec,
           w_o_rec, w_four, w_ffn_in, w_ffn_out, g_final):
    xs = (x_prompt.reshape(N_P, D), x_sample.reshape(N_S, D))
    cond = jnp.concatenate([c_ctx.reshape(1, D), c, jnp.zeros((MOD_ROWS - 1 - DEC_BATCH, D), F32)], axis=0)
    mod = _modulation(cond, w_ada, b_ada)
    cache_kt = jnp.transpose(cache_attn_k, (0, 1, 3, 4, 5, 2)).reshape(DEC_BATCH, -1, HEADS, HD, PAST)
    rope_tabs = _rope_tables()
    gla_tabs = _gla_tables()
    gla_tabs = (gla_tabs[0], gla_tabs[1].astype(BF16))
    new_kt, new_v, new_s = [], [], []
    for i in range(DEPTH):
        kind, j = i % N_MIXERS, i // N_MIXERS
        g_mix = g_norm_mix[i].reshape(1, D)
        x_p = xs[0]
        x_s, s_tile0 = (xs[1], 0) if len(xs) == 2 else (xs[0], NT_P)
        if kind == 0:
            lam_init = 0.8 - 0.6 * math.exp(-0.3 * i)
            g_sub = g_subln_attn[j].reshape(1, HD)
            kt, vp, op = _attn_prompt(x_p, mod[i], g_mix, w_qkv_attn, j, lam_attn[j], g_sub, lam_init)
            new_kt.append(kt)
            new_v.append(vp.reshape(BATCH, SEQ, D))
            qs, ks, vs = _qkv_sample(x_s, s_tile0, mod[i], g_mix, w_qkv_attn, j, rope_tabs)
            os_ = _attn_sample(qs, ks, vs, cache_kt, cache_attn_v, j, lam_attn[j], g_sub, lam_init)
            w_o = w_o_attn
        elif kind == 1:
            proj = _rec_proj(xs[0], mod[i], g_mix, w_in_rec, j)
            g_out = g_out_rec[j].reshape(1, HD)
            op, s_ctx = _gla(*proj, lb_logits_rec, g_out, gla_tabs, row0=0, nb=BATCH, ln=SEQ, layer=i)
            new_s.append(s_ctx)
            os_ = _gla(*proj, lb_logits_rec, g_out, gla_tabs, row0=N_P, nb=DEC_BATCH, ln=DEC_SEQ, layer=i,
                       s0=state_hgrn[:, j])[0]
            w_o = w_o_rec
        else:
            op = _fourier(xs[0], mod[i], g_mix, row0=0, nb=BATCH, ln=SEQ, row_of=lambda b: 0)
            os_ = _fourier(xs[0], mod[i], g_mix, row0=N_P, nb=DEC_BATCH, ln=DEC_SEQ, row_of=lambda b: 1 + b)
            w_o = w_four
        out = _post_ffn(xs, op, os_, mod[i], w_o, j, g_norm_ffn[i].reshape(1, D), w_ffn_in, w_ffn_out, i,
                        g_final.reshape(1, D) if i == DEPTH - 1 else None)
        xs = tuple(out) if i == DEPTH - 1 else (out,)
    y_prompt = xs[0].reshape(BATCH, SEQ, D)
    y_sample = xs[1].reshape(DEC_BATCH, DEC_SEQ, D)
    new_k = jnp.transpose(jnp.stack(new_kt, axis=1).reshape(BATCH, -1, HEADS, 2, DH, SEQ), (0, 1, 5, 2, 3, 4))
    new_v = jnp.stack(new_v, axis=1).reshape(BATCH, -1, SEQ, HEADS, HD)
    return (y_prompt, y_sample, new_k, new_v, jnp.stack(new_s, axis=1))
```

```python
import math

import numpy as np
import jax
import jax.numpy as jnp
from jax import lax
from jax.experimental import pallas as pl
from jax.experimental.pallas import tpu as pltpu

F32 = jnp.float32
BF16 = jnp.bfloat16

D = 1024
BATCH = 16
SEQ = 256
DEPTH = 4
DEC_BATCH = 4
DEC_SEQ = 1024
PAST = 512
GRID_W = 64
N_MIXERS = 3
HEADS = 8
DH = 64
HD = 128
ROPE_THETA = 10000.0
D_FF = 2816
N_MOD = 6
EPS = 1e-6
LOG2E = 1.0 / math.log(2.0)
FOUR_GROUPS = 4
FOUR_DG = 256

N_P = BATCH * SEQ
N_S = DEC_BATCH * DEC_SEQ
N_TOK = N_P + N_S
TM = 512
NT_P = N_P // TM
NT_S = N_S // TM
TILES_PER_DEC_SEQ = DEC_SEQ // TM
SEQ_PER_TILE = TM // SEQ
MOD_ROWS = 8
FF_CHUNK = 256
ATT_TQ = 256
ATT_HP = 2
GLA_GROUP = 256
GLA_BLK = 8
GLA_LEVELS = (8, 16, 32, 64, 128)
VMEM_LIMIT = 60 * 1024 * 1024


def _cparams(n_axes):
    return pltpu.CompilerParams(dimension_semantics=("arbitrary",) * n_axes,
                                vmem_limit_bytes=VMEM_LIMIT)


def _const_spec(shape):
    nd = len(shape)
    return pl.BlockSpec(shape, lambda *_: (0,) * nd, pipeline_mode=pl.Buffered(1))


def _sigmoid(x):
    return 1.0 / (1.0 + jnp.exp(-x))


def _silu(x):
    return x * _sigmoid(x)


def _adanorm(x, g, shift, scale):
    ms = jnp.mean(x * x, axis=-1, keepdims=True)
    return x * lax.rsqrt(ms + EPS) * (g * (1.0 + scale)) + shift


def _mod_row(tile):
    return jnp.where(tile < NT_P, 0, 1 + (tile - NT_P) // TILES_PER_DEC_SEQ)


def _dot(a, b):
    return jnp.dot(a, b, preferred_element_type=F32)


def _dot_nt(a, b):
    return lax.dot_general(a, b, (((1,), (1,)), ((), ())), preferred_element_type=F32)


def _dot_tn(a, b):
    return lax.dot_general(a, b, (((0,), (0,)), ((), ())), preferred_element_type=F32)


def _split3(x):
    hi = x.astype(BF16)
    r1 = x - hi.astype(F32)
    mid = r1.astype(BF16)
    lo = (r1 - mid.astype(F32)).astype(BF16)
    return hi, mid, lo


class _WeightStream:
    def __init__(self, jobs, stage_ref, sem_ref):
        self.jobs, self.stage, self.sem = jobs, stage_ref, sem_ref
        self.depth = stage_ref.shape[0]

    def _copy(self, k):
        slot = k % self.depth
        return pltpu.make_async_copy(self.jobs[k][0], self.stage.at[slot], self.sem.at[slot])

    def prime(self):
        for k in range(min(self.depth, len(self.jobs))):
            self._copy(k).start()

    def take(self, k):
        self._copy(k).wait()
        self.jobs[k][1][...] = self.stage[k % self.depth].astype(BF16)
        if k + self.depth < len(self.jobs):
            self._copy(k + self.depth).start()


def _stream_scratch(piece_shape, depth):
    return [pltpu.VMEM((depth,) + tuple(piece_shape), F32), pltpu.SemaphoreType.DMA((depth,))]


def _column_jobs(w_hbm, index, w_bf, width):
    return [(w_hbm.at[index, :, pl.ds(c, width)], w_bf.at[:, pl.ds(c, width)])
            for c in range(0, w_bf.shape[1], width)]


def _first_step_or_not(body):
    first = pl.program_id(0) == 0
    pl.when(first)(lambda: body(True))
    pl.when(jnp.logical_not(first))(lambda: body(False))


HBM_SPEC = pl.BlockSpec(memory_space=pl.ANY)
W_PIECE = 512
W_DEPTH = 3


MOD_TN = 1536


def _mod_kernel(cond_ref, w_ref, b_ref, o_ref):
    s = _silu(cond_ref[...])
    s_hi = s.astype(BF16)
    s_lo = (s - s_hi.astype(F32)).astype(BF16)
    w = w_ref[0]
    w_hi = w.astype(BF16)
    w_lo = (w - w_hi.astype(F32)).astype(BF16)
    o_ref[0] = _dot(s_hi, w_hi) + _dot(s_lo, w_hi) + _dot(s_hi, w_lo) + b_ref[0]


def _modulation(cond, w_ada, b_ada):
    return pl.pallas_call(
        _mod_kernel,
        grid=(DEPTH, N_MOD * D // MOD_TN),
        in_specs=[
            pl.BlockSpec((MOD_ROWS, D), lambda l, n: (0, 0)),
            pl.BlockSpec((1, D, MOD_TN), lambda l, n: (l, 0, n)),
            pl.BlockSpec((1, 1, MOD_TN), lambda l, n: (l, 0, n)),
        ],
        out_specs=pl.BlockSpec((1, MOD_ROWS, MOD_TN), lambda l, n: (l, 0, n)),
        out_shape=jax.ShapeDtypeStruct((DEPTH, MOD_ROWS, N_MOD * D), F32),
        compiler_params=_cparams(2),
        name="modulation",
    )(cond, w_ada, b_ada.reshape(DEPTH, 1, N_MOD * D))


def _rope_tables():
    t = np.arange(DEC_SEQ)
    row, col = t // GRID_W, t % GRID_W
    lane = np.arange(HD) % DH
    use_row = lane < DH // 2
    idx = (lane % (DH // 2)) % (DH // 4)
    inv = ROPE_THETA ** (-idx.astype(np.float64) / (DH // 4))
    pos = np.where(use_row[None, :], row[:, None], col[:, None]).astype(np.float64)
    ang = pos * inv[None, :]
    first = (lane % (DH // 2)) < DH // 4
    cos = np.cos(ang)
    sin = np.sin(ang)
    sin_minus = np.where(first[None, :], -sin, 0.0)
    sin_plus = np.where(first[None, :], 0.0, sin)
    return (jnp.asarray(cos, F32), jnp.asarray(sin_minus, F32), jnp.asarray(sin_plus, F32))


def _rope_tile(t, cos, sin_minus, sin_plus):
    return (t * cos + pltpu.roll(t, HD - DH // 4, 1) * sin_minus
            + pltpu.roll(t, DH // 4, 1) * sin_plus)


def _lambda(lam_ref, lam_init):
    lp = lam_ref[...]
    return (jnp.exp(jnp.sum(lp[0:1] * lp[1:2], axis=-1, keepdims=True))
            - jnp.exp(jnp.sum(lp[2:3] * lp[3:4], axis=-1, keepdims=True)) + lam_init)


def _with_ones(v):
    return jnp.concatenate([v, jnp.ones_like(v)], axis=1)


def _diff_attn_head(q, score_fns, values, lam, g_scaled):
    tq = q.shape[0]
    lane = lax.broadcasted_iota(jnp.int32, q.shape, 1)
    zero = jnp.zeros_like(q)
    qq = jnp.concatenate([jnp.where(lane < DH, q, zero), jnp.where(lane >= DH, q, zero)], axis=0)
    scores = [fn(qq) for fn in score_fns]
    mx = scores[0].max(axis=-1, keepdims=True)
    for s in scores[1:]:
        mx = jnp.maximum(mx, s.max(axis=-1, keepdims=True))
    acc = None
    for s, v in zip(scores, values):
        c = _dot(jnp.exp2(s - mx).astype(BF16), v)
        acc = c if acc is None else acc + c
    r = acc[:, 0:HD] / acc[:, HD:2 * HD]
    o = r[:tq] - lam * r[tq:]
    o = o * lax.rsqrt(jnp.mean(o * o, axis=-1, keepdims=True) + EPS) * g_scaled
    return o.astype(BF16)


def _attn_prompt_kernel(*refs, j, n_layers, lam_init):
    x_ref, mod_ref, g_ref, w_hbm, lam_ref, gsub_ref = refs[:6]
    kt_ref, v_ref, o_ref, w_bf, stage, sem = refs[-6:]
    owns_all = j == 0
    stream = _WeightStream(_column_jobs(w_hbm, j, w_bf, W_PIECE), stage, sem)
    per_mat = D // W_PIECE

    def body(first):
        if first:
            stream.prime()
        m = mod_ref[pl.ds(0, 1), :]
        h = _adanorm(x_ref[...], g_ref[...], m[:, 0:D], m[:, D:2 * D]).astype(BF16)
        lam = _lambda(lam_ref, lam_init)
        g_scaled = gsub_ref[...] * (1.0 - lam_init)
        mats = []
        for n in range(3):
            if first:
                for p in range(per_mat):
                    stream.take(n * per_mat + p)
            mats.append(_dot(h, w_bf[:, n * D:(n + 1) * D]))
        q = (mats[0] * (DH ** -0.5 * LOG2E)).astype(BF16)
        k, v = mats[1], mats[2]
        kb, vb = k.astype(BF16), v.astype(BF16)
        for s in range(SEQ_PER_TILE):
            rows = slice(s * SEQ, (s + 1) * SEQ)
            if owns_all:
                for jj in range(n_layers):
                    kt_ref[s, jj] = k[rows].T if jj == j else jnp.zeros((D, SEQ), F32)
                    v_ref[s, jj] = v[rows] if jj == j else jnp.zeros((SEQ, D), F32)
            else:
                kt_ref[s] = k[rows].T
                v_ref[s] = v[rows]
            for hh in range(HEADS):
                cols = slice(hh * HD, (hh + 1) * HD)
                kh = kb[rows, cols]
                o_ref[rows, cols] = _diff_attn_head(q[rows, cols], [lambda qq, kh=kh: _dot_nt(qq, kh)],
                                                    [_with_ones(vb[rows, cols])], lam, g_scaled)

    _first_step_or_not(body)


def _attn_prompt(x, mod_l, g, w_qkv, j, lam_p, g_sub, lam_init, prev=None):
    n_layers = w_qkv.shape[0]
    assert (prev is None) == (j == 0)
    tile = pl.BlockSpec((TM, D), lambda i: (i, 0))
    in_specs = [tile, _const_spec((MOD_ROWS, N_MOD * D)), _const_spec((1, D)), HBM_SPEC,
                _const_spec((4, DH)), _const_spec((1, HD))]
    args = [x, mod_l, g, w_qkv, lam_p, g_sub]
    if prev is None:
        kt_spec = pl.BlockSpec((SEQ_PER_TILE, n_layers, D, SEQ), lambda i: (i, 0, 0, 0))
        v_spec = pl.BlockSpec((SEQ_PER_TILE, n_layers, SEQ, D), lambda i: (i, 0, 0, 0))
        aliases = {}
    else:
        kt_spec = pl.BlockSpec((SEQ_PER_TILE, None, D, SEQ), lambda i: (i, j, 0, 0))
        v_spec = pl.BlockSpec((SEQ_PER_TILE, None, SEQ, D), lambda i: (i, j, 0, 0))
        aliases = {len(args): 0, len(args) + 1: 1}
        in_specs += [HBM_SPEC, HBM_SPEC]
        args += list(prev)
    kern = lambda *refs: _attn_prompt_kernel(*refs, j=j, n_layers=n_layers, lam_init=lam_init)
    return pl.pallas_call(
        kern,
        grid=(NT_P,),
        in_specs=in_specs,
        out_specs=[kt_spec, v_spec, tile],
        out_shape=[jax.ShapeDtypeStruct((BATCH, n_layers, D, SEQ), F32),
                   jax.ShapeDtypeStruct((BATCH, n_layers, SEQ, D), F32), jax.ShapeDtypeStruct((N_P, D), BF16)],
        input_output_aliases=aliases,
        scratch_shapes=[pltpu.VMEM((D, 3 * D), BF16)] + _stream_scratch((D, W_PIECE), W_DEPTH),
        compiler_params=_cparams(1),
        name="attn_prompt",
    )(*args)


def _qkv_sample_kernel(x_ref, mod_ref, g_ref, w_hbm, cos_ref, sm_ref, sp_ref, q_ref, k_ref, v_ref,
                       w_bf, stage, sem, *, j):
    stream = _WeightStream(_column_jobs(w_hbm, j, w_bf, W_PIECE), stage, sem)
    per_mat = D // W_PIECE

    def body(first):
        if first:
            stream.prime()
        i = pl.program_id(0)
        m = mod_ref[pl.ds(1 + i // TILES_PER_DEC_SEQ, 1), :]
        h = _adanorm(x_ref[...], g_ref[...], m[:, 0:D], m[:, D:2 * D]).astype(BF16)
        cos, sm, sp = cos_ref[...], sm_ref[...], sp_ref[...]
        for n, (o_ref, scale) in enumerate(((q_ref, DH ** -0.5 * LOG2E), (k_ref, None), (v_ref, None))):
            if first:
                for p in range(per_mat):
                    stream.take(n * per_mat + p)
            y = _dot(h, w_bf[:, n * D:(n + 1) * D])
            if o_ref is v_ref:
                o_ref[...] = y.astype(BF16)
                continue
            if scale is not None:
                y = y * scale
            for hh in range(HEADS):
                t = _rope_tile(y[:, hh * HD:(hh + 1) * HD], cos, sm, sp)
                o_ref[:, hh * HD:(hh + 1) * HD] = t.astype(BF16)

    _first_step_or_not(body)


def _qkv_sample(x, tile0, mod_l, g, w_qkv, j, tabs):
    tile = pl.BlockSpec((TM, D), lambda i: (i, 0))
    tab = pl.BlockSpec((TM, HD), lambda i: (i % TILES_PER_DEC_SEQ, 0))
    kern = lambda *refs: _qkv_sample_kernel(*refs, j=j)
    return pl.pallas_call(
        kern,
        grid=(NT_S,),
        in_specs=[pl.BlockSpec((TM, D), lambda i: (i + tile0, 0)), _const_spec((MOD_ROWS, N_MOD * D)),
                  _const_spec((1, D)), HBM_SPEC, tab, tab, tab],
        out_specs=[tile, tile, tile],
        out_shape=[jax.ShapeDtypeStruct((N_S, D), BF16)] * 3,
        scratch_shapes=[pltpu.VMEM((D, 3 * D), BF16)] + _stream_scratch((D, W_PIECE), W_DEPTH),
        compiler_params=_cparams(1),
        name="qkv_sample",
    )(x, mod_l, g, w_qkv, *tabs)


def _attn_sample_kernel(*refs, lam_init):
    q_ref, kn_ref, vn_ref, kct_ref, vc_ref, lam_ref, gsub_ref, o_ref = refs
    lam = _lambda(lam_ref, lam_init)
    g_scaled = gsub_ref[...] * (1.0 - lam_init)
    for hh in range(ATT_HP):
        cols = slice(hh * HD, (hh + 1) * HD)
        kn = kn_ref[:, cols]
        kct = kct_ref[hh].astype(BF16)
        vc = vc_ref[:, cols].astype(BF16)
        o_ref[:, cols] = _diff_attn_head(
            q_ref[:, cols], [lambda qq, kn=kn: _dot_nt(qq, kn), lambda qq, kct=kct: _dot(qq, kct)],
            [_with_ones(vn_ref[:, cols]), _with_ones(vc)], lam, g_scaled)


def _attn_sample(q, kn, vn, cache_kt, cache_v, j, lam_p, g_sub, lam_init):
    nq = DEC_SEQ // ATT_TQ
    w = ATT_HP * HD
    in_specs = [
        pl.BlockSpec((ATT_TQ, w), lambda b, h, i: (b * nq + i, h)),
        pl.BlockSpec((DEC_SEQ, w), lambda b, h, i: (b, h)),
        pl.BlockSpec((DEC_SEQ, w), lambda b, h, i: (b, h)),
        pl.BlockSpec((None, None, ATT_HP, HD, PAST), lambda b, h, i: (b, j, h, 0, 0)),
    ]
    in_specs += [pl.BlockSpec((None, None, PAST, w), lambda b, h, i: (b, j, 0, h)),
                 pl.BlockSpec((4, DH), lambda b, h, i: (0, 0)), pl.BlockSpec((1, HD), lambda b, h, i: (0, 0))]
    kern = lambda *refs: _attn_sample_kernel(*refs, lam_init=lam_init)
    return pl.pallas_call(
        kern,
        grid=(DEC_BATCH, HEADS // ATT_HP, nq),
        in_specs=in_specs,
        out_specs=pl.BlockSpec((ATT_TQ, w), lambda b, h, i: (b * nq + i, h)),
        out_shape=jax.ShapeDtypeStruct((N_S, D), BF16),
        compiler_params=_cparams(3),
        name="attn_sample",
    )(q, kn, vn, cache_kt, cache_v, lam_p, g_sub)


def _rec_proj_kernel(x_ref, mod_ref, g_ref, w_hbm, q_ref, v_ref, gate_ref, zf_ref, zb_ref,
                     w_bf, stage, sem, *, j):
    stream = _WeightStream(_column_jobs(w_hbm, j, w_bf, W_PIECE), stage, sem)
    per_mat = D // W_PIECE

    def body(first):
        if first:
            stream.prime()
        m = mod_ref[pl.ds(_mod_row(pl.program_id(0)), 1), :]
        h = _adanorm(x_ref[...], g_ref[...], m[:, 0:D], m[:, D:2 * D]).astype(BF16)
        for n, o_ref in enumerate((q_ref, v_ref, gate_ref, zf_ref, zb_ref)):
            if first:
                for p in range(per_mat):
                    stream.take(n * per_mat + p)
            y = _dot(h, w_bf[:, n * D:(n + 1) * D])
            if o_ref is q_ref or o_ref is gate_ref:
                y = _silu(y)
            o_ref[...] = y.astype(o_ref.dtype)

    _first_step_or_not(body)


def _rec_proj(x, mod_l, g, w_in, j):
    tile = pl.BlockSpec((TM, D), lambda i: (i, 0))
    kern = lambda *refs: _rec_proj_kernel(*refs, j=j)
    return pl.pallas_call(
        kern,
        grid=(N_TOK // TM,),
        in_specs=[tile, _const_spec((MOD_ROWS, N_MOD * D)), _const_spec((1, D)), HBM_SPEC],
        out_specs=[tile] * 5,
        out_shape=[jax.ShapeDtypeStruct((N_TOK, D), BF16)] * 3 + [jax.ShapeDtypeStruct((N_TOK, D), F32)] * 2,
        scratch_shapes=[pltpu.VMEM((D, 5 * D), BF16)] + _stream_scratch((D, W_PIECE), W_DEPTH),
        compiler_params=_cparams(1),
        name="rec_proj",
    )(x, mod_l, g, w_in)


def _gla_tables():
    t = np.arange(GLA_GROUP)[:, None]
    s = np.arange(GLA_GROUP)[None, :]
    ids = np.full((GLA_GROUP, GLA_GROUP), -1, np.int32)
    for l, b in reversed(list(enumerate(GLA_LEVELS, 1))):
        ids = np.where((t // (2 * b) == s // (2 * b)) & (s < t), l, ids)
    ids = np.where((t // GLA_BLK == s // GLA_BLK) & (s <= t), 0, ids)
    ids = np.stack([ids, ids.T]).astype(np.int32)
    e = np.zeros((GLA_BLK * HD, GLA_GROUP), np.float32)
    for j in range(GLA_BLK):
        e[j * HD:(j + 1) * HD, j::GLA_BLK] = 1.0
    return jnp.asarray(ids), jnp.asarray(e)


def _bcast_row(x, blk, idx):
    n, w = x.shape
    r = x.reshape(n // blk, blk, w)[:, idx:idx + 1, :]
    return jnp.broadcast_to(r, (n // blk, blk, w)).reshape(n, w)


def _gla_group(q, v, z, lb, ids, e_mat, st, backward):
    n = GLA_GROUP
    sig = _sigmoid(z)
    f = lb + (1.0 - lb) * sig
    k = (1.0 - lb) * (1.0 - sig)
    lf = jnp.log(f) * LOG2E
    row = lax.broadcasted_iota(jnp.int32, (n, n), 0)
    col = lax.broadcasted_iota(jnp.int32, (n, n), 1)
    tri = jnp.where((col >= row) if backward else (col <= row), 1.0, 0.0).astype(BF16)
    c3 = _dot(tri, jnp.concatenate(_split3(lf), axis=1))
    cum = c3[:, 0:HD] + c3[:, HD:2 * HD] + c3[:, 2 * HD:3 * HD]

    prods = []
    for j in range(GLA_BLK):
        kj = _bcast_row(k, GLA_BLK, j)
        cj = _bcast_row(cum, GLA_BLK, j)
        prods.append((q * kj * jnp.exp2(jnp.minimum(cum - cj, 0.0))).astype(BF16))
    half = n // 2
    ids_q = ids[0:half, 0:half]
    diag = lambda x, h: x[h * half:(h + 1) * half, h * half:(h + 1) * half]
    pair = _dot(jnp.concatenate(prods, axis=1), e_mat)
    quads = [jnp.where(ids_q == 0, diag(pair, h), 0.0) for h in range(2)]

    for l, b in enumerate(GLA_LEVELS[:-1], 1):
        edge = _bcast_row(cum, 2 * b, b if backward else b - 1)
        ql = (q * jnp.exp2(cum - edge)).astype(BF16)
        kl = (k * jnp.exp2(edge - cum)).astype(BF16)
        x = _dot_nt(ql, kl)
        quads = [jnp.where(ids_q == l, diag(x, h), quads[h]) for h in range(2)]

    assert GLA_LEVELS[-1] == half
    lo, hi = slice(0, half), slice(half, n)
    qs, ks = (lo, hi) if backward else (hi, lo)
    edge = cum[half:half + 1, :] if backward else cum[half - 1:half, :]
    cross = _dot_nt((q[qs] * jnp.exp2(cum[qs] - edge)).astype(BF16),
                    (k[ks] * jnp.exp2(edge - cum[ks])).astype(BF16)).astype(BF16)
    o_halves = [_dot(quads[h].astype(BF16), v[h * half:(h + 1) * half]) for h in range(2)]
    o_halves[0 if backward else 1] += _dot(cross, v[ks])
    o = jnp.concatenate(o_halves, axis=0)

    total = cum[0:1, :] if backward else cum[n - 1:n, :]
    if st is not None:
        o = o + _dot_nt((q * jnp.exp2(cum)).astype(BF16), st.astype(BF16))
    kdec = (k * jnp.exp2(total - cum)).astype(BF16)
    upd = _dot_tn(v, kdec)
    st_new = upd if st is None else st * jnp.exp2(total) + upd
    return o, st_new


def _lower_bound(x, layer):
    rows = [x[l:l + 1, :] for l in range(DEPTH)]
    mx = rows[0]
    for r in rows[1:]:
        mx = jnp.maximum(mx, r)
    ex = [jnp.exp(r - mx) for r in rows]
    tot = ex[0]
    for e in ex[1:]:
        tot = tot + e
    acc = ex[0] * 0.0
    for l in range(1, layer + 1):
        acc = acc + ex[l]
    return acc / tot


def _gla_kernel(*refs, n_groups, layer, has_state):
    if has_state:
        (q_ref, v_ref, gate_ref, zf_ref, zb_ref, lbl_ref, g_ref, ids_ref, e_ref, s0_ref,
         o_ref, acc_ref) = refs
    else:
        (q_ref, v_ref, gate_ref, zf_ref, zb_ref, lbl_ref, g_ref, ids_ref, e_ref,
         o_ref, sout_ref, acc_ref) = refs
    e_mat = e_ref[...]
    z_refs = (zf_ref, zb_ref)
    lbs = [_lower_bound(lbl_ref[d], layer) for d in range(2)]

    def step(gi, sts):
        new = []
        for d in range(2):
            g = gi if d == 0 else n_groups - 1 - gi
            start = g * GLA_GROUP
            rows = pl.ds(start if isinstance(start, int) else pl.multiple_of(start, GLA_GROUP), GLA_GROUP)
            o, st = _gla_group(q_ref[rows, :].astype(F32), v_ref[rows, :], z_refs[d][rows, :], lbs[d],
                               ids_ref[d], e_mat, sts[d], backward=(d == 1))
            acc_ref[rows, :] = acc_ref[rows, :] + o
            new.append(st)
        return tuple(new)

    acc_ref[...] = jnp.zeros_like(acc_ref)
    if has_state:
        init = tuple(s0_ref[d].T for d in range(2))
        lax.fori_loop(0, n_groups, step, init)
    else:
        sts = step(0, (None, None))
        for d in range(2):
            sout_ref[d] = sts[d].T
    o = acc_ref[...]
    y = o * lax.rsqrt(jnp.mean(o * o, axis=-1, keepdims=True) + EPS) * g_ref[...]
    o_ref[...] = (y * gate_ref[...].astype(F32)).astype(BF16)


def _gla(q, v, gate, zf, zb, lb_logits, g_out, tabs, *, row0, nb, ln, layer, s0=None):
    ids, e_mat = tabs
    assert ln % GLA_GROUP == 0 and (s0 is not None or ln == GLA_GROUP)
    b0 = row0 // ln
    seq = pl.BlockSpec((ln, HD), lambda b, h: (b + b0, h))
    in_specs = [seq] * 5 + [
        pl.BlockSpec((2, DEPTH, HD), lambda b, h: (0, 0, h)),
        pl.BlockSpec((1, HD), lambda b, h: (0, 0)),
        pl.BlockSpec((2, GLA_GROUP, GLA_GROUP), lambda b, h: (0, 0, 0)),
        pl.BlockSpec((GLA_BLK * HD, GLA_GROUP), lambda b, h: (0, 0)),
    ]
    args = [q, v, gate, zf, zb, lb_logits, g_out, ids, e_mat]
    out_specs = [pl.BlockSpec((ln, HD), lambda b, h: (b, h))]
    out_shape = [jax.ShapeDtypeStruct((nb * ln, D), BF16)]
    state_spec = pl.BlockSpec((None, 2, None, HD, HD), lambda b, h: (b, 0, h, 0, 0))
    if s0 is not None:
        in_specs.append(state_spec)
        args.append(s0)
    else:
        out_specs.append(state_spec)
        out_shape.append(jax.ShapeDtypeStruct((nb, 2, HEADS, HD, HD), F32))
    kern = lambda *refs: _gla_kernel(*refs, n_groups=ln // GLA_GROUP, layer=layer, has_state=s0 is not None)
    return pl.pallas_call(
        kern,
        grid=(nb, HEADS),
        in_specs=in_specs,
        out_specs=out_specs,
        out_shape=out_shape,
        scratch_shapes=[pltpu.VMEM((ln, HD), F32)],
        compiler_params=_cparams(2),
        name="gla_sample" if s0 is not None else "gla_prompt",
    )(*args)


def _dft_tables(n):
    c = np.arange(FOUR_DG)
    ang_c = 2.0 * np.pi * ((c[:, None] * c[None, :]) % FOUR_DG) / FOUR_DG
    cs = np.concatenate([np.cos(ang_c), np.sin(ang_c)], axis=1)
    t = np.arange(n)
    ang_n = 2.0 * np.pi * ((t[:, None] * t[None, :]) % n) / n
    cn = np.concatenate([np.cos(ang_n), -np.sin(ang_n)], axis=1) / math.sqrt(n * FOUR_DG)
    return jnp.asarray(cs, F32).astype(BF16), jnp.asarray(cn, F32).astype(BF16)


def _fourier_kernel(x_ref, mod_ref, g_ref, cs_ref, cn_ref, o_ref, *, row_of):
    m = mod_ref[pl.ds(row_of(pl.program_id(0)), 1), :]
    h = _adanorm(x_ref[...], g_ref[...], m[:, 0:D], m[:, D:2 * D]).astype(BF16)
    cs = cs_ref[...]
    cn = cn_ref[...]
    for g in range(FOUR_GROUPS):
        xcs = _dot(h[:, g * FOUR_DG:(g + 1) * FOUR_DG], cs)
        stacked = jnp.concatenate([xcs[:, 0:FOUR_DG], xcs[:, FOUR_DG:]], axis=0).astype(BF16)
        o_ref[:, g * FOUR_DG:(g + 1) * FOUR_DG] = _dot(cn, stacked).astype(BF16)


def _fourier(x, mod_l, g, *, row0, nb, ln, row_of):
    cs, cn = _dft_tables(ln)
    b0 = row0 // ln
    kern = lambda *refs: _fourier_kernel(*refs, row_of=row_of)
    return pl.pallas_call(
        kern,
        grid=(nb,),
        in_specs=[pl.BlockSpec((ln, D), lambda b: (b + b0, 0)), _const_spec((MOD_ROWS, N_MOD * D)),
                  _const_spec((1, D)), _const_spec((FOUR_DG, 2 * FOUR_DG)), _const_spec((ln, 2 * ln))],
        out_specs=pl.BlockSpec((ln, D), lambda b: (b, 0)),
        out_shape=jax.ShapeDtypeStruct((nb * ln, D), BF16),
        compiler_params=_cparams(1),
        name="fourier_%d" % ln,
    )(x, mod_l, g, cs, cn)


FF_DEPTH_COLS = 4
FF_DEPTH_ROWS = 2


def _post_ffn_kernel(*refs, layer, wo_index, split_x, final):
    refs = list(refs)
    x_refs = [refs.pop(0) for _ in range(2 if split_x else 1)]
    op_ref, os_ref, mod_ref, wo_hbm, g_ref, win_hbm, wout_hbm = refs[:7]
    refs = refs[7:]
    gf_ref = refs.pop(0) if final else None
    out_refs = [refs.pop(0) for _ in range(2 if final else 1)]
    wo_bf, win_bf, wout_bf, stage_c, sem_c, stage_r, sem_r = refs
    n_chunks = D_FF // FF_CHUNK
    wo_jobs = _column_jobs(wo_hbm, wo_index, wo_bf, FF_CHUNK)
    in_jobs = []
    for c in range(n_chunks):
        for c0 in (c * FF_CHUNK, D_FF + c * FF_CHUNK):
            in_jobs.append((win_hbm.at[layer, :, pl.ds(c0, FF_CHUNK)], win_bf.at[:, pl.ds(c0, FF_CHUNK)]))
    cols = _WeightStream(wo_jobs + in_jobs, stage_c, sem_c)
    rows = _WeightStream([(wout_hbm.at[layer, pl.ds(c * FF_CHUNK, FF_CHUNK), :],
                           wout_bf.at[pl.ds(c * FF_CHUNK, FF_CHUNK), :]) for c in range(n_chunks)],
                         stage_r, sem_r)

    def body(first):
        if first:
            cols.prime()
            rows.prime()
        i = pl.program_id(0)
        ctx = i < NT_P
        m = mod_ref[pl.ds(_mod_row(i), 1), :]
        x = jnp.where(ctx, x_refs[0][...], x_refs[1][...]) if split_x else x_refs[0][...]
        o = jnp.where(ctx, op_ref[...], os_ref[...])
        if first:
            for k in range(len(wo_jobs)):
                cols.take(k)
        x = x + m[:, 2 * D:3 * D] * _dot(o, wo_bf[...])
        h = _adanorm(x, g_ref[...], m[:, 3 * D:4 * D], m[:, 4 * D:5 * D]).astype(BF16)
        acc = None
        for c in range(n_chunks):
            if first:
                cols.take(len(wo_jobs) + 2 * c)
                cols.take(len(wo_jobs) + 2 * c + 1)
                rows.take(c)
            gt = _dot(h, win_bf[:, c * FF_CHUNK:(c + 1) * FF_CHUNK])
            up = _dot(h, win_bf[:, D_FF + c * FF_CHUNK:D_FF + (c + 1) * FF_CHUNK])
            part = _dot((_silu(gt) * up).astype(BF16), wout_bf[c * FF_CHUNK:(c + 1) * FF_CHUNK, :])
            acc = part if acc is None else acc + part
        x = x + m[:, 5 * D:6 * D] * acc
        if not final:
            out_refs[0][...] = x
            return
        y = x * lax.rsqrt(jnp.mean(x * x, axis=-1, keepdims=True) + EPS) * gf_ref[...]
        if first:
            out_refs[0][...] = y
            return

        @pl.when(ctx)
        def _():
            out_refs[0][...] = y

        @pl.when(jnp.logical_not(ctx))
        def _():
            out_refs[1][...] = y

    _first_step_or_not(body)


def _post_ffn(xs, op, os_, mod_l, w_o, wo_index, g, w_in, w_out, layer, g_final=None):
    tile = pl.BlockSpec((TM, D), lambda i: (i, 0))
    p_tile = pl.BlockSpec((TM, D), lambda i: (jnp.minimum(i, NT_P - 1), 0))
    s_tile = pl.BlockSpec((TM, D), lambda i: (jnp.maximum(i - NT_P, 0), 0))
    split_x = len(xs) == 2
    final = g_final is not None
    in_specs = ([p_tile, s_tile] if split_x else [tile]) + [
        p_tile, s_tile, _const_spec((MOD_ROWS, N_MOD * D)), HBM_SPEC, _const_spec((1, D)), HBM_SPEC, HBM_SPEC]
    args = list(xs) + [op, os_, mod_l, w_o, g, w_in, w_out]
    if final:
        in_specs.append(_const_spec((1, D)))
        args.append(g_final)
        out_specs = [p_tile, s_tile]
        out_shape = [jax.ShapeDtypeStruct((N_P, D), F32), jax.ShapeDtypeStruct((N_S, D), F32)]
    else:
        out_specs = tile
        out_shape = jax.ShapeDtypeStruct((N_TOK, D), F32)
    kern = lambda *refs: _post_ffn_kernel(*refs, layer=layer, wo_index=wo_index, split_x=split_x, final=final)
    return pl.pallas_call(
        kern,
        grid=(N_TOK // TM,),
        in_specs=in_specs,
        out_specs=out_specs,
        out_shape=out_shape,
        scratch_shapes=([pltpu.VMEM((D, D), BF16), pltpu.VMEM((D, 2 * D_FF), BF16), pltpu.VMEM((D_FF, D), BF16)]
                        + _stream_scratch((D, FF_CHUNK), FF_DEPTH_COLS)
                        + _stream_scratch((FF_CHUNK, D), FF_DEPTH_ROWS)),
        compiler_params=_cparams(1),
        name="post_ffn",
    )(*args)


def kernel(x_prompt, x_sample, cache_attn_k, cache_attn_v, state_hgrn, c, c_ctx, w_ada, b_ada, g_norm_mix,
           g_norm_ffn, w_qkv_attn, lam_attn, g_subln_attn, w_o_attn, w_in_rec, lb_logits_rec, g_out_rec,
           w_o_rec, w_four, w_ffn_in, w_ffn_out, g_final):
    xs = (x_prompt.reshape(N_P, D), x_sample.reshape(N_S, D))
    cond = jnp.concatenate([c_ctx.reshape(1, D), c, jnp.zeros((MOD_ROWS - 1 - DEC_BATCH, D), F32)], axis=0)
    mod = _modulation(cond, w_ada, b_ada)
    cache_kt = jnp.transpose(cache_attn_k, (0, 1, 3, 4, 5, 2)).reshape(DEC_BATCH, -1, HEADS, HD, PAST)
    cache_v = cache_attn_v.reshape(DEC_BATCH, -1, PAST, D)
    rope_tabs = _rope_tables()
    gla_tabs = _gla_tables()
    gla_tabs = (gla_tabs[0], gla_tabs[1].astype(BF16))
    kv_ctx, new_s = None, []
    for i in range(DEPTH):
        kind, j = i % N_MIXERS, i // N_MIXERS
        g_mix = g_norm_mix[i].reshape(1, D)
        x_p = xs[0]
        x_s, s_tile0 = (xs[1], 0) if len(xs) == 2 else (xs[0], NT_P)
        if kind == 0:
            lam_init = 0.8 - 0.6 * math.exp(-0.3 * i)
            g_sub = g_subln_attn[j].reshape(1, HD)
            kt_all, v_all, op = _attn_prompt(x_p, mod[i], g_mix, w_qkv_attn, j, lam_attn[j], g_sub, lam_init,
                                             prev=kv_ctx)
            kv_ctx = (kt_all, v_all)
            qs, ks, vs = _qkv_sample(x_s, s_tile0, mod[i], g_mix, w_qkv_attn, j, rope_tabs)
            os_ = _attn_sample(qs, ks, vs, cache_kt, cache_v, j, lam_attn[j], g_sub, lam_init)
            w_o = w_o_attn
        elif kind == 1:
            proj = _rec_proj(xs[0], mod[i], g_mix, w_in_rec, j)
            g_out = g_out_rec[j].reshape(1, HD)
            op, s_ctx = _gla(*proj, lb_logits_rec, g_out, gla_tabs, row0=0, nb=BATCH, ln=SEQ, layer=i)
            new_s.append(s_ctx)
            os_ = _gla(*proj, lb_logits_rec, g_out, gla_tabs, row0=N_P, nb=DEC_BATCH, ln=DEC_SEQ, layer=i,
                       s0=state_hgrn[:, j])[0]
            w_o = w_o_rec
        else:
            op = _fourier(xs[0], mod[i], g_mix, row0=0, nb=BATCH, ln=SEQ, row_of=lambda b: 0)
            os_ = _fourier(xs[0], mod[i], g_mix, row0=N_P, nb=DEC_BATCH, ln=DEC_SEQ, row_of=lambda b: 1 + b)
            w_o = w_four
        out = _post_ffn(xs, op, os_, mod[i], w_o, j, g_norm_ffn[i].reshape(1, D), w_ffn_in, w_ffn_out, i,
                        g_final.reshape(1, D) if i == DEPTH - 1 else None)
        xs = tuple(out) if i == DEPTH - 1 else (out,)
    y_prompt = xs[0].reshape(BATCH, SEQ, D)
    y_sample = xs[1].reshape(DEC_BATCH, DEC_SEQ, D)
    new_k = jnp.transpose(kv_ctx[0].reshape(BATCH, -1, HEADS, 2, DH, SEQ), (0, 1, 5, 2, 3, 4))
    new_v = kv_ctx[1].reshape(BATCH, -1, SEQ, HEADS, HD)
    return (y_prompt, y_sample, new_k, new_v, jnp.stack(new_s, axis=1))
```

```python
import math

import numpy as np
import jax
import jax.numpy as jnp
from jax import lax
from jax.experimental import pallas as pl
from jax.experimental.pallas import tpu as pltpu

F32 = jnp.float32
BF16 = jnp.bfloat16

D = 1024
BATCH = 16
SEQ = 256
DEPTH = 4
DEC_BATCH = 4
DEC_SEQ = 1024
PAST = 512
GRID_W = 64
N_MIXERS = 3
HEADS = 8
DH = 64
HD = 128
ROPE_THETA = 10000.0
D_FF = 2816
N_MOD = 6
EPS = 1e-6
LOG2E = 1.0 / math.log(2.0)
FOUR_GROUPS = 4
FOUR_DG = 256

N_P = BATCH * SEQ
N_S = DEC_BATCH * DEC_SEQ
N_TOK = N_P + N_S
TM = 512
NT_P = N_P // TM
NT_S = N_S // TM
TILES_PER_DEC_SEQ = DEC_SEQ // TM
SEQ_PER_TILE = TM // SEQ
MOD_ROWS = 8
FF_CHUNK = 256
ATT_TQ = 256
ATT_HP = 4
ATT_P_WIDTH = 2
ATT_SKEW = 1
GLA_HP = 4
GLA_GROUP = 256
GLA_BLK = 8
GLA_LEVELS = (8, 16, 32, 64, 128)
VMEM_LIMIT = 60 * 1024 * 1024


def _cparams(n_axes):
    return pltpu.CompilerParams(dimension_semantics=("arbitrary",) * n_axes,
                                vmem_limit_bytes=VMEM_LIMIT)


def _const_spec(shape):
    nd = len(shape)
    return pl.BlockSpec(shape, lambda *_: (0,) * nd, pipeline_mode=pl.Buffered(1))


def _sigmoid(x):
    return 1.0 / (1.0 + jnp.exp(-x))


def _silu(x):
    return x * _sigmoid(x)


def _adanorm(x, g, shift, scale):
    ms = jnp.mean(x * x, axis=-1, keepdims=True)
    return x * lax.rsqrt(ms + EPS) * (g * (1.0 + scale)) + shift


def _mod_row(tile):
    return jnp.where(tile < NT_P, 0, 1 + (tile - NT_P) // TILES_PER_DEC_SEQ)


def _dot(a, b):
    return jnp.dot(a, b, preferred_element_type=F32)


def _dot_nt(a, b):
    return lax.dot_general(a, b, (((1,), (1,)), ((), ())), preferred_element_type=F32)


def _dot_tn(a, b):
    return lax.dot_general(a, b, (((0,), (0,)), ((), ())), preferred_element_type=F32)


def _split3(x):
    hi = x.astype(BF16)
    r1 = x - hi.astype(F32)
    mid = r1.astype(BF16)
    lo = (r1 - mid.astype(F32)).astype(BF16)
    return hi, mid, lo


def _interleave(generators, skew=0):
    results = [None] * len(generators)
    active = list(range(len(generators)))
    rnd = 0
    while active:
        for i in list(active):
            if rnd < i * skew:
                continue
            try:
                next(generators[i])
            except StopIteration as stop:
                results[i] = stop.value
                active.remove(i)
        rnd += 1
    return results


class _WeightStream:
    def __init__(self, jobs, stage_ref, sem_ref):
        self.jobs, self.stage, self.sem = jobs, stage_ref, sem_ref
        self.depth = stage_ref.shape[0]

    def _copy(self, k):
        slot = k % self.depth
        return pltpu.make_async_copy(self.jobs[k][0], self.stage.at[slot], self.sem.at[slot])

    def prime(self):
        for k in range(min(self.depth, len(self.jobs))):
            self._copy(k).start()

    def take(self, k):
        self._copy(k).wait()
        self.jobs[k][1][...] = self.stage[k % self.depth].astype(BF16)
        if k + self.depth < len(self.jobs):
            self._copy(k + self.depth).start()


def _stream_scratch(piece_shape, depth):
    return [pltpu.VMEM((depth,) + tuple(piece_shape), F32), pltpu.SemaphoreType.DMA((depth,))]


def _column_jobs(w_hbm, index, w_bf, width):
    return [(w_hbm.at[index, :, pl.ds(c, width)], w_bf.at[:, pl.ds(c, width)])
            for c in range(0, w_bf.shape[1], width)]


def _first_step_or_not(body):
    first = pl.program_id(0) == 0
    pl.when(first)(lambda: body(True))
    pl.when(jnp.logical_not(first))(lambda: body(False))


HBM_SPEC = pl.BlockSpec(memory_space=pl.ANY)
W_PIECE = 512
W_DEPTH = 3


MOD_TN = 1536


def _mod_kernel(cond_ref, w_ref, b_ref, o_ref):
    s = _silu(cond_ref[...])
    s_hi = s.astype(BF16)
    s_lo = (s - s_hi.astype(F32)).astype(BF16)
    w = w_ref[0]
    w_hi = w.astype(BF16)
    w_lo = (w - w_hi.astype(F32)).astype(BF16)
    o_ref[0] = _dot(s_hi, w_hi) + _dot(s_lo, w_hi) + _dot(s_hi, w_lo) + b_ref[0]


def _modulation(cond, w_ada, b_ada):
    return pl.pallas_call(
        _mod_kernel,
        grid=(DEPTH, N_MOD * D // MOD_TN),
        in_specs=[
            pl.BlockSpec((MOD_ROWS, D), lambda l, n: (0, 0)),
            pl.BlockSpec((1, D, MOD_TN), lambda l, n: (l, 0, n)),
            pl.BlockSpec((1, 1, MOD_TN), lambda l, n: (l, 0, n)),
        ],
        out_specs=pl.BlockSpec((1, MOD_ROWS, MOD_TN), lambda l, n: (l, 0, n)),
        out_shape=jax.ShapeDtypeStruct((DEPTH, MOD_ROWS, N_MOD * D), F32),
        compiler_params=_cparams(2),
        name="modulation",
    )(cond, w_ada, b_ada.reshape(DEPTH, 1, N_MOD * D))


def _rope_tables():
    t = np.arange(DEC_SEQ)
    row, col = t // GRID_W, t % GRID_W
    lane = np.arange(HD) % DH
    use_row = lane < DH // 2
    idx = (lane % (DH // 2)) % (DH // 4)
    inv = ROPE_THETA ** (-idx.astype(np.float64) / (DH // 4))
    pos = np.where(use_row[None, :], row[:, None], col[:, None]).astype(np.float64)
    ang = pos * inv[None, :]
    first = (lane % (DH // 2)) < DH // 4
    cos = np.cos(ang)
    sin = np.sin(ang)
    sin_minus = np.where(first[None, :], -sin, 0.0)
    sin_plus = np.where(first[None, :], 0.0, sin)
    return (jnp.asarray(cos, F32), jnp.asarray(sin_minus, F32), jnp.asarray(sin_plus, F32))


def _rope_tile(t, cos, sin_minus, sin_plus):
    return (t * cos + pltpu.roll(t, HD - DH // 4, 1) * sin_minus
            + pltpu.roll(t, DH // 4, 1) * sin_plus)


def _lambda(lam_ref, lam_init):
    lp = lam_ref[...]
    return (jnp.exp(jnp.sum(lp[0:1] * lp[1:2], axis=-1, keepdims=True))
            - jnp.exp(jnp.sum(lp[2:3] * lp[3:4], axis=-1, keepdims=True)) + lam_init)


def _with_ones(v):
    return jnp.concatenate([v, jnp.ones_like(v)], axis=1)


def _diff_attn_head(q, score_fns, values, lam, g_scaled):
    tq = q.shape[0]
    lane = lax.broadcasted_iota(jnp.int32, q.shape, 1)
    zero = jnp.zeros_like(q)
    qq = jnp.concatenate([jnp.where(lane < DH, q, zero), jnp.where(lane >= DH, q, zero)], axis=0)
    scores = [fn(qq) for fn in score_fns]
    yield
    mx = scores[0].max(axis=-1, keepdims=True)
    for s in scores[1:]:
        mx = jnp.maximum(mx, s.max(axis=-1, keepdims=True))
    yield
    acc = None
    for s, v in zip(scores, values):
        c = _dot(jnp.exp2(s - mx).astype(BF16), v)
        acc = c if acc is None else acc + c
        yield
    r = acc[:, 0:HD] / acc[:, HD:2 * HD]
    o = r[:tq] - lam * r[tq:]
    o = o * lax.rsqrt(jnp.mean(o * o, axis=-1, keepdims=True) + EPS) * g_scaled
    return o.astype(BF16)


def _attn_prompt_kernel(*refs, j, n_layers, lam_init):
    x_ref, mod_ref, g_ref, w_hbm, lam_ref, gsub_ref = refs[:6]
    kt_ref, v_ref, o_ref, w_bf, stage, sem = refs[-6:]
    owns_all = j == 0
    stream = _WeightStream(_column_jobs(w_hbm, j, w_bf, W_PIECE), stage, sem)
    per_mat = D // W_PIECE

    def body(first):
        if first:
            stream.prime()
        m = mod_ref[pl.ds(0, 1), :]
        h = _adanorm(x_ref[...], g_ref[...], m[:, 0:D], m[:, D:2 * D]).astype(BF16)
        lam = _lambda(lam_ref, lam_init)
        g_scaled = gsub_ref[...] * (1.0 - lam_init)
        mats = []
        for n in range(3):
            if first:
                for p in range(per_mat):
                    stream.take(n * per_mat + p)
            mats.append(_dot(h, w_bf[:, n * D:(n + 1) * D]))
        q = (mats[0] * (DH ** -0.5 * LOG2E)).astype(BF16)
        k, v = mats[1], mats[2]
        kb, vb = k.astype(BF16), v.astype(BF16)
        for s in range(SEQ_PER_TILE):
            rows = slice(s * SEQ, (s + 1) * SEQ)
            if owns_all:
                for jj in range(n_layers):
                    kt_ref[s, jj] = k[rows].T if jj == j else jnp.zeros((D, SEQ), F32)
                    v_ref[s, jj] = v[rows] if jj == j else jnp.zeros((SEQ, D), F32)
            else:
                kt_ref[s] = k[rows].T
                v_ref[s] = v[rows]
            for h0 in range(0, HEADS, ATT_P_WIDTH):
                heads = []
                for hh in range(h0, h0 + ATT_P_WIDTH):
                    cols = slice(hh * HD, (hh + 1) * HD)
                    kh = kb[rows, cols]
                    heads.append(_diff_attn_head(q[rows, cols], [lambda qq, kh=kh: _dot_nt(qq, kh)],
                                                 [_with_ones(vb[rows, cols])], lam, g_scaled))
                for hh, o in zip(range(h0, h0 + ATT_P_WIDTH), _interleave(heads, ATT_SKEW)):
                    o_ref[rows, hh * HD:(hh + 1) * HD] = o

    _first_step_or_not(body)


def _attn_prompt(x, mod_l, g, w_qkv, j, lam_p, g_sub, lam_init, prev=None):
    n_layers = w_qkv.shape[0]
    assert (prev is None) == (j == 0)
    tile = pl.BlockSpec((TM, D), lambda i: (i, 0))
    in_specs = [tile, _const_spec((MOD_ROWS, N_MOD * D)), _const_spec((1, D)), HBM_SPEC,
                _const_spec((4, DH)), _const_spec((1, HD))]
    args = [x, mod_l, g, w_qkv, lam_p, g_sub]
    if prev is None:
        kt_spec = pl.BlockSpec((SEQ_PER_TILE, n_layers, D, SEQ), lambda i: (i, 0, 0, 0))
        v_spec = pl.BlockSpec((SEQ_PER_TILE, n_layers, SEQ, D), lambda i: (i, 0, 0, 0))
        aliases = {}
    else:
        kt_spec = pl.BlockSpec((SEQ_PER_TILE, None, D, SEQ), lambda i: (i, j, 0, 0))
        v_spec = pl.BlockSpec((SEQ_PER_TILE, None, SEQ, D), lambda i: (i, j, 0, 0))
        aliases = {len(args): 0, len(args) + 1: 1}
        in_specs += [HBM_SPEC, HBM_SPEC]
        args += list(prev)
    kern = lambda *refs: _attn_prompt_kernel(*refs, j=j, n_layers=n_layers, lam_init=lam_init)
    return pl.pallas_call(
        kern,
        grid=(NT_P,),
        in_specs=in_specs,
        out_specs=[kt_spec, v_spec, tile],
        out_shape=[jax.ShapeDtypeStruct((BATCH, n_layers, D, SEQ), F32),
                   jax.ShapeDtypeStruct((BATCH, n_layers, SEQ, D), F32), jax.ShapeDtypeStruct((N_P, D), BF16)],
        input_output_aliases=aliases,
        scratch_shapes=[pltpu.VMEM((D, 3 * D), BF16)] + _stream_scratch((D, W_PIECE), W_DEPTH),
        compiler_params=_cparams(1),
        name="attn_prompt",
    )(*args)


def _qkv_sample_kernel(x_ref, mod_ref, g_ref, w_hbm, cos_ref, sm_ref, sp_ref, q_ref, k_ref, v_ref,
                       w_bf, stage, sem, *, j):
    stream = _WeightStream(_column_jobs(w_hbm, j, w_bf, W_PIECE), stage, sem)
    per_mat = D // W_PIECE

    def body(first):
        if first:
            stream.prime()
        i = pl.program_id(0)
        m = mod_ref[pl.ds(1 + i // TILES_PER_DEC_SEQ, 1), :]
        h = _adanorm(x_ref[...], g_ref[...], m[:, 0:D], m[:, D:2 * D]).astype(BF16)
        cos, sm, sp = cos_ref[...], sm_ref[...], sp_ref[...]
        for n, (o_ref, scale) in enumerate(((q_ref, DH ** -0.5 * LOG2E), (k_ref, None), (v_ref, None))):
            if first:
                for p in range(per_mat):
                    stream.take(n * per_mat + p)
            y = _dot(h, w_bf[:, n * D:(n + 1) * D])
            if o_ref is v_ref:
                o_ref[...] = y.astype(BF16)
                continue
            if scale is not None:
                y = y * scale
            for hh in range(HEADS):
                t = _rope_tile(y[:, hh * HD:(hh + 1) * HD], cos, sm, sp)
                o_ref[:, hh * HD:(hh + 1) * HD] = t.astype(BF16)

    _first_step_or_not(body)


def _qkv_sample(x, tile0, mod_l, g, w_qkv, j, tabs):
    tile = pl.BlockSpec((TM, D), lambda i: (i, 0))
    tab = pl.BlockSpec((TM, HD), lambda i: (i % TILES_PER_DEC_SEQ, 0))
    kern = lambda *refs: _qkv_sample_kernel(*refs, j=j)
    return pl.pallas_call(
        kern,
        grid=(NT_S,),
        in_specs=[pl.BlockSpec((TM, D), lambda i: (i + tile0, 0)), _const_spec((MOD_ROWS, N_MOD * D)),
                  _const_spec((1, D)), HBM_SPEC, tab, tab, tab],
        out_specs=[tile, tile, tile],
        out_shape=[jax.ShapeDtypeStruct((N_S, D), BF16)] * 3,
        scratch_shapes=[pltpu.VMEM((D, 3 * D), BF16)] + _stream_scratch((D, W_PIECE), W_DEPTH),
        compiler_params=_cparams(1),
        name="qkv_sample",
    )(x, mod_l, g, w_qkv, *tabs)


def _attn_sample_kernel(*refs, lam_init):
    q_ref, kn_ref, vn_ref, kct_ref, vc_ref, lam_ref, gsub_ref, o_ref = refs
    lam = _lambda(lam_ref, lam_init)
    g_scaled = gsub_ref[...] * (1.0 - lam_init)
    heads = []
    for hh in range(ATT_HP):
        cols = slice(hh * HD, (hh + 1) * HD)
        kn = kn_ref[:, cols]
        kct = kct_ref[hh].astype(BF16)
        vc = vc_ref[:, cols].astype(BF16)
        heads.append(_diff_attn_head(
            q_ref[:, cols], [lambda qq, kn=kn: _dot_nt(qq, kn), lambda qq, kct=kct: _dot(qq, kct)],
            [_with_ones(vn_ref[:, cols]), _with_ones(vc)], lam, g_scaled))
    for hh, o in enumerate(_interleave(heads, ATT_SKEW)):
        o_ref[:, hh * HD:(hh + 1) * HD] = o


def _attn_sample(q, kn, vn, cache_kt, cache_v, j, lam_p, g_sub, lam_init):
    nq = DEC_SEQ // ATT_TQ
    w = ATT_HP * HD
    in_specs = [
        pl.BlockSpec((ATT_TQ, w), lambda b, h, i: (b * nq + i, h)),
        pl.BlockSpec((DEC_SEQ, w), lambda b, h, i: (b, h)),
        pl.BlockSpec((DEC_SEQ, w), lambda b, h, i: (b, h)),
        pl.BlockSpec((None, None, ATT_HP, HD, PAST), lambda b, h, i: (b, j, h, 0, 0)),
    ]
    in_specs += [pl.BlockSpec((None, None, PAST, w), lambda b, h, i: (b, j, 0, h)),
                 pl.BlockSpec((4, DH), lambda b, h, i: (0, 0)), pl.BlockSpec((1, HD), lambda b, h, i: (0, 0))]
    kern = lambda *refs: _attn_sample_kernel(*refs, lam_init=lam_init)
    return pl.pallas_call(
        kern,
        grid=(DEC_BATCH, HEADS // ATT_HP, nq),
        in_specs=in_specs,
        out_specs=pl.BlockSpec((ATT_TQ, w), lambda b, h, i: (b * nq + i, h)),
        out_shape=jax.ShapeDtypeStruct((N_S, D), BF16),
        compiler_params=_cparams(3),
        name="attn_sample",
    )(q, kn, vn, cache_kt, cache_v, lam_p, g_sub)


def _rec_proj_kernel(x_ref, mod_ref, g_ref, w_hbm, q_ref, v_ref, gate_ref, zf_ref, zb_ref,
                     w_bf, stage, sem, *, j):
    stream = _WeightStream(_column_jobs(w_hbm, j, w_bf, W_PIECE), stage, sem)
    per_mat = D // W_PIECE

    def body(first):
        if first:
            stream.prime()
        m = mod_ref[pl.ds(_mod_row(pl.program_id(0)), 1), :]
        h = _adanorm(x_ref[...], g_ref[...], m[:, 0:D], m[:, D:2 * D]).astype(BF16)
        for n, o_ref in enumerate((q_ref, v_ref, gate_ref, zf_ref, zb_ref)):
            if first:
                for p in range(per_mat):
                    stream.take(n * per_mat + p)
            y = _dot(h, w_bf[:, n * D:(n + 1) * D])
            if o_ref is q_ref or o_ref is gate_ref:
                y = _silu(y)
            o_ref[...] = y.astype(o_ref.dtype)

    _first_step_or_not(body)


def _rec_proj(x, mod_l, g, w_in, j):
    tile = pl.BlockSpec((TM, D), lambda i: (i, 0))
    kern = lambda *refs: _rec_proj_kernel(*refs, j=j)
    return pl.pallas_call(
        kern,
        grid=(N_TOK // TM,),
        in_specs=[tile, _const_spec((MOD_ROWS, N_MOD * D)), _const_spec((1, D)), HBM_SPEC],
        out_specs=[tile] * 5,
        out_shape=[jax.ShapeDtypeStruct((N_TOK, D), BF16)] * 3 + [jax.ShapeDtypeStruct((N_TOK, D), F32)] * 2,
        scratch_shapes=[pltpu.VMEM((D, 5 * D), BF16)] + _stream_scratch((D, W_PIECE), W_DEPTH),
        compiler_params=_cparams(1),
        name="rec_proj",
    )(x, mod_l, g, w_in)


def _gla_tables():
    t = np.arange(GLA_GROUP)[:, None]
    s = np.arange(GLA_GROUP)[None, :]
    ids = np.full((GLA_GROUP, GLA_GROUP), -1, np.int32)
    for l, b in reversed(list(enumerate(GLA_LEVELS, 1))):
        ids = np.where((t // (2 * b) == s // (2 * b)) & (s < t), l, ids)
    ids = np.where((t // GLA_BLK == s // GLA_BLK) & (s <= t), 0, ids)
    ids = np.stack([ids, ids.T]).astype(np.int32)
    e = np.zeros((GLA_BLK * HD, GLA_GROUP), np.float32)
    for j in range(GLA_BLK):
        e[j * HD:(j + 1) * HD, j::GLA_BLK] = 1.0
    return jnp.asarray(ids), jnp.asarray(e)


def _bcast_row(x, blk, idx):
    n, w = x.shape
    r = x.reshape(n // blk, blk, w)[:, idx:idx + 1, :]
    return jnp.broadcast_to(r, (n // blk, blk, w)).reshape(n, w)


def _gla_group(q, v, z, lb, ids, e_mat, st, backward):
    n = GLA_GROUP
    sig = _sigmoid(z)
    f = lb + (1.0 - lb) * sig
    k = (1.0 - lb) * (1.0 - sig)
    lf = jnp.log(f) * LOG2E
    row = lax.broadcasted_iota(jnp.int32, (n, n), 0)
    col = lax.broadcasted_iota(jnp.int32, (n, n), 1)
    tri = jnp.where((col >= row) if backward else (col <= row), 1.0, 0.0).astype(BF16)
    c3 = _dot(tri, jnp.concatenate(_split3(lf), axis=1))
    cum = c3[:, 0:HD] + c3[:, HD:2 * HD] + c3[:, 2 * HD:3 * HD]
    ck = jnp.log(k) * LOG2E - cum
    yield

    prods = []
    for j in range(GLA_BLK):
        prods.append((q * jnp.exp2(jnp.minimum(cum + _bcast_row(ck, GLA_BLK, j), 0.0))).astype(BF16))
        yield
    half = n // 2
    ids_q = ids[0:half, 0:half]
    diag = lambda x, h: x[h * half:(h + 1) * half, h * half:(h + 1) * half]
    pair = _dot(jnp.concatenate(prods, axis=1), e_mat)
    quads = [jnp.where(ids_q == 0, diag(pair, h), 0.0) for h in range(2)]
    yield

    for l, b in enumerate(GLA_LEVELS[:-1], 1):
        edge = _bcast_row(cum, 2 * b, b if backward else b - 1)
        ql = (q * jnp.exp2(cum - edge)).astype(BF16)
        kl = jnp.exp2(edge + ck).astype(BF16)
        x = _dot_nt(ql, kl)
        quads = [jnp.where(ids_q == l, diag(x, h), quads[h]) for h in range(2)]
        yield

    assert GLA_LEVELS[-1] == half
    lo, hi = slice(0, half), slice(half, n)
    qs, ks = (lo, hi) if backward else (hi, lo)
    edge = cum[half:half + 1, :] if backward else cum[half - 1:half, :]
    cross = _dot_nt((q[qs] * jnp.exp2(cum[qs] - edge)).astype(BF16),
                    jnp.exp2(edge + ck[ks]).astype(BF16)).astype(BF16)
    o_halves = [_dot(quads[h].astype(BF16), v[h * half:(h + 1) * half]) for h in range(2)]
    o_halves[0 if backward else 1] += _dot(cross, v[ks])
    o = jnp.concatenate(o_halves, axis=0)
    yield

    total = cum[0:1, :] if backward else cum[n - 1:n, :]
    if st is not None:
        o = o + _dot_nt((q * jnp.exp2(cum)).astype(BF16), st.astype(BF16))
    kdec = jnp.exp2(total + ck).astype(BF16)
    upd = _dot_tn(v, kdec)
    st_new = upd if st is None else st * jnp.exp2(total) + upd
    return o, st_new


def _lower_bound(x, layer):
    rows = [x[l:l + 1, :] for l in range(DEPTH)]
    mx = rows[0]
    for r in rows[1:]:
        mx = jnp.maximum(mx, r)
    ex = [jnp.exp(r - mx) for r in rows]
    tot = ex[0]
    for e in ex[1:]:
        tot = tot + e
    acc = ex[0] * 0.0
    for l in range(1, layer + 1):
        acc = acc + ex[l]
    return acc / tot


def _gla_kernel(*refs, n_groups, layer, has_state):
    if has_state:
        (q_ref, v_ref, gate_ref, zf_ref, zb_ref, lbl_ref, g_ref, ids_ref, e_ref, s0_ref,
         o_ref, acc_ref) = refs
    else:
        (q_ref, v_ref, gate_ref, zf_ref, zb_ref, lbl_ref, g_ref, ids_ref, e_ref,
         o_ref, sout_ref, acc_ref) = refs
    e_mat = e_ref[...]
    z_refs = (zf_ref, zb_ref)
    chains = [(hh, d) for hh in range(GLA_HP) for d in range(2)]
    cols = [slice(hh * HD, (hh + 1) * HD) for hh in range(GLA_HP)]
    lbs = {(hh, d): _lower_bound(lbl_ref[d, :, cols[hh]], layer) for hh, d in chains}

    def step(gi, sts):
        rows, gens = [], []
        for d in range(2):
            g = gi if d == 0 else n_groups - 1 - gi
            start = g * GLA_GROUP
            rows.append(pl.ds(start if isinstance(start, int) else pl.multiple_of(start, GLA_GROUP), GLA_GROUP))
        for c, (hh, d) in enumerate(chains):
            gens.append(_gla_group(q_ref[rows[d], cols[hh]].astype(F32), v_ref[rows[d], cols[hh]],
                                   z_refs[d][rows[d], cols[hh]], lbs[hh, d], ids_ref[d], e_mat, sts[c],
                                   backward=(d == 1)))
        new = []
        for (hh, d), (o, st) in zip(chains, _interleave(gens)):
            acc_ref[rows[d], cols[hh]] = acc_ref[rows[d], cols[hh]] + o
            new.append(st)
        return tuple(new)

    acc_ref[...] = jnp.zeros_like(acc_ref)
    if has_state:
        init = tuple(s0_ref[d, hh].T for hh, d in chains)
        lax.fori_loop(0, n_groups, step, init)
    else:
        sts = step(0, (None,) * len(chains))
        for (hh, d), st in zip(chains, sts):
            sout_ref[d, hh] = st.T
    for hh in range(GLA_HP):
        o = acc_ref[:, cols[hh]]
        y = o * lax.rsqrt(jnp.mean(o * o, axis=-1, keepdims=True) + EPS) * g_ref[...]
        o_ref[:, cols[hh]] = (y * gate_ref[:, cols[hh]].astype(F32)).astype(BF16)


def _gla(q, v, gate, zf, zb, lb_logits, g_out, tabs, *, row0, nb, ln, layer, s0=None):
    ids, e_mat = tabs
    assert ln % GLA_GROUP == 0 and (s0 is not None or ln == GLA_GROUP)
    b0 = row0 // ln
    w = GLA_HP * HD
    seq = pl.BlockSpec((ln, w), lambda b, h: (b + b0, h))
    in_specs = [seq] * 5 + [
        pl.BlockSpec((2, DEPTH, w), lambda b, h: (0, 0, h)),
        pl.BlockSpec((1, HD), lambda b, h: (0, 0)),
        pl.BlockSpec((2, GLA_GROUP, GLA_GROUP), lambda b, h: (0, 0, 0)),
        pl.BlockSpec((GLA_BLK * HD, GLA_GROUP), lambda b, h: (0, 0)),
    ]
    args = [q, v, gate, zf, zb, lb_logits, g_out, ids, e_mat]
    out_specs = [pl.BlockSpec((ln, w), lambda b, h: (b, h))]
    out_shape = [jax.ShapeDtypeStruct((nb * ln, D), BF16)]
    state_spec = pl.BlockSpec((None, 2, GLA_HP, HD, HD), lambda b, h: (b, 0, h, 0, 0))
    if s0 is not None:
        in_specs.append(state_spec)
        args.append(s0)
    else:
        out_specs.append(state_spec)
        out_shape.append(jax.ShapeDtypeStruct((nb, 2, HEADS, HD, HD), F32))
    kern = lambda *refs: _gla_kernel(*refs, n_groups=ln // GLA_GROUP, layer=layer, has_state=s0 is not None)
    return pl.pallas_call(
        kern,
        grid=(nb, HEADS // GLA_HP),
        in_specs=in_specs,
        out_specs=out_specs,
        out_shape=out_shape,
        scratch_shapes=[pltpu.VMEM((ln, w), F32)],
        compiler_params=_cparams(2),
        name="gla_sample" if s0 is not None else "gla_prompt",
    )(*args)


def _dft_tables(n):
    c = np.arange(FOUR_DG)
    ang_c = 2.0 * np.pi * ((c[:, None] * c[None, :]) % FOUR_DG) / FOUR_DG
    cs = np.concatenate([np.cos(ang_c), np.sin(ang_c)], axis=1)
    t = np.arange(n)
    ang_n = 2.0 * np.pi * ((t[:, None] * t[None, :]) % n) / n
    cn = np.concatenate([np.cos(ang_n), -np.sin(ang_n)], axis=1) / math.sqrt(n * FOUR_DG)
    return jnp.asarray(cs, F32).astype(BF16), jnp.asarray(cn, F32).astype(BF16)


def _fourier_kernel(x_ref, mod_ref, g_ref, cs_ref, cn_ref, o_ref, *, row_of):
    m = mod_ref[pl.ds(row_of(pl.program_id(0)), 1), :]
    h = _adanorm(x_ref[...], g_ref[...], m[:, 0:D], m[:, D:2 * D]).astype(BF16)
    cs = cs_ref[...]
    cn = cn_ref[...]
    for g in range(FOUR_GROUPS):
        xcs = _dot(h[:, g * FOUR_DG:(g + 1) * FOUR_DG], cs)
        stacked = jnp.concatenate([xcs[:, 0:FOUR_DG], xcs[:, FOUR_DG:]], axis=0).astype(BF16)
        o_ref[:, g * FOUR_DG:(g + 1) * FOUR_DG] = _dot(cn, stacked).astype(BF16)


def _fourier(x, mod_l, g, *, row0, nb, ln, row_of):
    cs, cn = _dft_tables(ln)
    b0 = row0 // ln
    kern = lambda *refs: _fourier_kernel(*refs, row_of=row_of)
    return pl.pallas_call(
        kern,
        grid=(nb,),
        in_specs=[pl.BlockSpec((ln, D), lambda b: (b + b0, 0)), _const_spec((MOD_ROWS, N_MOD * D)),
                  _const_spec((1, D)), _const_spec((FOUR_DG, 2 * FOUR_DG)), _const_spec((ln, 2 * ln))],
        out_specs=pl.BlockSpec((ln, D), lambda b: (b, 0)),
        out_shape=jax.ShapeDtypeStruct((nb * ln, D), BF16),
        compiler_params=_cparams(1),
        name="fourier_%d" % ln,
    )(x, mod_l, g, cs, cn)


FF_DEPTH_COLS = 4
FF_DEPTH_ROWS = 2


def _post_ffn_kernel(*refs, layer, wo_index, split_x, final):
    refs = list(refs)
    x_refs = [refs.pop(0) for _ in range(2 if split_x else 1)]
    op_ref, os_ref, mod_ref, wo_hbm, g_ref, win_hbm, wout_hbm = refs[:7]
    refs = refs[7:]
    gf_ref = refs.pop(0) if final else None
    out_refs = [refs.pop(0) for _ in range(2 if final else 1)]
    wo_bf, win_bf, wout_bf, stage_c, sem_c, stage_r, sem_r = refs
    n_chunks = D_FF // FF_CHUNK
    wo_jobs = _column_jobs(wo_hbm, wo_index, wo_bf, FF_CHUNK)
    in_jobs = []
    for c in range(n_chunks):
        for c0 in (c * FF_CHUNK, D_FF + c * FF_CHUNK):
            in_jobs.append((win_hbm.at[layer, :, pl.ds(c0, FF_CHUNK)], win_bf.at[:, pl.ds(c0, FF_CHUNK)]))
    cols = _WeightStream(wo_jobs + in_jobs, stage_c, sem_c)
    rows = _WeightStream([(wout_hbm.at[layer, pl.ds(c * FF_CHUNK, FF_CHUNK), :],
                           wout_bf.at[pl.ds(c * FF_CHUNK, FF_CHUNK), :]) for c in range(n_chunks)],
                         stage_r, sem_r)

    def body(first):
        if first:
            cols.prime()
            rows.prime()
        i = pl.program_id(0)
        ctx = i < NT_P
        m = mod_ref[pl.ds(_mod_row(i), 1), :]
        x = jnp.where(ctx, x_refs[0][...], x_refs[1][...]) if split_x else x_refs[0][...]
        o = jnp.where(ctx, op_ref[...], os_ref[...])
        if first:
            for k in range(len(wo_jobs)):
                cols.take(k)
        x = x + m[:, 2 * D:3 * D] * _dot(o, wo_bf[...])
        h = _adanorm(x, g_ref[...], m[:, 3 * D:4 * D], m[:, 4 * D:5 * D]).astype(BF16)
        acc = None
        for c in range(n_chunks):
            if first:
                cols.take(len(wo_jobs) + 2 * c)
                cols.take(len(wo_jobs) + 2 * c + 1)
                rows.take(c)
            gt = _dot(h, win_bf[:, c * FF_CHUNK:(c + 1) * FF_CHUNK])
            up = _dot(h, win_bf[:, D_FF + c * FF_CHUNK:D_FF + (c + 1) * FF_CHUNK])
            part = _dot((_silu(gt) * up).astype(BF16), wout_bf[c * FF_CHUNK:(c + 1) * FF_CHUNK, :])
            acc = part if acc is None else acc + part
        x = x + m[:, 5 * D:6 * D] * acc
        if not final:
            out_refs[0][...] = x
            return
        y = x * lax.rsqrt(jnp.mean(x * x, axis=-1, keepdims=True) + EPS) * gf_ref[...]
        if first:
            out_refs[0][...] = y
            return

        @pl.when(ctx)
        def _():
            out_refs[0][...] = y

        @pl.when(jnp.logical_not(ctx))
        def _():
            out_refs[1][...] = y

    _first_step_or_not(body)


def _post_ffn(xs, op, os_, mod_l, w_o, wo_index, g, w_in, w_out, layer, g_final=None):
    tile = pl.BlockSpec((TM, D), lambda i: (i, 0))
    p_tile = pl.BlockSpec((TM, D), lambda i: (jnp.minimum(i, NT_P - 1), 0))
    s_tile = pl.BlockSpec((TM, D), lambda i: (jnp.maximum(i - NT_P, 0), 0))
    split_x = len(xs) == 2
    final = g_final is not None
    in_specs = ([p_tile, s_tile] if split_x else [tile]) + [
        p_tile, s_tile, _const_spec((MOD_ROWS, N_MOD * D)), HBM_SPEC, _const_spec((1, D)), HBM_SPEC, HBM_SPEC]
    args = list(xs) + [op, os_, mod_l, w_o, g, w_in, w_out]
    if final:
        in_specs.append(_const_spec((1, D)))
        args.append(g_final)
        out_specs = [p_tile, s_tile]
        out_shape = [jax.ShapeDtypeStruct((N_P, D), F32), jax.ShapeDtypeStruct((N_S, D), F32)]
    else:
        out_specs = tile
        out_shape = jax.ShapeDtypeStruct((N_TOK, D), F32)
    kern = lambda *refs: _post_ffn_kernel(*refs, layer=layer, wo_index=wo_index, split_x=split_x, final=final)
    return pl.pallas_call(
        kern,
        grid=(N_TOK // TM,),
        in_specs=in_specs,
        out_specs=out_specs,
        out_shape=out_shape,
        scratch_shapes=([pltpu.VMEM((D, D), BF16), pltpu.VMEM((D, 2 * D_FF), BF16), pltpu.VMEM((D_FF, D), BF16)]
                        + _stream_scratch((D, FF_CHUNK), FF_DEPTH_COLS)
                        + _stream_scratch((FF_CHUNK, D), FF_DEPTH_ROWS)),
        compiler_params=_cparams(1),
        name="post_ffn",
    )(*args)


def kernel(x_prompt, x_sample, cache_attn_k, cache_attn_v, state_hgrn, c, c_ctx, w_ada, b_ada, g_norm_mix,
           g_norm_ffn, w_qkv_attn, lam_attn, g_subln_attn, w_o_attn, w_in_rec, lb_logits_rec, g_out_rec,
           w_o_rec, w_four, w_ffn_in, w_ffn_out, g_final):
    xs = (x_prompt.reshape(N_P, D), x_sample.reshape(N_S, D))
    cond = jnp.concatenate([c_ctx.reshape(1, D), c, jnp.zeros((MOD_ROWS - 1 - DEC_BATCH, D), F32)], axis=0)
    mod = _modulation(cond, w_ada, b_ada)
    cache_kt = jnp.transpose(cache_attn_k, (0, 1, 3, 4, 5, 2)).reshape(DEC_BATCH, -1, HEADS, HD, PAST)
    cache_v = cache_attn_v.reshape(DEC_BATCH, -1, PAST, D)
    rope_tabs = _rope_tables()
    gla_tabs = _gla_tables()
    gla_tabs = (gla_tabs[0], gla_tabs[1].astype(BF16))
    kv_ctx, new_s = None, []
    for i in range(DEPTH):
        kind, j = i % N_MIXERS, i // N_MIXERS
        g_mix = g_norm_mix[i].reshape(1, D)
        x_p = xs[0]
        x_s, s_tile0 = (xs[1], 0) if len(xs) == 2 else (xs[0], NT_P)
        if kind == 0:
            lam_init = 0.8 - 0.6 * math.exp(-0.3 * i)
            g_sub = g_subln_attn[j].reshape(1, HD)
            kt_all, v_all, op = _attn_prompt(x_p, mod[i], g_mix, w_qkv_attn, j, lam_attn[j], g_sub, lam_init,
                                             prev=kv_ctx)
            kv_ctx = (kt_all, v_all)
            qs, ks, vs = _qkv_sample(x_s, s_tile0, mod[i], g_mix, w_qkv_attn, j, rope_tabs)
            os_ = _attn_sample(qs, ks, vs, cache_kt, cache_v, j, lam_attn[j], g_sub, lam_init)
            w_o = w_o_attn
        elif kind == 1:
            proj = _rec_proj(xs[0], mod[i], g_mix, w_in_rec, j)
            g_out = g_out_rec[j].reshape(1, HD)
            op, s_ctx = _gla(*proj, lb_logits_rec, g_out, gla_tabs, row0=0, nb=BATCH, ln=SEQ, layer=i)
            new_s.append(s_ctx)
            os_ = _gla(*proj, lb_logits_rec, g_out, gla_tabs, row0=N_P, nb=DEC_BATCH, ln=DEC_SEQ, layer=i,
                       s0=state_hgrn[:, j])[0]
            w_o = w_o_rec
        else:
            op = _fourier(xs[0], mod[i], g_mix, row0=0, nb=BATCH, ln=SEQ, row_of=lambda b: 0)
            os_ = _fourier(xs[0], mod[i], g_mix, row0=N_P, nb=DEC_BATCH, ln=DEC_SEQ, row_of=lambda b: 1 + b)
            w_o = w_four
        out = _post_ffn(xs, op, os_, mod[i], w_o, j, g_norm_ffn[i].reshape(1, D), w_ffn_in, w_ffn_out, i,
                        g_final.reshape(1, D) if i == DEPTH - 1 else None)
        xs = tuple(out) if i == DEPTH - 1 else (out,)
    y_prompt = xs[0].reshape(BATCH, SEQ, D)
    y_sample = xs[1].reshape(DEC_BATCH, DEC_SEQ, D)
    new_k = jnp.transpose(kv_ctx[0].reshape(BATCH, -1, HEADS, 2, DH, SEQ), (0, 1, 5, 2, 3, 4))
    new_v = kv_ctx[1].reshape(BATCH, -1, SEQ, HEADS, HD)
    return (y_prompt, y_sample, new_k, new_v, jnp.stack(new_s, axis=1))
```

```python
import math

import numpy as np
import jax
import jax.numpy as jnp
from jax import lax
from jax.experimental import pallas as pl
from jax.experimental.pallas import tpu as pltpu

F32 = jnp.float32
BF16 = jnp.bfloat16

D = 1024
BATCH = 16
SEQ = 256
DEPTH = 4
DEC_BATCH = 4
DEC_SEQ = 1024
PAST = 512
GRID_W = 64
N_MIXERS = 3
HEADS = 8
DH = 64
HD = 128
ROPE_THETA = 10000.0
D_FF = 2816
N_MOD = 6
EPS = 1e-6
LOG2E = 1.0 / math.log(2.0)
FOUR_GROUPS = 4
FOUR_DG = 256

N_P = BATCH * SEQ
N_S = DEC_BATCH * DEC_SEQ
N_TOK = N_P + N_S
TM = 512
NT_P = N_P // TM
NT_S = N_S // TM
TILES_PER_DEC_SEQ = DEC_SEQ // TM
SEQ_PER_TILE = TM // SEQ
MOD_ROWS = 8
FF_CHUNK = 256
ATT_TQ = 256
ATT_HP = 4
ATT_P_WIDTH = 2
ATT_SKEW = 1
GLA_HP = 4
GLA_GROUP = 256
GLA_BLK = 8
GLA_LEVELS = (8, 16, 32, 64, 128)
VMEM_LIMIT = 60 * 1024 * 1024


def _cparams(n_axes):
    return pltpu.CompilerParams(dimension_semantics=("arbitrary",) * n_axes,
                                vmem_limit_bytes=VMEM_LIMIT)


def _const_spec(shape):
    nd = len(shape)
    return pl.BlockSpec(shape, lambda *_: (0,) * nd, pipeline_mode=pl.Buffered(1))


def _sigmoid(x):
    return 1.0 / (1.0 + jnp.exp(-x))


def _silu(x):
    return x * _sigmoid(x)


def _adanorm(x, g, shift, scale):
    ms = jnp.mean(x * x, axis=-1, keepdims=True)
    return x * lax.rsqrt(ms + EPS) * (g * (1.0 + scale)) + shift


def _mod_row(tile):
    return jnp.where(tile < NT_P, 0, 1 + (tile - NT_P) // TILES_PER_DEC_SEQ)


def _dot(a, b):
    return jnp.dot(a, b, preferred_element_type=F32)


def _dot_nt(a, b):
    return lax.dot_general(a, b, (((1,), (1,)), ((), ())), preferred_element_type=F32)


def _dot_tn(a, b):
    return lax.dot_general(a, b, (((0,), (0,)), ((), ())), preferred_element_type=F32)


def _split2(x):
    hi = x.astype(BF16)
    return hi, (x - hi.astype(F32)).astype(BF16)


def _interleave(generators, skew=0):
    results = [None] * len(generators)
    active = list(range(len(generators)))
    rnd = 0
    while active:
        for i in list(active):
            if rnd < i * skew:
                continue
            try:
                next(generators[i])
            except StopIteration as stop:
                results[i] = stop.value
                active.remove(i)
        rnd += 1
    return results


class _WeightStream:
    def __init__(self, jobs, stage_ref, sem_ref):
        self.jobs, self.stage, self.sem = jobs, stage_ref, sem_ref
        self.depth = stage_ref.shape[0]

    def _copy(self, k):
        slot = k % self.depth
        return pltpu.make_async_copy(self.jobs[k][0], self.stage.at[slot], self.sem.at[slot])

    def prime(self):
        for k in range(min(self.depth, len(self.jobs))):
            self._copy(k).start()

    def take(self, k):
        self._copy(k).wait()
        self.jobs[k][1][...] = self.stage[k % self.depth].astype(BF16)
        if k + self.depth < len(self.jobs):
            self._copy(k + self.depth).start()


def _stream_scratch(piece_shape, depth):
    return [pltpu.VMEM((depth,) + tuple(piece_shape), F32), pltpu.SemaphoreType.DMA((depth,))]


def _column_jobs(w_hbm, index, w_bf, width):
    return [(w_hbm.at[index, :, pl.ds(c, width)], w_bf.at[:, pl.ds(c, width)])
            for c in range(0, w_bf.shape[1], width)]


def _first_step_or_not(body):
    first = pl.program_id(0) == 0
    pl.when(first)(lambda: body(True))
    pl.when(jnp.logical_not(first))(lambda: body(False))


HBM_SPEC = pl.BlockSpec(memory_space=pl.ANY)
W_PIECE = 512
W_DEPTH = 3


MOD_TN = 1536


def _mod_kernel(cond_ref, w_ref, b_ref, o_ref):
    s = _silu(cond_ref[...])
    s_hi = s.astype(BF16)
    s_lo = (s - s_hi.astype(F32)).astype(BF16)
    w = w_ref[0]
    w_hi = w.astype(BF16)
    w_lo = (w - w_hi.astype(F32)).astype(BF16)
    o_ref[0] = _dot(s_hi, w_hi) + _dot(s_lo, w_hi) + _dot(s_hi, w_lo) + b_ref[0]


def _modulation(cond, w_ada, b_ada):
    return pl.pallas_call(
        _mod_kernel,
        grid=(DEPTH, N_MOD * D // MOD_TN),
        in_specs=[
            pl.BlockSpec((MOD_ROWS, D), lambda l, n: (0, 0)),
            pl.BlockSpec((1, D, MOD_TN), lambda l, n: (l, 0, n)),
            pl.BlockSpec((1, 1, MOD_TN), lambda l, n: (l, 0, n)),
        ],
        out_specs=pl.BlockSpec((1, MOD_ROWS, MOD_TN), lambda l, n: (l, 0, n)),
        out_shape=jax.ShapeDtypeStruct((DEPTH, MOD_ROWS, N_MOD * D), F32),
        compiler_params=_cparams(2),
        name="modulation",
    )(cond, w_ada, b_ada.reshape(DEPTH, 1, N_MOD * D))


def _rope_tables():
    t = np.arange(DEC_SEQ)
    row, col = t // GRID_W, t % GRID_W
    lane = np.arange(HD) % DH
    use_row = lane < DH // 2
    idx = (lane % (DH // 2)) % (DH // 4)
    inv = ROPE_THETA ** (-idx.astype(np.float64) / (DH // 4))
    pos = np.where(use_row[None, :], row[:, None], col[:, None]).astype(np.float64)
    ang = pos * inv[None, :]
    first = (lane % (DH // 2)) < DH // 4
    cos = np.cos(ang)
    sin = np.sin(ang)
    sin_minus = np.where(first[None, :], -sin, 0.0)
    sin_plus = np.where(first[None, :], 0.0, sin)
    return (jnp.asarray(cos, F32), jnp.asarray(sin_minus, F32), jnp.asarray(sin_plus, F32))


def _rope_tile(t, cos, sin_minus, sin_plus):
    return (t * cos + pltpu.roll(t, HD - DH // 4, 1) * sin_minus
            + pltpu.roll(t, DH // 4, 1) * sin_plus)


def _lambda(lam_ref, lam_init):
    lp = lam_ref[...]
    return (jnp.exp(jnp.sum(lp[0:1] * lp[1:2], axis=-1, keepdims=True))
            - jnp.exp(jnp.sum(lp[2:3] * lp[3:4], axis=-1, keepdims=True)) + lam_init)


def _with_ones(v):
    return jnp.concatenate([v, jnp.ones_like(v)], axis=1)


def _diff_attn_head(q, score_fns, values, lam, g_scaled):
    tq = q.shape[0]
    lane = lax.broadcasted_iota(jnp.int32, q.shape, 1)
    zero = jnp.zeros_like(q)
    qq = jnp.concatenate([jnp.where(lane < DH, q, zero), jnp.where(lane >= DH, q, zero)], axis=0)
    scores = [fn(qq) for fn in score_fns]
    yield
    mx = scores[0].max(axis=-1, keepdims=True)
    for s in scores[1:]:
        mx = jnp.maximum(mx, s.max(axis=-1, keepdims=True))
    yield
    acc = None
    for s, v in zip(scores, values):
        c = _dot(jnp.exp2(s - mx).astype(BF16), v)
        acc = c if acc is None else acc + c
        yield
    r = acc[:, 0:HD] / acc[:, HD:2 * HD]
    o = r[:tq] - lam * r[tq:]
    o = o * lax.rsqrt(jnp.mean(o * o, axis=-1, keepdims=True) + EPS) * g_scaled
    return o.astype(BF16)


def _attn_prompt_kernel(*refs, j, n_layers, lam_init):
    x_ref, mod_ref, g_ref, w_hbm, lam_ref, gsub_ref = refs[:6]
    kt_ref, v_ref, o_ref, w_bf, stage, sem = refs[-6:]
    owns_all = j == 0
    stream = _WeightStream(_column_jobs(w_hbm, j, w_bf, W_PIECE), stage, sem)
    per_mat = D // W_PIECE

    def body(first):
        if first:
            stream.prime()
        m = mod_ref[pl.ds(0, 1), :]
        h = _adanorm(x_ref[...], g_ref[...], m[:, 0:D], m[:, D:2 * D]).astype(BF16)
        lam = _lambda(lam_ref, lam_init)
        g_scaled = gsub_ref[...] * (1.0 - lam_init)
        mats = []
        for n in range(3):
            if first:
                for p in range(per_mat):
                    stream.take(n * per_mat + p)
            mats.append(_dot(h, w_bf[:, n * D:(n + 1) * D]))
        q = (mats[0] * (DH ** -0.5 * LOG2E)).astype(BF16)
        k, v = mats[1], mats[2]
        kb, vb = k.astype(BF16), v.astype(BF16)
        for s in range(SEQ_PER_TILE):
            rows = slice(s * SEQ, (s + 1) * SEQ)
            if owns_all:
                for jj in range(n_layers):
                    kt_ref[s, jj] = k[rows].T if jj == j else jnp.zeros((D, SEQ), F32)
                    v_ref[s, jj] = v[rows] if jj == j else jnp.zeros((SEQ, D), F32)
            else:
                kt_ref[s] = k[rows].T
                v_ref[s] = v[rows]
            for h0 in range(0, HEADS, ATT_P_WIDTH):
                heads = []
                for hh in range(h0, h0 + ATT_P_WIDTH):
                    cols = slice(hh * HD, (hh + 1) * HD)
                    kh = kb[rows, cols]
                    heads.append(_diff_attn_head(q[rows, cols], [lambda qq, kh=kh: _dot_nt(qq, kh)],
                                                 [_with_ones(vb[rows, cols])], lam, g_scaled))
                for hh, o in zip(range(h0, h0 + ATT_P_WIDTH), _interleave(heads, ATT_SKEW)):
                    o_ref[rows, hh * HD:(hh + 1) * HD] = o

    _first_step_or_not(body)


def _attn_prompt(x, mod_l, g, w_qkv, j, lam_p, g_sub, lam_init, prev=None):
    n_layers = w_qkv.shape[0]
    assert (prev is None) == (j == 0)
    tile = pl.BlockSpec((TM, D), lambda i: (i, 0))
    in_specs = [tile, _const_spec((MOD_ROWS, N_MOD * D)), _const_spec((1, D)), HBM_SPEC,
                _const_spec((4, DH)), _const_spec((1, HD))]
    args = [x, mod_l, g, w_qkv, lam_p, g_sub]
    if prev is None:
        kt_spec = pl.BlockSpec((SEQ_PER_TILE, n_layers, D, SEQ), lambda i: (i, 0, 0, 0))
        v_spec = pl.BlockSpec((SEQ_PER_TILE, n_layers, SEQ, D), lambda i: (i, 0, 0, 0))
        aliases = {}
    else:
        kt_spec = pl.BlockSpec((SEQ_PER_TILE, None, D, SEQ), lambda i: (i, j, 0, 0))
        v_spec = pl.BlockSpec((SEQ_PER_TILE, None, SEQ, D), lambda i: (i, j, 0, 0))
        aliases = {len(args): 0, len(args) + 1: 1}
        in_specs += [HBM_SPEC, HBM_SPEC]
        args += list(prev)
    kern = lambda *refs: _attn_prompt_kernel(*refs, j=j, n_layers=n_layers, lam_init=lam_init)
    return pl.pallas_call(
        kern,
        grid=(NT_P,),
        in_specs=in_specs,
        out_specs=[kt_spec, v_spec, tile],
        out_shape=[jax.ShapeDtypeStruct((BATCH, n_layers, D, SEQ), F32),
                   jax.ShapeDtypeStruct((BATCH, n_layers, SEQ, D), F32), jax.ShapeDtypeStruct((N_P, D), BF16)],
        input_output_aliases=aliases,
        scratch_shapes=[pltpu.VMEM((D, 3 * D), BF16)] + _stream_scratch((D, W_PIECE), W_DEPTH),
        compiler_params=_cparams(1),
        name="attn_prompt",
    )(*args)


def _qkv_sample_kernel(x_ref, mod_ref, g_ref, w_hbm, cos_ref, sm_ref, sp_ref, q_ref, k_ref, v_ref,
                       w_bf, stage, sem, *, j):
    stream = _WeightStream(_column_jobs(w_hbm, j, w_bf, W_PIECE), stage, sem)
    per_mat = D // W_PIECE

    def body(first):
        if first:
            stream.prime()
        i = pl.program_id(0)
        m = mod_ref[pl.ds(1 + i // TILES_PER_DEC_SEQ, 1), :]
        h = _adanorm(x_ref[...], g_ref[...], m[:, 0:D], m[:, D:2 * D]).astype(BF16)
        cos, sm, sp = cos_ref[...], sm_ref[...], sp_ref[...]
        for n, (o_ref, scale) in enumerate(((q_ref, DH ** -0.5 * LOG2E), (k_ref, None), (v_ref, None))):
            if first:
                for p in range(per_mat):
                    stream.take(n * per_mat + p)
            y = _dot(h, w_bf[:, n * D:(n + 1) * D])
            if o_ref is v_ref:
                o_ref[...] = y.astype(BF16)
                continue
            if scale is not None:
                y = y * scale
            for hh in range(HEADS):
                t = _rope_tile(y[:, hh * HD:(hh + 1) * HD], cos, sm, sp)
                o_ref[:, hh * HD:(hh + 1) * HD] = t.astype(BF16)

    _first_step_or_not(body)


def _qkv_sample(x, tile0, mod_l, g, w_qkv, j, tabs):
    tile = pl.BlockSpec((TM, D), lambda i: (i, 0))
    tab = pl.BlockSpec((TM, HD), lambda i: (i % TILES_PER_DEC_SEQ, 0))
    kern = lambda *refs: _qkv_sample_kernel(*refs, j=j)
    return pl.pallas_call(
        kern,
        grid=(NT_S,),
        in_specs=[pl.BlockSpec((TM, D), lambda i: (i + tile0, 0)), _const_spec((MOD_ROWS, N_MOD * D)),
                  _const_spec((1, D)), HBM_SPEC, tab, tab, tab],
        out_specs=[tile, tile, tile],
        out_shape=[jax.ShapeDtypeStruct((N_S, D), BF16)] * 3,
        scratch_shapes=[pltpu.VMEM((D, 3 * D), BF16)] + _stream_scratch((D, W_PIECE), W_DEPTH),
        compiler_params=_cparams(1),
        name="qkv_sample",
    )(x, mod_l, g, w_qkv, *tabs)


def _attn_sample_kernel(*refs, lam_init):
    q_ref, kn_ref, vn_ref, kct_ref, vc_ref, lam_ref, gsub_ref, o_ref = refs
    lam = _lambda(lam_ref, lam_init)
    g_scaled = gsub_ref[...] * (1.0 - lam_init)
    for h0 in range(0, HEADS, ATT_HP):
        heads = []
        for hh in range(h0, h0 + ATT_HP):
            cols = slice(hh * HD, (hh + 1) * HD)
            kn = kn_ref[:, cols]
            kct = kct_ref[hh].astype(BF16)
            vc = vc_ref[:, cols].astype(BF16)
            heads.append(_diff_attn_head(
                q_ref[:, cols], [lambda qq, kn=kn: _dot_nt(qq, kn), lambda qq, kct=kct: _dot(qq, kct)],
                [_with_ones(vn_ref[:, cols]), _with_ones(vc)], lam, g_scaled))
        for hh, o in zip(range(h0, h0 + ATT_HP), _interleave(heads, ATT_SKEW)):
            o_ref[:, hh * HD:(hh + 1) * HD] = o


def _attn_sample(q, kn, vn, cache_kt, cache_v, j, lam_p, g_sub, lam_init):
    nq = DEC_SEQ // ATT_TQ
    seq = pl.BlockSpec((DEC_SEQ, D), lambda b, i: (b, 0))
    blk = pl.BlockSpec((ATT_TQ, D), lambda b, i: (b * nq + i, 0))
    in_specs = [blk, seq, seq,
                pl.BlockSpec((None, None, HEADS, HD, PAST), lambda b, i: (b, j, 0, 0, 0)),
                pl.BlockSpec((None, None, PAST, D), lambda b, i: (b, j, 0, 0)),
                pl.BlockSpec((4, DH), lambda b, i: (0, 0)), pl.BlockSpec((1, HD), lambda b, i: (0, 0))]
    kern = lambda *refs: _attn_sample_kernel(*refs, lam_init=lam_init)
    return pl.pallas_call(
        kern,
        grid=(DEC_BATCH, nq),
        in_specs=in_specs,
        out_specs=blk,
        out_shape=jax.ShapeDtypeStruct((N_S, D), BF16),
        compiler_params=_cparams(2),
        name="attn_sample",
    )(q, kn, vn, cache_kt, cache_v, lam_p, g_sub)


def _rec_proj_kernel(x_ref, mod_ref, g_ref, w_hbm, q_ref, v_ref, gate_ref, zf_ref, zb_ref,
                     w_bf, stage, sem, *, j):
    stream = _WeightStream(_column_jobs(w_hbm, j, w_bf, W_PIECE), stage, sem)
    per_mat = D // W_PIECE

    def body(first):
        if first:
            stream.prime()
        m = mod_ref[pl.ds(_mod_row(pl.program_id(0)), 1), :]
        h = _adanorm(x_ref[...], g_ref[...], m[:, 0:D], m[:, D:2 * D]).astype(BF16)
        for n, o_ref in enumerate((q_ref, v_ref, gate_ref, zf_ref, zb_ref)):
            if first:
                for p in range(per_mat):
                    stream.take(n * per_mat + p)
            y = _dot(h, w_bf[:, n * D:(n + 1) * D])
            if o_ref is q_ref or o_ref is gate_ref:
                y = _silu(y)
            o_ref[...] = y.astype(o_ref.dtype)

    _first_step_or_not(body)


def _rec_proj(x, mod_l, g, w_in, j):
    tile = pl.BlockSpec((TM, D), lambda i: (i, 0))
    kern = lambda *refs: _rec_proj_kernel(*refs, j=j)
    return pl.pallas_call(
        kern,
        grid=(N_TOK // TM,),
        in_specs=[tile, _const_spec((MOD_ROWS, N_MOD * D)), _const_spec((1, D)), HBM_SPEC],
        out_specs=[tile] * 5,
        out_shape=[jax.ShapeDtypeStruct((N_TOK, D), BF16)] * 3 + [jax.ShapeDtypeStruct((N_TOK, D), F32)] * 2,
        scratch_shapes=[pltpu.VMEM((D, 5 * D), BF16)] + _stream_scratch((D, W_PIECE), W_DEPTH),
        compiler_params=_cparams(1),
        name="rec_proj",
    )(x, mod_l, g, w_in)


def _gla_tables():
    t = np.arange(GLA_GROUP)[:, None]
    s = np.arange(GLA_GROUP)[None, :]
    ids = np.full((GLA_GROUP, GLA_GROUP), -1, np.int32)
    for l, b in reversed(list(enumerate(GLA_LEVELS, 1))):
        ids = np.where((t // (2 * b) == s // (2 * b)) & (s < t), l, ids)
    ids = np.where((t // GLA_BLK == s // GLA_BLK) & (s <= t), 0, ids)
    ids = np.stack([ids, ids.T]).astype(np.int32)
    e = np.zeros((GLA_BLK * HD, GLA_GROUP), np.float32)
    for j in range(GLA_BLK):
        e[j * HD:(j + 1) * HD, j::GLA_BLK] = 1.0
    return jnp.asarray(ids), jnp.asarray(e)


def _bcast_row(x, blk, idx):
    n, w = x.shape
    r = x.reshape(n // blk, blk, w)[:, idx:idx + 1, :]
    return jnp.broadcast_to(r, (n // blk, blk, w)).reshape(n, w)


def _gla_group(q, v, z, lb, ids, e_mat, st, backward):
    n = GLA_GROUP
    sig = _sigmoid(z)
    f = lb + (1.0 - lb) * sig
    k = (1.0 - lb) * (1.0 - sig)
    lf = jnp.log(f) * LOG2E
    row = lax.broadcasted_iota(jnp.int32, (n, n), 0)
    col = lax.broadcasted_iota(jnp.int32, (n, n), 1)
    tri = jnp.where((col >= row) if backward else (col <= row), 1.0, 0.0).astype(BF16)
    c2 = _dot(tri, jnp.concatenate(_split2(lf), axis=1))
    cum = c2[:, 0:HD] + c2[:, HD:2 * HD]
    ck = jnp.log(k) * LOG2E - cum
    yield

    prods = []
    for j in range(GLA_BLK):
        prods.append((q * jnp.exp2(jnp.minimum(cum + _bcast_row(ck, GLA_BLK, j), 0.0))).astype(BF16))
        yield
    half = n // 2
    ids_q = ids[0:half, 0:half]
    diag = lambda x, h: x[h * half:(h + 1) * half, h * half:(h + 1) * half]
    pair = _dot(jnp.concatenate(prods, axis=1), e_mat)
    quads = [jnp.where(ids_q == 0, diag(pair, h), 0.0) for h in range(2)]
    yield

    for l, b in enumerate(GLA_LEVELS[:-1], 1):
        edge = _bcast_row(cum, 2 * b, b if backward else b - 1)
        ql = (q * jnp.exp2(cum - edge)).astype(BF16)
        kl = jnp.exp2(edge + ck).astype(BF16)
        x = _dot_nt(ql, kl)
        quads = [jnp.where(ids_q == l, diag(x, h), quads[h]) for h in range(2)]
        yield

    assert GLA_LEVELS[-1] == half
    lo, hi = slice(0, half), slice(half, n)
    qs, ks = (lo, hi) if backward else (hi, lo)
    edge = cum[half:half + 1, :] if backward else cum[half - 1:half, :]
    cross = _dot_nt((q[qs] * jnp.exp2(cum[qs] - edge)).astype(BF16),
                    jnp.exp2(edge + ck[ks]).astype(BF16)).astype(BF16)
    o_halves = [_dot(quads[h].astype(BF16), v[h * half:(h + 1) * half]) for h in range(2)]
    o_halves[0 if backward else 1] += _dot(cross, v[ks])
    o = jnp.concatenate(o_halves, axis=0)
    yield

    total = cum[0:1, :] if backward else cum[n - 1:n, :]
    if st is not None:
        o = o + _dot_nt((q * jnp.exp2(cum)).astype(BF16), st.astype(BF16))
    kdec = jnp.exp2(total + ck).astype(BF16)
    upd = _dot_tn(v, kdec)
    st_new = upd if st is None else st * jnp.exp2(total) + upd
    return o, st_new


def _lower_bound(x, layer):
    rows = [x[l:l + 1, :] for l in range(DEPTH)]
    mx = rows[0]
    for r in rows[1:]:
        mx = jnp.maximum(mx, r)
    ex = [jnp.exp(r - mx) for r in rows]
    tot = ex[0]
    for e in ex[1:]:
        tot = tot + e
    acc = ex[0] * 0.0
    for l in range(1, layer + 1):
        acc = acc + ex[l]
    return acc / tot


def _gla_kernel(*refs, n_groups, layer, has_state):
    if has_state:
        (q_ref, v_ref, gate_ref, zf_ref, zb_ref, lbl_ref, g_ref, ids_ref, e_ref, s0_ref,
         o_ref, acc_ref) = refs
    else:
        (q_ref, v_ref, gate_ref, zf_ref, zb_ref, lbl_ref, g_ref, ids_ref, e_ref,
         o_ref, sout_ref, acc_ref) = refs
    e_mat = e_ref[...]
    z_refs = (zf_ref, zb_ref)
    chains = [(hh, d) for hh in range(GLA_HP) for d in range(2)]
    cols = [slice(hh * HD, (hh + 1) * HD) for hh in range(GLA_HP)]
    lbs = {(hh, d): _lower_bound(lbl_ref[d, :, cols[hh]], layer) for hh, d in chains}

    def step(gi, sts):
        rows, gens = [], []
        for d in range(2):
            g = gi if d == 0 else n_groups - 1 - gi
            start = g * GLA_GROUP
            rows.append(pl.ds(start if isinstance(start, int) else pl.multiple_of(start, GLA_GROUP), GLA_GROUP))
        for c, (hh, d) in enumerate(chains):
            gens.append(_gla_group(q_ref[rows[d], cols[hh]].astype(F32), v_ref[rows[d], cols[hh]],
                                   z_refs[d][rows[d], cols[hh]], lbs[hh, d], ids_ref[d], e_mat, sts[c],
                                   backward=(d == 1)))
        new = []
        for (hh, d), (o, st) in zip(chains, _interleave(gens)):
            acc_ref[rows[d], cols[hh]] = acc_ref[rows[d], cols[hh]] + o
            new.append(st)
        return tuple(new)

    acc_ref[...] = jnp.zeros_like(acc_ref)
    if has_state:
        init = tuple(s0_ref[d, hh].T for hh, d in chains)
        lax.fori_loop(0, n_groups, step, init)
    else:
        sts = step(0, (None,) * len(chains))
        for (hh, d), st in zip(chains, sts):
            sout_ref[d, hh] = st.T
    for hh in range(GLA_HP):
        o = acc_ref[:, cols[hh]]
        y = o * lax.rsqrt(jnp.mean(o * o, axis=-1, keepdims=True) + EPS) * g_ref[...]
        o_ref[:, cols[hh]] = (y * gate_ref[:, cols[hh]].astype(F32)).astype(BF16)


def _gla(q, v, gate, zf, zb, lb_logits, g_out, tabs, *, row0, nb, ln, layer, s0=None):
    ids, e_mat = tabs
    assert ln % GLA_GROUP == 0 and (s0 is not None or ln == GLA_GROUP)
    b0 = row0 // ln
    w = GLA_HP * HD
    seq = pl.BlockSpec((ln, w), lambda b, h: (b + b0, h))
    in_specs = [seq] * 5 + [
        pl.BlockSpec((2, DEPTH, w), lambda b, h: (0, 0, h)),
        pl.BlockSpec((1, HD), lambda b, h: (0, 0)),
        pl.BlockSpec((2, GLA_GROUP, GLA_GROUP), lambda b, h: (0, 0, 0)),
        pl.BlockSpec((GLA_BLK * HD, GLA_GROUP), lambda b, h: (0, 0)),
    ]
    args = [q, v, gate, zf, zb, lb_logits, g_out, ids, e_mat]
    out_specs = [pl.BlockSpec((ln, w), lambda b, h: (b, h))]
    out_shape = [jax.ShapeDtypeStruct((nb * ln, D), BF16)]
    state_spec = pl.BlockSpec((None, 2, GLA_HP, HD, HD), lambda b, h: (b, 0, h, 0, 0))
    if s0 is not None:
        in_specs.append(state_spec)
        args.append(s0)
    else:
        out_specs.append(state_spec)
        out_shape.append(jax.ShapeDtypeStruct((nb, 2, HEADS, HD, HD), F32))
    kern = lambda *refs: _gla_kernel(*refs, n_groups=ln // GLA_GROUP, layer=layer, has_state=s0 is not None)
    return pl.pallas_call(
        kern,
        grid=(nb, HEADS // GLA_HP),
        in_specs=in_specs,
        out_specs=out_specs,
        out_shape=out_shape,
        scratch_shapes=[pltpu.VMEM((ln, w), F32)],
        compiler_params=_cparams(2),
        name="gla_sample" if s0 is not None else "gla_prompt",
    )(*args)


def _dft_tables(n):
    c = np.arange(FOUR_DG)
    ang_c = 2.0 * np.pi * ((c[:, None] * c[None, :]) % FOUR_DG) / FOUR_DG
    cs = np.concatenate([np.cos(ang_c), np.sin(ang_c)], axis=1)
    t = np.arange(n)
    ang_n = 2.0 * np.pi * ((t[:, None] * t[None, :]) % n) / n
    cn = np.concatenate([np.cos(ang_n), -np.sin(ang_n)], axis=1) / math.sqrt(n * FOUR_DG)
    return jnp.asarray(cs, F32).astype(BF16), jnp.asarray(cn, F32).astype(BF16)


def _fourier_kernel(x_ref, mod_ref, g_ref, cs_ref, cn_ref, o_ref, *, row_of):
    m = mod_ref[pl.ds(row_of(pl.program_id(0)), 1), :]
    h = _adanorm(x_ref[...], g_ref[...], m[:, 0:D], m[:, D:2 * D]).astype(BF16)
    cs = cs_ref[...]
    cn = cn_ref[...]
    for g in range(FOUR_GROUPS):
        xcs = _dot(h[:, g * FOUR_DG:(g + 1) * FOUR_DG], cs)
        stacked = jnp.concatenate([xcs[:, 0:FOUR_DG], xcs[:, FOUR_DG:]], axis=0).astype(BF16)
        o_ref[:, g * FOUR_DG:(g + 1) * FOUR_DG] = _dot(cn, stacked).astype(BF16)


def _fourier(x, mod_l, g, *, row0, nb, ln, row_of):
    cs, cn = _dft_tables(ln)
    b0 = row0 // ln
    kern = lambda *refs: _fourier_kernel(*refs, row_of=row_of)
    return pl.pallas_call(
        kern,
        grid=(nb,),
        in_specs=[pl.BlockSpec((ln, D), lambda b: (b + b0, 0)), _const_spec((MOD_ROWS, N_MOD * D)),
                  _const_spec((1, D)), _const_spec((FOUR_DG, 2 * FOUR_DG)), _const_spec((ln, 2 * ln))],
        out_specs=pl.BlockSpec((ln, D), lambda b: (b, 0)),
        out_shape=jax.ShapeDtypeStruct((nb * ln, D), BF16),
        compiler_params=_cparams(1),
        name="fourier_%d" % ln,
    )(x, mod_l, g, cs, cn)


FF_DEPTH_COLS = 4
FF_DEPTH_ROWS = 2


def _post_ffn_kernel(*refs, layer, wo_index, split_x, final):
    refs = list(refs)
    x_refs = [refs.pop(0) for _ in range(2 if split_x else 1)]
    op_ref, os_ref, mod_ref, wo_hbm, g_ref, win_hbm, wout_hbm = refs[:7]
    refs = refs[7:]
    gf_ref = refs.pop(0) if final else None
    out_refs = [refs.pop(0) for _ in range(2 if final else 1)]
    wo_bf, win_bf, wout_bf, stage_c, sem_c, stage_r, sem_r = refs
    n_chunks = D_FF // FF_CHUNK
    wo_jobs = _column_jobs(wo_hbm, wo_index, wo_bf, FF_CHUNK)
    in_jobs = []
    for c in range(n_chunks):
        for c0 in (c * FF_CHUNK, D_FF + c * FF_CHUNK):
            in_jobs.append((win_hbm.at[layer, :, pl.ds(c0, FF_CHUNK)], win_bf.at[:, pl.ds(c0, FF_CHUNK)]))
    cols = _WeightStream(wo_jobs + in_jobs, stage_c, sem_c)
    rows = _WeightStream([(wout_hbm.at[layer, pl.ds(c * FF_CHUNK, FF_CHUNK), :],
                           wout_bf.at[pl.ds(c * FF_CHUNK, FF_CHUNK), :]) for c in range(n_chunks)],
                         stage_r, sem_r)

    def body(first):
        if first:
            cols.prime()
            rows.prime()
        i = pl.program_id(0)
        ctx = i < NT_P
        m = mod_ref[pl.ds(_mod_row(i), 1), :]
        x = jnp.where(ctx, x_refs[0][...], x_refs[1][...]) if split_x else x_refs[0][...]
        o = jnp.where(ctx, op_ref[...], os_ref[...])
        if first:
            for k in range(len(wo_jobs)):
                cols.take(k)
        x = x + m[:, 2 * D:3 * D] * _dot(o, wo_bf[...])
        h = _adanorm(x, g_ref[...], m[:, 3 * D:4 * D], m[:, 4 * D:5 * D]).astype(BF16)
        acc = None
        for c in range(n_chunks):
            if first:
                cols.take(len(wo_jobs) + 2 * c)
                cols.take(len(wo_jobs) + 2 * c + 1)
                rows.take(c)
            gt = _dot(h, win_bf[:, c * FF_CHUNK:(c + 1) * FF_CHUNK])
            up = _dot(h, win_bf[:, D_FF + c * FF_CHUNK:D_FF + (c + 1) * FF_CHUNK])
            part = _dot((_silu(gt) * up).astype(BF16), wout_bf[c * FF_CHUNK:(c + 1) * FF_CHUNK, :])
            acc = part if acc is None else acc + part
        x = x + m[:, 5 * D:6 * D] * acc
        if not final:
            out_refs[0][...] = x
            return
        y = x * lax.rsqrt(jnp.mean(x * x, axis=-1, keepdims=True) + EPS) * gf_ref[...]
        if first:
            out_refs[0][...] = y
            return

        @pl.when(ctx)
        def _():
            out_refs[0][...] = y

        @pl.when(jnp.logical_not(ctx))
        def _():
            out_refs[1][...] = y

    _first_step_or_not(body)


def _post_ffn(xs, op, os_, mod_l, w_o, wo_index, g, w_in, w_out, layer, g_final=None):
    tile = pl.BlockSpec((TM, D), lambda i: (i, 0))
    p_tile = pl.BlockSpec((TM, D), lambda i: (jnp.minimum(i, NT_P - 1), 0))
    s_tile = pl.BlockSpec((TM, D), lambda i: (jnp.maximum(i - NT_P, 0), 0))
    split_x = len(xs) == 2
    final = g_final is not None
    in_specs = ([p_tile, s_tile] if split_x else [tile]) + [
        p_tile, s_tile, _const_spec((MOD_ROWS, N_MOD * D)), HBM_SPEC, _const_spec((1, D)), HBM_SPEC, HBM_SPEC]
    args = list(xs) + [op, os_, mod_l, w_o, g, w_in, w_out]
    if final:
        in_specs.append(_const_spec((1, D)))
        args.append(g_final)
        out_specs = [p_tile, s_tile]
        out_shape = [jax.ShapeDtypeStruct((N_P, D), F32), jax.ShapeDtypeStruct((N_S, D), F32)]
    else:
        out_specs = tile
        out_shape = jax.ShapeDtypeStruct((N_TOK, D), F32)
    kern = lambda *refs: _post_ffn_kernel(*refs, layer=layer, wo_index=wo_index, split_x=split_x, final=final)
    return pl.pallas_call(
        kern,
        grid=(N_TOK // TM,),
        in_specs=in_specs,
        out_specs=out_specs,
        out_shape=out_shape,
        scratch_shapes=([pltpu.VMEM((D, D), BF16), pltpu.VMEM((D, 2 * D_FF), BF16), pltpu.VMEM((D_FF, D), BF16)]
                        + _stream_scratch((D, FF_CHUNK), FF_DEPTH_COLS)
                        + _stream_scratch((FF_CHUNK, D), FF_DEPTH_ROWS)),
        compiler_params=_cparams(1),
        name="post_ffn",
    )(*args)


def kernel(x_prompt, x_sample, cache_attn_k, cache_attn_v, state_hgrn, c, c_ctx, w_ada, b_ada, g_norm_mix,
           g_norm_ffn, w_qkv_attn, lam_attn, g_subln_attn, w_o_attn, w_in_rec, lb_logits_rec, g_out_rec,
           w_o_rec, w_four, w_ffn_in, w_ffn_out, g_final):
    xs = (x_prompt.reshape(N_P, D), x_sample.reshape(N_S, D))
    cond = jnp.concatenate([c_ctx.reshape(1, D), c, jnp.zeros((MOD_ROWS - 1 - DEC_BATCH, D), F32)], axis=0)
    mod = _modulation(cond, w_ada, b_ada)
    cache_kt = jnp.transpose(cache_attn_k, (0, 1, 3, 4, 5, 2)).reshape(DEC_BATCH, -1, HEADS, HD, PAST)
    cache_v = cache_attn_v.reshape(DEC_BATCH, -1, PAST, D)
    rope_tabs = _rope_tables()
    gla_tabs = _gla_tables()
    gla_tabs = (gla_tabs[0], gla_tabs[1].astype(BF16))
    kv_ctx, new_s = None, []
    for i in range(DEPTH):
        kind, j = i % N_MIXERS, i // N_MIXERS
        g_mix = g_norm_mix[i].reshape(1, D)
        x_p = xs[0]
        x_s, s_tile0 = (xs[1], 0) if len(xs) == 2 else (xs[0], NT_P)
        if kind == 0:
            lam_init = 0.8 - 0.6 * math.exp(-0.3 * i)
            g_sub = g_subln_attn[j].reshape(1, HD)
            kt_all, v_all, op = _attn_prompt(x_p, mod[i], g_mix, w_qkv_attn, j, lam_attn[j], g_sub, lam_init,
                                             prev=kv_ctx)
            kv_ctx = (kt_all, v_all)
            qs, ks, vs = _qkv_sample(x_s, s_tile0, mod[i], g_mix, w_qkv_attn, j, rope_tabs)
            os_ = _attn_sample(qs, ks, vs, cache_kt, cache_v, j, lam_attn[j], g_sub, lam_init)
            w_o = w_o_attn
        elif kind == 1:
            proj = _rec_proj(xs[0], mod[i], g_mix, w_in_rec, j)
            g_out = g_out_rec[j].reshape(1, HD)
            op, s_ctx = _gla(*proj, lb_logits_rec, g_out, gla_tabs, row0=0, nb=BATCH, ln=SEQ, layer=i)
            new_s.append(s_ctx)
            os_ = _gla(*proj, lb_logits_rec, g_out, gla_tabs, row0=N_P, nb=DEC_BATCH, ln=DEC_SEQ, layer=i,
                       s0=state_hgrn[:, j])[0]
            w_o = w_o_rec
        else:
            op = _fourier(xs[0], mod[i], g_mix, row0=0, nb=BATCH, ln=SEQ, row_of=lambda b: 0)
            os_ = _fourier(xs[0], mod[i], g_mix, row0=N_P, nb=DEC_BATCH, ln=DEC_SEQ, row_of=lambda b: 1 + b)
            w_o = w_four
        out = _post_ffn(xs, op, os_, mod[i], w_o, j, g_norm_ffn[i].reshape(1, D), w_ffn_in, w_ffn_out, i,
                        g_final.reshape(1, D) if i == DEPTH - 1 else None)
        xs = tuple(out) if i == DEPTH - 1 else (out,)
    y_prompt = xs[0].reshape(BATCH, SEQ, D)
    y_sample = xs[1].reshape(DEC_BATCH, DEC_SEQ, D)
    new_k = jnp.transpose(kv_ctx[0].reshape(BATCH, -1, HEADS, 2, DH, SEQ), (0, 1, 5, 2, 3, 4))
    new_v = kv_ctx[1].reshape(BATCH, -1, SEQ, HEADS, HD)
    return (y_prompt, y_sample, new_k, new_v, jnp.stack(new_s, axis=1))
```

```python
import math

import numpy as np
import jax
import jax.numpy as jnp
from jax import lax
from jax.experimental import pallas as pl
from jax.experimental.pallas import tpu as pltpu

F32 = jnp.float32
BF16 = jnp.bfloat16

D = 1024
BATCH = 16
SEQ = 256
DEPTH = 4
DEC_BATCH = 4
DEC_SEQ = 1024
PAST = 512
GRID_W = 64
N_MIXERS = 3
HEADS = 8
DH = 64
HD = 128
ROPE_THETA = 10000.0
D_FF = 2816
N_MOD = 6
EPS = 1e-6
LOG2E = 1.0 / math.log(2.0)
FOUR_GROUPS = 4
FOUR_DG = 256

N_P = BATCH * SEQ
N_S = DEC_BATCH * DEC_SEQ
N_TOK = N_P + N_S
TM = 512
NT_P = N_P // TM
NT_S = N_S // TM
TILES_PER_DEC_SEQ = DEC_SEQ // TM
SEQ_PER_TILE = TM // SEQ
MOD_ROWS = 8
FF_CHUNK = 256
ATT_TQ = 512
ATT_HP = 2
ATT_P_WIDTH = 2
ATT_SKEW = 1
GLA_HP = 4
GLA_GROUP = 256
GLA_BLK = 8
GLA_LEVELS = (8, 16, 32, 64, 128)
VMEM_LIMIT = 60 * 1024 * 1024


def _cparams(n_axes):
    return pltpu.CompilerParams(dimension_semantics=("arbitrary",) * n_axes,
                                vmem_limit_bytes=VMEM_LIMIT)


def _const_spec(shape):
    nd = len(shape)
    return pl.BlockSpec(shape, lambda *_: (0,) * nd, pipeline_mode=pl.Buffered(1))


def _sigmoid(x):
    return 1.0 / (1.0 + jnp.exp(-x))


def _silu(x):
    return x * _sigmoid(x)


def _adanorm(x, g, shift, scale):
    ms = jnp.mean(x * x, axis=-1, keepdims=True)
    return x * lax.rsqrt(ms + EPS) * (g * (1.0 + scale)) + shift


def _mod_row(tile):
    return jnp.where(tile < NT_P, 0, 1 + (tile - NT_P) // TILES_PER_DEC_SEQ)


def _dot(a, b):
    return jnp.dot(a, b, preferred_element_type=F32)


def _dot_nt(a, b):
    return lax.dot_general(a, b, (((1,), (1,)), ((), ())), preferred_element_type=F32)


def _dot_tn(a, b):
    return lax.dot_general(a, b, (((0,), (0,)), ((), ())), preferred_element_type=F32)


def _split2(x):
    hi = x.astype(BF16)
    return hi, (x - hi.astype(F32)).astype(BF16)


def _interleave(generators, skew=0):
    results = [None] * len(generators)
    active = list(range(len(generators)))
    rnd = 0
    while active:
        for i in list(active):
            if rnd < i * skew:
                continue
            try:
                next(generators[i])
            except StopIteration as stop:
                results[i] = stop.value
                active.remove(i)
        rnd += 1
    return results


class _WeightStream:
    def __init__(self, jobs, stage_ref, sem_ref):
        self.jobs, self.stage, self.sem = jobs, stage_ref, sem_ref
        self.depth = stage_ref.shape[0]

    def _copy(self, k):
        slot = k % self.depth
        return pltpu.make_async_copy(self.jobs[k][0], self.stage.at[slot], self.sem.at[slot])

    def prime(self):
        for k in range(min(self.depth, len(self.jobs))):
            self._copy(k).start()

    def take(self, k):
        self._copy(k).wait()
        self.jobs[k][1][...] = self.stage[k % self.depth].astype(BF16)
        if k + self.depth < len(self.jobs):
            self._copy(k + self.depth).start()


def _stream_scratch(piece_shape, depth):
    return [pltpu.VMEM((depth,) + tuple(piece_shape), F32), pltpu.SemaphoreType.DMA((depth,))]


def _column_jobs(w_hbm, index, w_bf, width):
    return [(w_hbm.at[index, :, pl.ds(c, width)], w_bf.at[:, pl.ds(c, width)])
            for c in range(0, w_bf.shape[1], width)]


def _first_step_or_not(body):
    first = pl.program_id(0) == 0
    pl.when(first)(lambda: body(True))
    pl.when(jnp.logical_not(first))(lambda: body(False))


HBM_SPEC = pl.BlockSpec(memory_space=pl.ANY)
W_PIECE = 512
W_DEPTH = 3


MOD_TK = 256


def _mod_kernel(cond_ref, w_ref, b_ref, o_ref):
    kk = pl.program_id(1)
    s = _silu(cond_ref[...])
    s_hi = s.astype(BF16)
    s_lo = (s - s_hi.astype(F32)).astype(BF16)
    w = w_ref[0]
    w_hi = w.astype(BF16)
    w_lo = (w - w_hi.astype(F32)).astype(BF16)
    r = _dot(jnp.concatenate([s_hi, s_lo], axis=0), w_hi)
    part = r[:MOD_ROWS] + r[MOD_ROWS:] + _dot(s_hi, w_lo)

    @pl.when(kk == 0)
    def _():
        o_ref[0] = part + b_ref[0]

    @pl.when(kk != 0)
    def _():
        o_ref[0] = o_ref[0] + part


def _modulation(cond, w_ada, b_ada):
    return pl.pallas_call(
        _mod_kernel,
        grid=(DEPTH, D // MOD_TK),
        in_specs=[
            pl.BlockSpec((MOD_ROWS, MOD_TK), lambda l, k: (0, k)),
            pl.BlockSpec((1, MOD_TK, N_MOD * D), lambda l, k: (l, k, 0)),
            pl.BlockSpec((1, 1, N_MOD * D), lambda l, k: (l, 0, 0)),
        ],
        out_specs=pl.BlockSpec((1, MOD_ROWS, N_MOD * D), lambda l, k: (l, 0, 0)),
        out_shape=jax.ShapeDtypeStruct((DEPTH, MOD_ROWS, N_MOD * D), F32),
        compiler_params=_cparams(2),
        name="modulation",
    )(cond, w_ada, b_ada.reshape(DEPTH, 1, N_MOD * D))


def _rope_tables():
    t = np.arange(DEC_SEQ)
    row, col = t // GRID_W, t % GRID_W
    lane = np.arange(HD) % DH
    use_row = lane < DH // 2
    idx = (lane % (DH // 2)) % (DH // 4)
    inv = ROPE_THETA ** (-idx.astype(np.float64) / (DH // 4))
    pos = np.where(use_row[None, :], row[:, None], col[:, None]).astype(np.float64)
    ang = pos * inv[None, :]
    first = (lane % (DH // 2)) < DH // 4
    cos = np.cos(ang)
    sin = np.sin(ang)
    sin_minus = np.where(first[None, :], -sin, 0.0)
    sin_plus = np.where(first[None, :], 0.0, sin)
    return (jnp.asarray(cos, F32), jnp.asarray(sin_minus, F32), jnp.asarray(sin_plus, F32))


def _rope_tile(t, cos, sin_minus, sin_plus):
    return (t * cos + pltpu.roll(t, HD - DH // 4, 1) * sin_minus
            + pltpu.roll(t, DH // 4, 1) * sin_plus)


def _lambda(lam_ref, lam_init):
    lp = lam_ref[...]
    return (jnp.exp(jnp.sum(lp[0:1] * lp[1:2], axis=-1, keepdims=True))
            - jnp.exp(jnp.sum(lp[2:3] * lp[3:4], axis=-1, keepdims=True)) + lam_init)


def _with_ones(v):
    return jnp.concatenate([v, jnp.ones_like(v)], axis=1)


def _diff_attn_head(q, score_fns, values, lam, g_scaled):
    tq = q.shape[0]
    lane = lax.broadcasted_iota(jnp.int32, q.shape, 1)
    zero = jnp.zeros_like(q)
    qq = jnp.concatenate([jnp.where(lane < DH, q, zero), jnp.where(lane >= DH, q, zero)], axis=0)
    scores = [fn(qq) for fn in score_fns]
    yield
    mx = scores[0].max(axis=-1, keepdims=True)
    for s in scores[1:]:
        mx = jnp.maximum(mx, s.max(axis=-1, keepdims=True))
    yield
    acc = None
    for s, v in zip(scores, values):
        c = _dot(jnp.exp2(s - mx).astype(BF16), v)
        acc = c if acc is None else acc + c
        yield
    r = acc[:, 0:HD] / acc[:, HD:2 * HD]
    o = r[:tq] - lam * r[tq:]
    o = o * lax.rsqrt(jnp.mean(o * o, axis=-1, keepdims=True) + EPS) * g_scaled
    return o.astype(BF16)


def _attn_prompt_kernel(*refs, j, n_layers, lam_init):
    x_ref, mod_ref, g_ref, w_hbm, lam_ref, gsub_ref = refs[:6]
    kt_ref, v_ref, o_ref, w_bf, stage, sem = refs[-6:]
    owns_all = j == 0
    stream = _WeightStream(_column_jobs(w_hbm, j, w_bf, W_PIECE), stage, sem)
    per_mat = D // W_PIECE

    def body(first):
        if first:
            stream.prime()
        m = mod_ref[pl.ds(0, 1), :]
        h = _adanorm(x_ref[...], g_ref[...], m[:, 0:D], m[:, D:2 * D]).astype(BF16)
        lam = _lambda(lam_ref, lam_init)
        g_scaled = gsub_ref[...] * (1.0 - lam_init)
        mats = []
        for n in range(3):
            if first:
                for p in range(per_mat):
                    stream.take(n * per_mat + p)
            mats.append(_dot(h, w_bf[:, n * D:(n + 1) * D]))
        q = (mats[0] * (DH ** -0.5 * LOG2E)).astype(BF16)
        k, v = mats[1], mats[2]
        kb, vb = k.astype(BF16), v.astype(BF16)
        for s in range(SEQ_PER_TILE):
            rows = slice(s * SEQ, (s + 1) * SEQ)
            if owns_all:
                for jj in range(n_layers):
                    kt_ref[s, jj] = k[rows].T if jj == j else jnp.zeros((D, SEQ), F32)
                    v_ref[s, jj] = v[rows] if jj == j else jnp.zeros((SEQ, D), F32)
            else:
                kt_ref[s] = k[rows].T
                v_ref[s] = v[rows]
            for h0 in range(0, HEADS, ATT_P_WIDTH):
                heads = []
                for hh in range(h0, h0 + ATT_P_WIDTH):
                    cols = slice(hh * HD, (hh + 1) * HD)
                    kh = kb[rows, cols]
                    heads.append(_diff_attn_head(q[rows, cols], [lambda qq, kh=kh: _dot_nt(qq, kh)],
                                                 [_with_ones(vb[rows, cols])], lam, g_scaled))
                for hh, o in zip(range(h0, h0 + ATT_P_WIDTH), _interleave(heads, ATT_SKEW)):
                    o_ref[rows, hh * HD:(hh + 1) * HD] = o

    _first_step_or_not(body)


def _attn_prompt(x, mod_l, g, w_qkv, j, lam_p, g_sub, lam_init, prev=None):
    n_layers = w_qkv.shape[0]
    assert (prev is None) == (j == 0)
    tile = pl.BlockSpec((TM, D), lambda i: (i, 0))
    in_specs = [tile, _const_spec((MOD_ROWS, N_MOD * D)), _const_spec((1, D)), HBM_SPEC,
                _const_spec((4, DH)), _const_spec((1, HD))]
    args = [x, mod_l, g, w_qkv, lam_p, g_sub]
    if prev is None:
        kt_spec = pl.BlockSpec((SEQ_PER_TILE, n_layers, D, SEQ), lambda i: (i, 0, 0, 0))
        v_spec = pl.BlockSpec((SEQ_PER_TILE, n_layers, SEQ, D), lambda i: (i, 0, 0, 0))
        aliases = {}
    else:
        kt_spec = pl.BlockSpec((SEQ_PER_TILE, None, D, SEQ), lambda i: (i, j, 0, 0))
        v_spec = pl.BlockSpec((SEQ_PER_TILE, None, SEQ, D), lambda i: (i, j, 0, 0))
        aliases = {len(args): 0, len(args) + 1: 1}
        in_specs += [HBM_SPEC, HBM_SPEC]
        args += list(prev)
    kern = lambda *refs: _attn_prompt_kernel(*refs, j=j, n_layers=n_layers, lam_init=lam_init)
    return pl.pallas_call(
        kern,
        grid=(NT_P,),
        in_specs=in_specs,
        out_specs=[kt_spec, v_spec, tile],
        out_shape=[jax.ShapeDtypeStruct((BATCH, n_layers, D, SEQ), F32),
                   jax.ShapeDtypeStruct((BATCH, n_layers, SEQ, D), F32), jax.ShapeDtypeStruct((N_P, D), BF16)],
        input_output_aliases=aliases,
        scratch_shapes=[pltpu.VMEM((D, 3 * D), BF16)] + _stream_scratch((D, W_PIECE), W_DEPTH),
        compiler_params=_cparams(1),
        name="attn_prompt",
    )(*args)


def _qkv_sample_kernel(x_ref, mod_ref, g_ref, w_hbm, cos_ref, sm_ref, sp_ref, q_ref, k_ref, v_ref,
                       w_bf, stage, sem, *, j):
    stream = _WeightStream(_column_jobs(w_hbm, j, w_bf, W_PIECE), stage, sem)
    per_mat = D // W_PIECE

    def body(first):
        if first:
            stream.prime()
        i = pl.program_id(0)
        m = mod_ref[pl.ds(1 + i // TILES_PER_DEC_SEQ, 1), :]
        h = _adanorm(x_ref[...], g_ref[...], m[:, 0:D], m[:, D:2 * D]).astype(BF16)
        cos, sm, sp = cos_ref[...], sm_ref[...], sp_ref[...]
        for n, (o_ref, scale) in enumerate(((q_ref, DH ** -0.5 * LOG2E), (k_ref, None), (v_ref, None))):
            if first:
                for p in range(per_mat):
                    stream.take(n * per_mat + p)
            y = _dot(h, w_bf[:, n * D:(n + 1) * D])
            if o_ref is v_ref:
                o_ref[...] = y.astype(BF16)
                continue
            if scale is not None:
                y = y * scale
            for hh in range(HEADS):
                t = _rope_tile(y[:, hh * HD:(hh + 1) * HD], cos, sm, sp)
                o_ref[:, hh * HD:(hh + 1) * HD] = t.astype(BF16)

    _first_step_or_not(body)


def _qkv_sample(x, tile0, mod_l, g, w_qkv, j, tabs):
    tile = pl.BlockSpec((TM, D), lambda i: (i, 0))
    tab = pl.BlockSpec((TM, HD), lambda i: (i % TILES_PER_DEC_SEQ, 0))
    kern = lambda *refs: _qkv_sample_kernel(*refs, j=j)
    return pl.pallas_call(
        kern,
        grid=(NT_S,),
        in_specs=[pl.BlockSpec((TM, D), lambda i: (i + tile0, 0)), _const_spec((MOD_ROWS, N_MOD * D)),
                  _const_spec((1, D)), HBM_SPEC, tab, tab, tab],
        out_specs=[tile, tile, tile],
        out_shape=[jax.ShapeDtypeStruct((N_S, D), BF16)] * 3,
        scratch_shapes=[pltpu.VMEM((D, 3 * D), BF16)] + _stream_scratch((D, W_PIECE), W_DEPTH),
        compiler_params=_cparams(1),
        name="qkv_sample",
    )(x, mod_l, g, w_qkv, *tabs)


def _attn_sample_kernel(*refs, lam_init):
    q_ref, kn_ref, vn_ref, kct_ref, vc_ref, lam_ref, gsub_ref, o_ref = refs
    lam = _lambda(lam_ref, lam_init)
    g_scaled = gsub_ref[...] * (1.0 - lam_init)
    for h0 in range(0, HEADS, ATT_HP):
        heads = []
        for hh in range(h0, h0 + ATT_HP):
            cols = slice(hh * HD, (hh + 1) * HD)
            kn = kn_ref[:, cols]
            kct = kct_ref[hh].astype(BF16)
            vc = vc_ref[:, cols].astype(BF16)
            heads.append(_diff_attn_head(
                q_ref[:, cols], [lambda qq, kn=kn: _dot_nt(qq, kn), lambda qq, kct=kct: _dot(qq, kct)],
                [_with_ones(vn_ref[:, cols]), _with_ones(vc)], lam, g_scaled))
        for hh, o in zip(range(h0, h0 + ATT_HP), _interleave(heads, ATT_SKEW)):
            o_ref[:, hh * HD:(hh + 1) * HD] = o


def _attn_sample(q, kn, vn, cache_kt, cache_v, j, lam_p, g_sub, lam_init):
    nq = DEC_SEQ // ATT_TQ
    seq = pl.BlockSpec((DEC_SEQ, D), lambda b, i: (b, 0))
    blk = pl.BlockSpec((ATT_TQ, D), lambda b, i: (b * nq + i, 0))
    in_specs = [blk, seq, seq,
                pl.BlockSpec((None, None, HEADS, HD, PAST), lambda b, i: (b, j, 0, 0, 0)),
                pl.BlockSpec((None, None, PAST, D), lambda b, i: (b, j, 0, 0)),
                pl.BlockSpec((4, DH), lambda b, i: (0, 0)), pl.BlockSpec((1, HD), lambda b, i: (0, 0))]
    kern = lambda *refs: _attn_sample_kernel(*refs, lam_init=lam_init)
    return pl.pallas_call(
        kern,
        grid=(DEC_BATCH, nq),
        in_specs=in_specs,
        out_specs=blk,
        out_shape=jax.ShapeDtypeStruct((N_S, D), BF16),
        compiler_params=_cparams(2),
        name="attn_sample",
    )(q, kn, vn, cache_kt, cache_v, lam_p, g_sub)


def _rec_proj_kernel(x_ref, mod_ref, g_ref, w_hbm, q_ref, v_ref, gate_ref, zf_ref, zb_ref,
                     w_bf, stage, sem, *, j):
    stream = _WeightStream(_column_jobs(w_hbm, j, w_bf, W_PIECE), stage, sem)
    per_mat = D // W_PIECE

    def body(first):
        if first:
            stream.prime()
        m = mod_ref[pl.ds(_mod_row(pl.program_id(0)), 1), :]
        h = _adanorm(x_ref[...], g_ref[...], m[:, 0:D], m[:, D:2 * D]).astype(BF16)
        for n, o_ref in enumerate((q_ref, v_ref, gate_ref, zf_ref, zb_ref)):
            if first:
                for p in range(per_mat):
                    stream.take(n * per_mat + p)
            y = _dot(h, w_bf[:, n * D:(n + 1) * D])
            if o_ref is q_ref or o_ref is gate_ref:
                y = _silu(y)
            o_ref[...] = y.astype(o_ref.dtype)

    _first_step_or_not(body)


def _rec_proj(x, mod_l, g, w_in, j):
    tile = pl.BlockSpec((TM, D), lambda i: (i, 0))
    kern = lambda *refs: _rec_proj_kernel(*refs, j=j)
    return pl.pallas_call(
        kern,
        grid=(N_TOK // TM,),
        in_specs=[tile, _const_spec((MOD_ROWS, N_MOD * D)), _const_spec((1, D)), HBM_SPEC],
        out_specs=[tile] * 5,
        out_shape=[jax.ShapeDtypeStruct((N_TOK, D), BF16)] * 3 + [jax.ShapeDtypeStruct((N_TOK, D), F32)] * 2,
        scratch_shapes=[pltpu.VMEM((D, 5 * D), BF16)] + _stream_scratch((D, W_PIECE), W_DEPTH),
        compiler_params=_cparams(1),
        name="rec_proj",
    )(x, mod_l, g, w_in)


def _gla_tables():
    t = np.arange(GLA_GROUP)[:, None]
    s = np.arange(GLA_GROUP)[None, :]
    ids = np.full((GLA_GROUP, GLA_GROUP), -1, np.int32)
    for l, b in reversed(list(enumerate(GLA_LEVELS, 1))):
        ids = np.where((t // (2 * b) == s // (2 * b)) & (s < t), l, ids)
    ids = np.where((t // GLA_BLK == s // GLA_BLK) & (s <= t), 0, ids)
    ids = np.stack([ids, ids.T]).astype(np.int32)
    e = np.zeros((GLA_BLK * HD, GLA_GROUP), np.float32)
    for j in range(GLA_BLK):
        e[j * HD:(j + 1) * HD, j::GLA_BLK] = 1.0
    return jnp.asarray(ids), jnp.asarray(e)


def _bcast_row(x, blk, idx):
    n, w = x.shape
    r = x.reshape(n // blk, blk, w)[:, idx:idx + 1, :]
    return jnp.broadcast_to(r, (n // blk, blk, w)).reshape(n, w)


def _gla_group(q, v, z, lb, ids, e_mat, st, backward):
    n = GLA_GROUP
    sig = _sigmoid(z)
    f = lb + (1.0 - lb) * sig
    k = (1.0 - lb) * (1.0 - sig)
    lf = jnp.log(f) * LOG2E
    row = lax.broadcasted_iota(jnp.int32, (n, n), 0)
    col = lax.broadcasted_iota(jnp.int32, (n, n), 1)
    tri = jnp.where((col >= row) if backward else (col <= row), 1.0, 0.0).astype(BF16)
    c2 = _dot(tri, jnp.concatenate(_split2(lf), axis=1))
    cum = c2[:, 0:HD] + c2[:, HD:2 * HD]
    ck = jnp.log(k) * LOG2E - cum
    yield

    prods = []
    for j in range(GLA_BLK):
        prods.append((q * jnp.exp2(jnp.minimum(cum + _bcast_row(ck, GLA_BLK, j), 0.0))).astype(BF16))
        yield
    half = n // 2
    ids_q = ids[0:half, 0:half]
    diag = lambda x, h: x[h * half:(h + 1) * half, h * half:(h + 1) * half]
    pair = _dot(jnp.concatenate(prods, axis=1), e_mat)
    quads = [jnp.where(ids_q == 0, diag(pair, h), 0.0) for h in range(2)]
    yield

    for l, b in enumerate(GLA_LEVELS[:-1], 1):
        edge = _bcast_row(cum, 2 * b, b if backward else b - 1)
        ql = (q * jnp.exp2(cum - edge)).astype(BF16)
        kl = jnp.exp2(edge + ck).astype(BF16)
        x = _dot_nt(ql, kl)
        quads = [jnp.where(ids_q == l, diag(x, h), quads[h]) for h in range(2)]
        yield

    assert GLA_LEVELS[-1] == half
    lo, hi = slice(0, half), slice(half, n)
    qs, ks = (lo, hi) if backward else (hi, lo)
    edge = cum[half:half + 1, :] if backward else cum[half - 1:half, :]
    cross = _dot_nt((q[qs] * jnp.exp2(cum[qs] - edge)).astype(BF16),
                    jnp.exp2(edge + ck[ks]).astype(BF16)).astype(BF16)
    o_halves = [_dot(quads[h].astype(BF16), v[h * half:(h + 1) * half]) for h in range(2)]
    o_halves[0 if backward else 1] += _dot(cross, v[ks])
    o = jnp.concatenate(o_halves, axis=0)
    yield

    total = cum[0:1, :] if backward else cum[n - 1:n, :]
    if st is not None:
        o = o + _dot_nt((q * jnp.exp2(cum)).astype(BF16), st.astype(BF16))
    kdec = jnp.exp2(total + ck).astype(BF16)
    upd = _dot_tn(v, kdec)
    st_new = upd if st is None else st * jnp.exp2(total) + upd
    return o, st_new


def _lower_bound(x, layer):
    rows = [x[l:l + 1, :] for l in range(DEPTH)]
    mx = rows[0]
    for r in rows[1:]:
        mx = jnp.maximum(mx, r)
    ex = [jnp.exp(r - mx) for r in rows]
    tot = ex[0]
    for e in ex[1:]:
        tot = tot + e
    acc = ex[0] * 0.0
    for l in range(1, layer + 1):
        acc = acc + ex[l]
    return acc / tot


def _gla_kernel(*refs, n_groups, layer, has_state):
    if has_state:
        (q_ref, v_ref, gate_ref, zf_ref, zb_ref, lbl_ref, g_ref, ids_ref, e_ref, s0_ref,
         o_ref, acc_ref) = refs
    else:
        (q_ref, v_ref, gate_ref, zf_ref, zb_ref, lbl_ref, g_ref, ids_ref, e_ref,
         o_ref, sout_ref, acc_ref) = refs
    e_mat = e_ref[...]
    z_refs = (zf_ref, zb_ref)
    chains = [(hh, d) for hh in range(GLA_HP) for d in range(2)]
    cols = [slice(hh * HD, (hh + 1) * HD) for hh in range(GLA_HP)]
    lbs = {(hh, d): _lower_bound(lbl_ref[d, :, cols[hh]], layer) for hh, d in chains}

    def step(gi, sts):
        rows, gens = [], []
        for d in range(2):
            g = gi if d == 0 else n_groups - 1 - gi
            start = g * GLA_GROUP
            rows.append(pl.ds(start if isinstance(start, int) else pl.multiple_of(start, GLA_GROUP), GLA_GROUP))
        for c, (hh, d) in enumerate(chains):
            gens.append(_gla_group(q_ref[rows[d], cols[hh]].astype(F32), v_ref[rows[d], cols[hh]],
                                   z_refs[d][rows[d], cols[hh]], lbs[hh, d], ids_ref[d], e_mat, sts[c],
                                   backward=(d == 1)))
        new = []
        for (hh, d), (o, st) in zip(chains, _interleave(gens)):
            acc_ref[rows[d], cols[hh]] = acc_ref[rows[d], cols[hh]] + o
            new.append(st)
        return tuple(new)

    acc_ref[...] = jnp.zeros_like(acc_ref)
    if has_state:
        init = tuple(s0_ref[d, hh].T for hh, d in chains)
        lax.fori_loop(0, n_groups, step, init)
    else:
        sts = step(0, (None,) * len(chains))
        for (hh, d), st in zip(chains, sts):
            sout_ref[d, hh] = st.T
    for hh in range(GLA_HP):
        o = acc_ref[:, cols[hh]]
        y = o * lax.rsqrt(jnp.mean(o * o, axis=-1, keepdims=True) + EPS) * g_ref[...]
        o_ref[:, cols[hh]] = (y * gate_ref[:, cols[hh]].astype(F32)).astype(BF16)


def _gla(q, v, gate, zf, zb, lb_logits, g_out, tabs, *, row0, nb, ln, layer, s0=None):
    ids, e_mat = tabs
    assert ln % GLA_GROUP == 0 and (s0 is not None or ln == GLA_GROUP)
    b0 = row0 // ln
    w = GLA_HP * HD
    seq = pl.BlockSpec((ln, w), lambda b, h: (b + b0, h))
    in_specs = [seq] * 5 + [
        pl.BlockSpec((2, DEPTH, w), lambda b, h: (0, 0, h)),
        pl.BlockSpec((1, HD), lambda b, h: (0, 0)),
        pl.BlockSpec((2, GLA_GROUP, GLA_GROUP), lambda b, h: (0, 0, 0)),
        pl.BlockSpec((GLA_BLK * HD, GLA_GROUP), lambda b, h: (0, 0)),
    ]
    args = [q, v, gate, zf, zb, lb_logits, g_out, ids, e_mat]
    out_specs = [pl.BlockSpec((ln, w), lambda b, h: (b, h))]
    out_shape = [jax.ShapeDtypeStruct((nb * ln, D), BF16)]
    state_spec = pl.BlockSpec((None, 2, GLA_HP, HD, HD), lambda b, h: (b, 0, h, 0, 0))
    if s0 is not None:
        in_specs.append(state_spec)
        args.append(s0)
    else:
        out_specs.append(state_spec)
        out_shape.append(jax.ShapeDtypeStruct((nb, 2, HEADS, HD, HD), F32))
    kern = lambda *refs: _gla_kernel(*refs, n_groups=ln // GLA_GROUP, layer=layer, has_state=s0 is not None)
    return pl.pallas_call(
        kern,
        grid=(nb, HEADS // GLA_HP),
        in_specs=in_specs,
        out_specs=out_specs,
        out_shape=out_shape,
        scratch_shapes=[pltpu.VMEM((ln, w), F32)],
        compiler_params=_cparams(2),
        name="gla_sample" if s0 is not None else "gla_prompt",
    )(*args)


def _dft_tables(n):
    c = np.arange(FOUR_DG)
    ang_c = 2.0 * np.pi * ((c[:, None] * c[None, :]) % FOUR_DG) / FOUR_DG
    cs = np.concatenate([np.cos(ang_c), np.sin(ang_c)], axis=1)
    t = np.arange(n)
    ang_n = 2.0 * np.pi * ((t[:, None] * t[None, :]) % n) / n
    cn = np.concatenate([np.cos(ang_n), -np.sin(ang_n)], axis=1) / math.sqrt(n * FOUR_DG)
    return jnp.asarray(cs, F32).astype(BF16), jnp.asarray(cn, F32).astype(BF16)


def _fourier_kernel(x_ref, mod_ref, g_ref, cs_ref, cn_ref, o_ref, *, row_of):
    m = mod_ref[pl.ds(row_of(pl.program_id(0)), 1), :]
    h = _adanorm(x_ref[...], g_ref[...], m[:, 0:D], m[:, D:2 * D]).astype(BF16)
    cs = cs_ref[...]
    cn = cn_ref[...]
    for g in range(FOUR_GROUPS):
        xcs = _dot(h[:, g * FOUR_DG:(g + 1) * FOUR_DG], cs)
        stacked = jnp.concatenate([xcs[:, 0:FOUR_DG], xcs[:, FOUR_DG:]], axis=0).astype(BF16)
        o_ref[:, g * FOUR_DG:(g + 1) * FOUR_DG] = _dot(cn, stacked).astype(BF16)


def _fourier(x, mod_l, g, *, row0, nb, ln, row_of):
    cs, cn = _dft_tables(ln)
    b0 = row0 // ln
    kern = lambda *refs: _fourier_kernel(*refs, row_of=row_of)
    return pl.pallas_call(
        kern,
        grid=(nb,),
        in_specs=[pl.BlockSpec((ln, D), lambda b: (b + b0, 0)), _const_spec((MOD_ROWS, N_MOD * D)),
                  _const_spec((1, D)), _const_spec((FOUR_DG, 2 * FOUR_DG)), _const_spec((ln, 2 * ln))],
        out_specs=pl.BlockSpec((ln, D), lambda b: (b, 0)),
        out_shape=jax.ShapeDtypeStruct((nb * ln, D), BF16),
        compiler_params=_cparams(1),
        name="fourier_%d" % ln,
    )(x, mod_l, g, cs, cn)


FF_DEPTH_COLS = 4
FF_DEPTH_ROWS = 2


def _post_ffn_kernel(*refs, layer, wo_index, split_x, final):
    refs = list(refs)
    x_refs = [refs.pop(0) for _ in range(2 if split_x else 1)]
    op_ref, os_ref, mod_ref, wo_hbm, g_ref, win_hbm, wout_hbm = refs[:7]
    refs = refs[7:]
    gf_ref = refs.pop(0) if final else None
    out_refs = [refs.pop(0) for _ in range(2 if final else 1)]
    wo_bf, win_bf, wout_bf, stage_c, sem_c, stage_r, sem_r = refs
    n_chunks = D_FF // FF_CHUNK
    wo_jobs = _column_jobs(wo_hbm, wo_index, wo_bf, FF_CHUNK)
    in_jobs = []
    for c in range(n_chunks):
        for c0 in (c * FF_CHUNK, D_FF + c * FF_CHUNK):
            in_jobs.append((win_hbm.at[layer, :, pl.ds(c0, FF_CHUNK)], win_bf.at[:, pl.ds(c0, FF_CHUNK)]))
    cols = _WeightStream(wo_jobs + in_jobs, stage_c, sem_c)
    rows = _WeightStream([(wout_hbm.at[layer, pl.ds(c * FF_CHUNK, FF_CHUNK), :],
                           wout_bf.at[pl.ds(c * FF_CHUNK, FF_CHUNK), :]) for c in range(n_chunks)],
                         stage_r, sem_r)

    def body(first):
        if first:
            cols.prime()
            rows.prime()
        i = pl.program_id(0)
        ctx = i < NT_P
        m = mod_ref[pl.ds(_mod_row(i), 1), :]
        x = jnp.where(ctx, x_refs[0][...], x_refs[1][...]) if split_x else x_refs[0][...]
        o = jnp.where(ctx, op_ref[...], os_ref[...])
        if first:
            for k in range(len(wo_jobs)):
                cols.take(k)
        x = x + m[:, 2 * D:3 * D] * _dot(o, wo_bf[...])
        h = _adanorm(x, g_ref[...], m[:, 3 * D:4 * D], m[:, 4 * D:5 * D]).astype(BF16)
        acc = None
        for c in range(n_chunks):
            if first:
                cols.take(len(wo_jobs) + 2 * c)
                cols.take(len(wo_jobs) + 2 * c + 1)
                rows.take(c)
            gt = _dot(h, win_bf[:, c * FF_CHUNK:(c + 1) * FF_CHUNK])
            up = _dot(h, win_bf[:, D_FF + c * FF_CHUNK:D_FF + (c + 1) * FF_CHUNK])
            part = _dot((_silu(gt) * up).astype(BF16), wout_bf[c * FF_CHUNK:(c + 1) * FF_CHUNK, :])
            acc = part if acc is None else acc + part
        x = x + m[:, 5 * D:6 * D] * acc
        if not final:
            out_refs[0][...] = x
            return
        y = x * lax.rsqrt(jnp.mean(x * x, axis=-1, keepdims=True) + EPS) * gf_ref[...]
        if first:
            out_refs[0][...] = y
            return

        @pl.when(ctx)
        def _():
            out_refs[0][...] = y

        @pl.when(jnp.logical_not(ctx))
        def _():
            out_refs[1][...] = y

    _first_step_or_not(body)


def _post_ffn(xs, op, os_, mod_l, w_o, wo_index, g, w_in, w_out, layer, g_final=None):
    tile = pl.BlockSpec((TM, D), lambda i: (i, 0))
    p_tile = pl.BlockSpec((TM, D), lambda i: (jnp.minimum(i, NT_P - 1), 0))
    s_tile = pl.BlockSpec((TM, D), lambda i: (jnp.maximum(i - NT_P, 0), 0))
    split_x = len(xs) == 2
    final = g_final is not None
    in_specs = ([p_tile, s_tile] if split_x else [tile]) + [
        p_tile, s_tile, _const_spec((MOD_ROWS, N_MOD * D)), HBM_SPEC, _const_spec((1, D)), HBM_SPEC, HBM_SPEC]
    args = list(xs) + [op, os_, mod_l, w_o, g, w_in, w_out]
    if final:
        in_specs.append(_const_spec((1, D)))
        args.append(g_final)
        out_specs = [p_tile, s_tile]
        out_shape = [jax.ShapeDtypeStruct((N_P, D), F32), jax.ShapeDtypeStruct((N_S, D), F32)]
    else:
        out_specs = tile
        out_shape = jax.ShapeDtypeStruct((N_TOK, D), F32)
    kern = lambda *refs: _post_ffn_kernel(*refs, layer=layer, wo_index=wo_index, split_x=split_x, final=final)
    return pl.pallas_call(
        kern,
        grid=(N_TOK // TM,),
        in_specs=in_specs,
        out_specs=out_specs,
        out_shape=out_shape,
        scratch_shapes=([pltpu.VMEM((D, D), BF16), pltpu.VMEM((D, 2 * D_FF), BF16), pltpu.VMEM((D_FF, D), BF16)]
                        + _stream_scratch((D, FF_CHUNK), FF_DEPTH_COLS)
                        + _stream_scratch((FF_CHUNK, D), FF_DEPTH_ROWS)),
        compiler_params=_cparams(1),
        name="post_ffn",
    )(*args)


def kernel(x_prompt, x_sample, cache_attn_k, cache_attn_v, state_hgrn, c, c_ctx, w_ada, b_ada, g_norm_mix,
           g_norm_ffn, w_qkv_attn, lam_attn, g_subln_attn, w_o_attn, w_in_rec, lb_logits_rec, g_out_rec,
           w_o_rec, w_four, w_ffn_in, w_ffn_out, g_final):
    xs = (x_prompt.reshape(N_P, D), x_sample.reshape(N_S, D))
    cond = jnp.concatenate([c_ctx.reshape(1, D), c, jnp.zeros((MOD_ROWS - 1 - DEC_BATCH, D), F32)], axis=0)
    mod = _modulation(cond, w_ada, b_ada)
    cache_kt = jnp.transpose(cache_attn_k, (0, 1, 3, 4, 5, 2)).reshape(DEC_BATCH, -1, HEADS, HD, PAST)
    cache_v = cache_attn_v.reshape(DEC_BATCH, -1, PAST, D)
    rope_tabs = _rope_tables()
    gla_tabs = _gla_tables()
    gla_tabs = (gla_tabs[0], gla_tabs[1].astype(BF16))
    kv_ctx, new_s = None, []
    for i in range(DEPTH):
        kind, j = i % N_MIXERS, i // N_MIXERS
        g_mix = g_norm_mix[i].reshape(1, D)
        x_p = xs[0]
        x_s, s_tile0 = (xs[1], 0) if len(xs) == 2 else (xs[0], NT_P)
        if kind == 0:
            lam_init = 0.8 - 0.6 * math.exp(-0.3 * i)
            g_sub = g_subln_attn[j].reshape(1, HD)
            kt_all, v_all, op = _attn_prompt(x_p, mod[i], g_mix, w_qkv_attn, j, lam_attn[j], g_sub, lam_init,
                                             prev=kv_ctx)
            kv_ctx = (kt_all, v_all)
            qs, ks, vs = _qkv_sample(x_s, s_tile0, mod[i], g_mix, w_qkv_attn, j, rope_tabs)
            os_ = _attn_sample(qs, ks, vs, cache_kt, cache_v, j, lam_attn[j], g_sub, lam_init)
            w_o = w_o_attn
        elif kind == 1:
            proj = _rec_proj(xs[0], mod[i], g_mix, w_in_rec, j)
            g_out = g_out_rec[j].reshape(1, HD)
            op, s_ctx = _gla(*proj, lb_logits_rec, g_out, gla_tabs, row0=0, nb=BATCH, ln=SEQ, layer=i)
            new_s.append(s_ctx)
            os_ = _gla(*proj, lb_logits_rec, g_out, gla_tabs, row0=N_P, nb=DEC_BATCH, ln=DEC_SEQ, layer=i,
                       s0=state_hgrn[:, j])[0]
            w_o = w_o_rec
        else:
            op = _fourier(xs[0], mod[i], g_mix, row0=0, nb=BATCH, ln=SEQ, row_of=lambda b: 0)
            os_ = _fourier(xs[0], mod[i], g_mix, row0=N_P, nb=DEC_BATCH, ln=DEC_SEQ, row_of=lambda b: 1 + b)
            w_o = w_four
        out = _post_ffn(xs, op, os_, mod[i], w_o, j, g_norm_ffn[i].reshape(1, D), w_ffn_in, w_ffn_out, i,
                        g_final.reshape(1, D) if i == DEPTH - 1 else None)
        xs = tuple(out) if i == DEPTH - 1 else (out,)
    y_prompt = xs[0].reshape(BATCH, SEQ, D)
    y_sample = xs[1].reshape(DEC_BATCH, DEC_SEQ, D)
    new_k = jnp.transpose(kv_ctx[0].reshape(BATCH, -1, HEADS, 2, DH, SEQ), (0, 1, 5, 2, 3, 4))
    new_v = kv_ctx[1].reshape(BATCH, -1, SEQ, HEADS, HD)
    return (y_prompt, y_sample, new_k, new_v, jnp.stack(new_s, axis=1))
```

```python
import math

import numpy as np
import jax
import jax.numpy as jnp
from jax import lax
from jax.experimental import pallas as pl
from jax.experimental.pallas import tpu as pltpu

F32 = jnp.float32
BF16 = jnp.bfloat16

D = 1024
BATCH = 16
SEQ = 256
DEPTH = 4
DEC_BATCH = 4
DEC_SEQ = 1024
PAST = 512
GRID_W = 64
N_MIXERS = 3
HEADS = 8
DH = 64
HD = 128
ROPE_THETA = 10000.0
D_FF = 2816
N_MOD = 6
EPS = 1e-6
LOG2E = 1.0 / math.log(2.0)
FOUR_GROUPS = 4
FOUR_DG = 256

N_P = BATCH * SEQ
N_S = DEC_BATCH * DEC_SEQ
N_TOK = N_P + N_S
TM = 512
NT_P = N_P // TM
NT_S = N_S // TM
TILES_PER_DEC_SEQ = DEC_SEQ // TM
SEQ_PER_TILE = TM // SEQ
MOD_ROWS = 8
FF_CHUNK = 256
ATT_TQ = 512
ATT_HP = 2
ATT_P_WIDTH = 2
ATT_SKEW = 1
GLA_HP = 8
GLA_GROUP = 256
GLA_BLK = 8
GLA_LEVELS = (8, 16, 32, 64, 128)
VMEM_LIMIT = 60 * 1024 * 1024


def _cparams(n_axes):
    return pltpu.CompilerParams(dimension_semantics=("arbitrary",) * n_axes,
                                vmem_limit_bytes=VMEM_LIMIT)


def _const_spec(shape):
    nd = len(shape)
    return pl.BlockSpec(shape, lambda *_: (0,) * nd, pipeline_mode=pl.Buffered(1))


def _sigmoid(x):
    return 1.0 / (1.0 + jnp.exp(-x))


def _silu(x):
    return x * _sigmoid(x)


def _adanorm(x, g, shift, scale):
    ms = jnp.mean(x * x, axis=-1, keepdims=True)
    return x * lax.rsqrt(ms + EPS) * (g * (1.0 + scale)) + shift


def _mod_row(tile):
    return jnp.where(tile < NT_P, 0, 1 + (tile - NT_P) // TILES_PER_DEC_SEQ)


def _dot(a, b):
    return jnp.dot(a, b, preferred_element_type=F32)


def _dot_nt(a, b):
    return lax.dot_general(a, b, (((1,), (1,)), ((), ())), preferred_element_type=F32)


def _dot_tn(a, b):
    return lax.dot_general(a, b, (((0,), (0,)), ((), ())), preferred_element_type=F32)


def _split2(x):
    hi = x.astype(BF16)
    return hi, (x - hi.astype(F32)).astype(BF16)


def _interleave(generators, skew=0):
    results = [None] * len(generators)
    active = list(range(len(generators)))
    rnd = 0
    while active:
        for i in list(active):
            if rnd < i * skew:
                continue
            try:
                next(generators[i])
            except StopIteration as stop:
                results[i] = stop.value
                active.remove(i)
        rnd += 1
    return results


class _WeightStream:
    def __init__(self, jobs, stage_ref, sem_ref):
        self.jobs, self.stage, self.sem = jobs, stage_ref, sem_ref
        self.depth = stage_ref.shape[0]

    def _copy(self, k):
        slot = k % self.depth
        return pltpu.make_async_copy(self.jobs[k][0], self.stage.at[slot], self.sem.at[slot])

    def prime(self):
        for k in range(min(self.depth, len(self.jobs))):
            self._copy(k).start()

    def take(self, k):
        self._copy(k).wait()
        self.jobs[k][1][...] = self.stage[k % self.depth].astype(BF16)
        if k + self.depth < len(self.jobs):
            self._copy(k + self.depth).start()


def _stream_scratch(piece_shape, depth):
    return [pltpu.VMEM((depth,) + tuple(piece_shape), F32), pltpu.SemaphoreType.DMA((depth,))]


def _column_jobs(w_hbm, index, w_bf, width):
    return [(w_hbm.at[index, :, pl.ds(c, width)], w_bf.at[:, pl.ds(c, width)])
            for c in range(0, w_bf.shape[1], width)]


def _first_step_or_not(body):
    first = pl.program_id(0) == 0
    pl.when(first)(lambda: body(True))
    pl.when(jnp.logical_not(first))(lambda: body(False))


HBM_SPEC = pl.BlockSpec(memory_space=pl.ANY)
W_PIECE = 512
W_DEPTH = 3


MOD_TK = 256


def _mod_kernel(cond_ref, w_ref, b_ref, o_ref):
    kk = pl.program_id(1)
    s = _silu(cond_ref[...])
    s_hi = s.astype(BF16)
    s_lo = (s - s_hi.astype(F32)).astype(BF16)
    w = w_ref[0]
    w_hi = w.astype(BF16)
    w_lo = (w - w_hi.astype(F32)).astype(BF16)
    r = _dot(jnp.concatenate([s_hi, s_lo], axis=0), w_hi)
    part = r[:MOD_ROWS] + r[MOD_ROWS:] + _dot(s_hi, w_lo)

    @pl.when(kk == 0)
    def _():
        o_ref[0] = part + b_ref[0]

    @pl.when(kk != 0)
    def _():
        o_ref[0] = o_ref[0] + part


def _modulation(cond, w_ada, b_ada):
    return pl.pallas_call(
        _mod_kernel,
        grid=(DEPTH, D // MOD_TK),
        in_specs=[
            pl.BlockSpec((MOD_ROWS, MOD_TK), lambda l, k: (0, k)),
            pl.BlockSpec((1, MOD_TK, N_MOD * D), lambda l, k: (l, k, 0)),
            pl.BlockSpec((1, 1, N_MOD * D), lambda l, k: (l, 0, 0)),
        ],
        out_specs=pl.BlockSpec((1, MOD_ROWS, N_MOD * D), lambda l, k: (l, 0, 0)),
        out_shape=jax.ShapeDtypeStruct((DEPTH, MOD_ROWS, N_MOD * D), F32),
        compiler_params=_cparams(2),
        name="modulation",
    )(cond, w_ada, b_ada.reshape(DEPTH, 1, N_MOD * D))


def _rope_tables():
    t = np.arange(DEC_SEQ)
    row, col = t // GRID_W, t % GRID_W
    lane = np.arange(HD) % DH
    use_row = lane < DH // 2
    idx = (lane % (DH // 2)) % (DH // 4)
    inv = ROPE_THETA ** (-idx.astype(np.float64) / (DH // 4))
    pos = np.where(use_row[None, :], row[:, None], col[:, None]).astype(np.float64)
    ang = pos * inv[None, :]
    first = (lane % (DH // 2)) < DH // 4
    cos = np.cos(ang)
    sin = np.sin(ang)
    sin_minus = np.where(first[None, :], -sin, 0.0)
    sin_plus = np.where(first[None, :], 0.0, sin)
    return (jnp.asarray(cos, F32), jnp.asarray(sin_minus, F32), jnp.asarray(sin_plus, F32))


def _rope_tile(t, cos, sin_minus, sin_plus):
    return (t * cos + pltpu.roll(t, HD - DH // 4, 1) * sin_minus
            + pltpu.roll(t, DH // 4, 1) * sin_plus)


def _lambda(lam_ref, lam_init):
    lp = lam_ref[...]
    return (jnp.exp(jnp.sum(lp[0:1] * lp[1:2], axis=-1, keepdims=True))
            - jnp.exp(jnp.sum(lp[2:3] * lp[3:4], axis=-1, keepdims=True)) + lam_init)


def _with_ones(v):
    return jnp.concatenate([v, jnp.ones_like(v)], axis=1)


def _diff_attn_head(q, score_fns, values, lam, g_scaled):
    tq = q.shape[0]
    lane = lax.broadcasted_iota(jnp.int32, q.shape, 1)
    zero = jnp.zeros_like(q)
    qq = jnp.concatenate([jnp.where(lane < DH, q, zero), jnp.where(lane >= DH, q, zero)], axis=0)
    scores = [fn(qq) for fn in score_fns]
    yield
    mx = scores[0].max(axis=-1, keepdims=True)
    for s in scores[1:]:
        mx = jnp.maximum(mx, s.max(axis=-1, keepdims=True))
    yield
    acc = None
    for s, v in zip(scores, values):
        c = _dot(jnp.exp2(s - mx).astype(BF16), v)
        acc = c if acc is None else acc + c
        yield
    r = acc[:, 0:HD] / acc[:, HD:2 * HD]
    o = r[:tq] - lam * r[tq:]
    o = o * lax.rsqrt(jnp.mean(o * o, axis=-1, keepdims=True) + EPS) * g_scaled
    return o.astype(BF16)


def _attn_prompt_kernel(*refs, j, n_layers, lam_init):
    x_ref, mod_ref, g_ref, w_hbm, lam_ref, gsub_ref = refs[:6]
    kt_ref, v_ref, o_ref, w_bf, stage, sem = refs[-6:]
    owns_all = j == 0
    stream = _WeightStream(_column_jobs(w_hbm, j, w_bf, W_PIECE), stage, sem)
    per_mat = D // W_PIECE

    def body(first):
        if first:
            stream.prime()
        m = mod_ref[pl.ds(0, 1), :]
        h = _adanorm(x_ref[...], g_ref[...], m[:, 0:D], m[:, D:2 * D]).astype(BF16)
        lam = _lambda(lam_ref, lam_init)
        g_scaled = gsub_ref[...] * (1.0 - lam_init)
        mats = []
        for n in range(3):
            if first:
                for p in range(per_mat):
                    stream.take(n * per_mat + p)
            mats.append(_dot(h, w_bf[:, n * D:(n + 1) * D]))
        q = (mats[0] * (DH ** -0.5 * LOG2E)).astype(BF16)
        k, v = mats[1], mats[2]
        kb, vb = k.astype(BF16), v.astype(BF16)
        for s in range(SEQ_PER_TILE):
            rows = slice(s * SEQ, (s + 1) * SEQ)
            if owns_all:
                for jj in range(n_layers):
                    kt_ref[s, jj] = k[rows].T if jj == j else jnp.zeros((D, SEQ), F32)
                    v_ref[s, jj] = v[rows] if jj == j else jnp.zeros((SEQ, D), F32)
            else:
                kt_ref[s] = k[rows].T
                v_ref[s] = v[rows]
            for h0 in range(0, HEADS, ATT_P_WIDTH):
                heads = []
                for hh in range(h0, h0 + ATT_P_WIDTH):
                    cols = slice(hh * HD, (hh + 1) * HD)
                    kh = kb[rows, cols]
                    heads.append(_diff_attn_head(q[rows, cols], [lambda qq, kh=kh: _dot_nt(qq, kh)],
                                                 [_with_ones(vb[rows, cols])], lam, g_scaled))
                for hh, o in zip(range(h0, h0 + ATT_P_WIDTH), _interleave(heads, ATT_SKEW)):
                    o_ref[rows, hh * HD:(hh + 1) * HD] = o

    _first_step_or_not(body)


def _attn_prompt(x, mod_l, g, w_qkv, j, lam_p, g_sub, lam_init, prev=None):
    n_layers = w_qkv.shape[0]
    assert (prev is None) == (j == 0)
    tile = pl.BlockSpec((TM, D), lambda i: (i, 0))
    in_specs = [tile, _const_spec((MOD_ROWS, N_MOD * D)), _const_spec((1, D)), HBM_SPEC,
                _const_spec((4, DH)), _const_spec((1, HD))]
    args = [x, mod_l, g, w_qkv, lam_p, g_sub]
    if prev is None:
        kt_spec = pl.BlockSpec((SEQ_PER_TILE, n_layers, D, SEQ), lambda i: (i, 0, 0, 0))
        v_spec = pl.BlockSpec((SEQ_PER_TILE, n_layers, SEQ, D), lambda i: (i, 0, 0, 0))
        aliases = {}
    else:
        kt_spec = pl.BlockSpec((SEQ_PER_TILE, None, D, SEQ), lambda i: (i, j, 0, 0))
        v_spec = pl.BlockSpec((SEQ_PER_TILE, None, SEQ, D), lambda i: (i, j, 0, 0))
        aliases = {len(args): 0, len(args) + 1: 1}
        in_specs += [HBM_SPEC, HBM_SPEC]
        args += list(prev)
    kern = lambda *refs: _attn_prompt_kernel(*refs, j=j, n_layers=n_layers, lam_init=lam_init)
    return pl.pallas_call(
        kern,
        grid=(NT_P,),
        in_specs=in_specs,
        out_specs=[kt_spec, v_spec, tile],
        out_shape=[jax.ShapeDtypeStruct((BATCH, n_layers, D, SEQ), F32),
                   jax.ShapeDtypeStruct((BATCH, n_layers, SEQ, D), F32), jax.ShapeDtypeStruct((N_P, D), BF16)],
        input_output_aliases=aliases,
        scratch_shapes=[pltpu.VMEM((D, 3 * D), BF16)] + _stream_scratch((D, W_PIECE), W_DEPTH),
        compiler_params=_cparams(1),
        name="attn_prompt",
    )(*args)


def _qkv_sample_kernel(x_ref, mod_ref, g_ref, w_hbm, cos_ref, sm_ref, sp_ref, q_ref, k_ref, v_ref,
                       w_bf, stage, sem, *, j):
    stream = _WeightStream(_column_jobs(w_hbm, j, w_bf, W_PIECE), stage, sem)
    per_mat = D // W_PIECE

    def body(first):
        if first:
            stream.prime()
        i = pl.program_id(0)
        m = mod_ref[pl.ds(1 + i // TILES_PER_DEC_SEQ, 1), :]
        h = _adanorm(x_ref[...], g_ref[...], m[:, 0:D], m[:, D:2 * D]).astype(BF16)
        cos, sm, sp = cos_ref[...], sm_ref[...], sp_ref[...]
        for n, (o_ref, scale) in enumerate(((q_ref, DH ** -0.5 * LOG2E), (k_ref, None), (v_ref, None))):
            if first:
                for p in range(per_mat):
                    stream.take(n * per_mat + p)
            y = _dot(h, w_bf[:, n * D:(n + 1) * D])
            if o_ref is v_ref:
                o_ref[...] = y.astype(BF16)
                continue
            if scale is not None:
                y = y * scale
            for hh in range(HEADS):
                t = _rope_tile(y[:, hh * HD:(hh + 1) * HD], cos, sm, sp)
                o_ref[:, hh * HD:(hh + 1) * HD] = t.astype(BF16)

    _first_step_or_not(body)


def _qkv_sample(x, tile0, mod_l, g, w_qkv, j, tabs):
    tile = pl.BlockSpec((TM, D), lambda i: (i, 0))
    tab = pl.BlockSpec((TM, HD), lambda i: (i % TILES_PER_DEC_SEQ, 0))
    kern = lambda *refs: _qkv_sample_kernel(*refs, j=j)
    return pl.pallas_call(
        kern,
        grid=(NT_S,),
        in_specs=[pl.BlockSpec((TM, D), lambda i: (i + tile0, 0)), _const_spec((MOD_ROWS, N_MOD * D)),
                  _const_spec((1, D)), HBM_SPEC, tab, tab, tab],
        out_specs=[tile, tile, tile],
        out_shape=[jax.ShapeDtypeStruct((N_S, D), BF16)] * 3,
        scratch_shapes=[pltpu.VMEM((D, 3 * D), BF16)] + _stream_scratch((D, W_PIECE), W_DEPTH),
        compiler_params=_cparams(1),
        name="qkv_sample",
    )(x, mod_l, g, w_qkv, *tabs)


def _attn_sample_kernel(*refs, lam_init):
    q_ref, kn_ref, vn_ref, kct_ref, vc_ref, lam_ref, gsub_ref, o_ref = refs
    lam = _lambda(lam_ref, lam_init)
    g_scaled = gsub_ref[...] * (1.0 - lam_init)
    for h0 in range(0, HEADS, ATT_HP):
        heads = []
        for hh in range(h0, h0 + ATT_HP):
            cols = slice(hh * HD, (hh + 1) * HD)
            kn = kn_ref[:, cols]
            kct = kct_ref[hh].astype(BF16)
            vc = vc_ref[:, cols].astype(BF16)
            heads.append(_diff_attn_head(
                q_ref[:, cols], [lambda qq, kn=kn: _dot_nt(qq, kn), lambda qq, kct=kct: _dot(qq, kct)],
                [_with_ones(vn_ref[:, cols]), _with_ones(vc)], lam, g_scaled))
        for hh, o in zip(range(h0, h0 + ATT_HP), _interleave(heads, ATT_SKEW)):
            o_ref[:, hh * HD:(hh + 1) * HD] = o


def _attn_sample(q, kn, vn, cache_kt, cache_v, j, lam_p, g_sub, lam_init):
    nq = DEC_SEQ // ATT_TQ
    seq = pl.BlockSpec((DEC_SEQ, D), lambda b, i: (b, 0))
    blk = pl.BlockSpec((ATT_TQ, D), lambda b, i: (b * nq + i, 0))
    in_specs = [blk, seq, seq,
                pl.BlockSpec((None, None, HEADS, HD, PAST), lambda b, i: (b, j, 0, 0, 0)),
                pl.BlockSpec((None, None, PAST, D), lambda b, i: (b, j, 0, 0)),
                pl.BlockSpec((4, DH), lambda b, i: (0, 0)), pl.BlockSpec((1, HD), lambda b, i: (0, 0))]
    kern = lambda *refs: _attn_sample_kernel(*refs, lam_init=lam_init)
    return pl.pallas_call(
        kern,
        grid=(DEC_BATCH, nq),
        in_specs=in_specs,
        out_specs=blk,
        out_shape=jax.ShapeDtypeStruct((N_S, D), BF16),
        compiler_params=_cparams(2),
        name="attn_sample",
    )(q, kn, vn, cache_kt, cache_v, lam_p, g_sub)


def _rec_proj_kernel(x_ref, mod_ref, g_ref, w_hbm, q_ref, v_ref, gate_ref, zf_ref, zb_ref,
                     w_bf, stage, sem, *, j):
    stream = _WeightStream(_column_jobs(w_hbm, j, w_bf, W_PIECE), stage, sem)
    per_mat = D // W_PIECE

    def body(first):
        if first:
            stream.prime()
        m = mod_ref[pl.ds(_mod_row(pl.program_id(0)), 1), :]
        h = _adanorm(x_ref[...], g_ref[...], m[:, 0:D], m[:, D:2 * D]).astype(BF16)
        for n, o_ref in enumerate((q_ref, v_ref, gate_ref, zf_ref, zb_ref)):
            if first:
                for p in range(per_mat):
                    stream.take(n * per_mat + p)
            y = _dot(h, w_bf[:, n * D:(n + 1) * D])
            if o_ref is q_ref or o_ref is gate_ref:
                y = _silu(y)
            o_ref[...] = y.astype(o_ref.dtype)

    _first_step_or_not(body)


def _rec_proj(x, mod_l, g, w_in, j):
    tile = pl.BlockSpec((TM, D), lambda i: (i, 0))
    kern = lambda *refs: _rec_proj_kernel(*refs, j=j)
    return pl.pallas_call(
        kern,
        grid=(N_TOK // TM,),
        in_specs=[tile, _const_spec((MOD_ROWS, N_MOD * D)), _const_spec((1, D)), HBM_SPEC],
        out_specs=[tile] * 5,
        out_shape=[jax.ShapeDtypeStruct((N_TOK, D), BF16)] * 3 + [jax.ShapeDtypeStruct((N_TOK, D), F32)] * 2,
        scratch_shapes=[pltpu.VMEM((D, 5 * D), BF16)] + _stream_scratch((D, W_PIECE), W_DEPTH),
        compiler_params=_cparams(1),
        name="rec_proj",
    )(x, mod_l, g, w_in)


def _gla_tables():
    t = np.arange(GLA_GROUP)[:, None]
    s = np.arange(GLA_GROUP)[None, :]
    ids = np.full((GLA_GROUP, GLA_GROUP), -1, np.int32)
    for l, b in reversed(list(enumerate(GLA_LEVELS, 1))):
        ids = np.where((t // (2 * b) == s // (2 * b)) & (s < t), l, ids)
    ids = np.where((t // GLA_BLK == s // GLA_BLK) & (s <= t), 0, ids)
    ids = np.stack([ids, ids.T]).astype(np.int32)
    e = np.zeros((GLA_BLK * HD, GLA_GROUP), np.float32)
    for j in range(GLA_BLK):
        e[j * HD:(j + 1) * HD, j::GLA_BLK] = 1.0
    return jnp.asarray(ids), jnp.asarray(e)


def _bcast_row(x, blk, idx):
    n, w = x.shape
    r = x.reshape(n // blk, blk, w)[:, idx:idx + 1, :]
    return jnp.broadcast_to(r, (n // blk, blk, w)).reshape(n, w)


def _gla_group(q, v, z, lb, ids, e_mat, st, backward):
    n = GLA_GROUP
    sig = 0.5 * jnp.tanh(0.5 * z) + 0.5
    f = lb + (1.0 - lb) * sig
    k = (1.0 - lb) * (1.0 - sig)
    lf = jnp.log(f) * LOG2E
    row = lax.broadcasted_iota(jnp.int32, (n, n), 0)
    col = lax.broadcasted_iota(jnp.int32, (n, n), 1)
    tri = jnp.where((col >= row) if backward else (col <= row), 1.0, 0.0).astype(BF16)
    c2 = _dot(tri, jnp.concatenate(_split2(lf), axis=1))
    cum = c2[:, 0:HD] + c2[:, HD:2 * HD]
    ck = jnp.log(k) * LOG2E - cum
    yield

    prods = []
    for j in range(GLA_BLK):
        prods.append((q * jnp.exp2(jnp.minimum(cum + _bcast_row(ck, GLA_BLK, j), 0.0))).astype(BF16))
        yield
    half = n // 2
    ids_q = ids[0:half, 0:half]
    diag = lambda x, h: x[h * half:(h + 1) * half, h * half:(h + 1) * half]
    pair = _dot(jnp.concatenate(prods, axis=1), e_mat)
    quads = [jnp.where(ids_q == 0, diag(pair, h), 0.0) for h in range(2)]
    yield

    for l, b in enumerate(GLA_LEVELS[:-1], 1):
        edge = _bcast_row(cum, 2 * b, b if backward else b - 1)
        ql = (q * jnp.exp2(cum - edge)).astype(BF16)
        kl = jnp.exp2(edge + ck).astype(BF16)
        x = _dot_nt(ql, kl)
        quads = [jnp.where(ids_q == l, diag(x, h), quads[h]) for h in range(2)]
        yield

    assert GLA_LEVELS[-1] == half
    lo, hi = slice(0, half), slice(half, n)
    qs, ks = (lo, hi) if backward else (hi, lo)
    edge = cum[half:half + 1, :] if backward else cum[half - 1:half, :]
    cross = _dot_nt((q[qs] * jnp.exp2(cum[qs] - edge)).astype(BF16),
                    jnp.exp2(edge + ck[ks]).astype(BF16)).astype(BF16)
    o_halves = [_dot(quads[h].astype(BF16), v[h * half:(h + 1) * half]) for h in range(2)]
    o_halves[0 if backward else 1] += _dot(cross, v[ks])
    o = jnp.concatenate(o_halves, axis=0)
    yield

    total = cum[0:1, :] if backward else cum[n - 1:n, :]
    if st is not None:
        o = o + _dot_nt((q * jnp.exp2(cum)).astype(BF16), st.astype(BF16))
    kdec = jnp.exp2(total + ck).astype(BF16)
    upd = _dot_tn(v, kdec)
    st_new = upd if st is None else st * jnp.exp2(total) + upd
    return o, st_new


def _lower_bound(x, layer):
    rows = [x[l:l + 1, :] for l in range(DEPTH)]
    mx = rows[0]
    for r in rows[1:]:
        mx = jnp.maximum(mx, r)
    ex = [jnp.exp(r - mx) for r in rows]
    tot = ex[0]
    for e in ex[1:]:
        tot = tot + e
    acc = ex[0] * 0.0
    for l in range(1, layer + 1):
        acc = acc + ex[l]
    return acc / tot


def _gla_kernel(*refs, n_groups, layer, has_state):
    if has_state:
        (q_ref, v_ref, gate_ref, zf_ref, zb_ref, lbl_ref, g_ref, ids_ref, e_ref, s0_ref,
         o_ref, acc_ref) = refs
    else:
        (q_ref, v_ref, gate_ref, zf_ref, zb_ref, lbl_ref, g_ref, ids_ref, e_ref,
         o_ref, sout_ref, acc_ref) = refs
    e_mat = e_ref[...]
    z_refs = (zf_ref, zb_ref)
    chains = [(hh, d) for hh in range(GLA_HP) for d in range(2)]
    cols = [slice(hh * HD, (hh + 1) * HD) for hh in range(GLA_HP)]
    lbs = {(hh, d): _lower_bound(lbl_ref[d, :, cols[hh]], layer) for hh, d in chains}

    def step(gi, sts):
        rows, gens = [], []
        for d in range(2):
            g = gi if d == 0 else n_groups - 1 - gi
            start = g * GLA_GROUP
            rows.append(pl.ds(start if isinstance(start, int) else pl.multiple_of(start, GLA_GROUP), GLA_GROUP))
        for c, (hh, d) in enumerate(chains):
            gens.append(_gla_group(q_ref[rows[d], cols[hh]].astype(F32), v_ref[rows[d], cols[hh]],
                                   z_refs[d][rows[d], cols[hh]], lbs[hh, d], ids_ref[d], e_mat, sts[c],
                                   backward=(d == 1)))
        new = []
        for (hh, d), (o, st) in zip(chains, _interleave(gens)):
            acc_ref[rows[d], cols[hh]] = acc_ref[rows[d], cols[hh]] + o
            new.append(st)
        return tuple(new)

    acc_ref[...] = jnp.zeros_like(acc_ref)
    if has_state:
        init = tuple(s0_ref[d, hh].T for hh, d in chains)
        lax.fori_loop(0, n_groups, step, init)
    else:
        sts = step(0, (None,) * len(chains))
        for (hh, d), st in zip(chains, sts):
            sout_ref[d, hh] = st.T
    for hh in range(GLA_HP):
        o = acc_ref[:, cols[hh]]
        y = o * lax.rsqrt(jnp.mean(o * o, axis=-1, keepdims=True) + EPS) * g_ref[...]
        o_ref[:, cols[hh]] = (y * gate_ref[:, cols[hh]].astype(F32)).astype(BF16)


def _gla(q, v, gate, zf, zb, lb_logits, g_out, tabs, *, row0, nb, ln, layer, s0=None):
    ids, e_mat = tabs
    assert ln % GLA_GROUP == 0 and (s0 is not None or ln == GLA_GROUP)
    b0 = row0 // ln
    w = GLA_HP * HD
    seq = pl.BlockSpec((ln, w), lambda b, h: (b + b0, h))
    in_specs = [seq] * 5 + [
        pl.BlockSpec((2, DEPTH, w), lambda b, h: (0, 0, h)),
        pl.BlockSpec((1, HD), lambda b, h: (0, 0)),
        pl.BlockSpec((2, GLA_GROUP, GLA_GROUP), lambda b, h: (0, 0, 0)),
        pl.BlockSpec((GLA_BLK * HD, GLA_GROUP), lambda b, h: (0, 0)),
    ]
    args = [q, v, gate, zf, zb, lb_logits, g_out, ids, e_mat]
    out_specs = [pl.BlockSpec((ln, w), lambda b, h: (b, h))]
    out_shape = [jax.ShapeDtypeStruct((nb * ln, D), BF16)]
    state_spec = pl.BlockSpec((None, 2, GLA_HP, HD, HD), lambda b, h: (b, 0, h, 0, 0))
    if s0 is not None:
        in_specs.append(state_spec)
        args.append(s0)
    else:
        out_specs.append(state_spec)
        out_shape.append(jax.ShapeDtypeStruct((nb, 2, HEADS, HD, HD), F32))
    kern = lambda *refs: _gla_kernel(*refs, n_groups=ln // GLA_GROUP, layer=layer, has_state=s0 is not None)
    return pl.pallas_call(
        kern,
        grid=(nb, HEADS // GLA_HP),
        in_specs=in_specs,
        out_specs=out_specs,
        out_shape=out_shape,
        scratch_shapes=[pltpu.VMEM((ln, w), F32)],
        compiler_params=_cparams(2),
        name="gla_sample" if s0 is not None else "gla_prompt",
    )(*args)


def _dft_tables(n):
    c = np.arange(FOUR_DG)
    ang_c = 2.0 * np.pi * ((c[:, None] * c[None, :]) % FOUR_DG) / FOUR_DG
    cs = np.concatenate([np.cos(ang_c), np.sin(ang_c)], axis=1)
    t = np.arange(n)
    ang_n = 2.0 * np.pi * ((t[:, None] * t[None, :]) % n) / n
    cn = np.concatenate([np.cos(ang_n), -np.sin(ang_n)], axis=1) / math.sqrt(n * FOUR_DG)
    return jnp.asarray(cs, F32).astype(BF16), jnp.asarray(cn, F32).astype(BF16)


def _fourier_kernel(x_ref, mod_ref, g_ref, cs_ref, cn_ref, o_ref, *, row_of):
    m = mod_ref[pl.ds(row_of(pl.program_id(0)), 1), :]
    h = _adanorm(x_ref[...], g_ref[...], m[:, 0:D], m[:, D:2 * D]).astype(BF16)
    cs = cs_ref[...]
    cn = cn_ref[...]
    for g in range(FOUR_GROUPS):
        xcs = _dot(h[:, g * FOUR_DG:(g + 1) * FOUR_DG], cs)
        stacked = jnp.concatenate([xcs[:, 0:FOUR_DG], xcs[:, FOUR_DG:]], axis=0).astype(BF16)
        o_ref[:, g * FOUR_DG:(g + 1) * FOUR_DG] = _dot(cn, stacked).astype(BF16)


def _fourier(x, mod_l, g, *, row0, nb, ln, row_of):
    cs, cn = _dft_tables(ln)
    b0 = row0 // ln
    kern = lambda *refs: _fourier_kernel(*refs, row_of=row_of)
    return pl.pallas_call(
        kern,
        grid=(nb,),
        in_specs=[pl.BlockSpec((ln, D), lambda b: (b + b0, 0)), _const_spec((MOD_ROWS, N_MOD * D)),
                  _const_spec((1, D)), _const_spec((FOUR_DG, 2 * FOUR_DG)), _const_spec((ln, 2 * ln))],
        out_specs=pl.BlockSpec((ln, D), lambda b: (b, 0)),
        out_shape=jax.ShapeDtypeStruct((nb * ln, D), BF16),
        compiler_params=_cparams(1),
        name="fourier_%d" % ln,
    )(x, mod_l, g, cs, cn)


FF_DEPTH_COLS = 4
FF_DEPTH_ROWS = 2


def _post_ffn_kernel(*refs, layer, wo_index, split_x, final):
    refs = list(refs)
    x_refs = [refs.pop(0) for _ in range(2 if split_x else 1)]
    op_ref, os_ref, mod_ref, wo_hbm, g_ref, win_hbm, wout_hbm = refs[:7]
    refs = refs[7:]
    gf_ref = refs.pop(0) if final else None
    out_refs = [refs.pop(0) for _ in range(2 if final else 1)]
    wo_bf, win_bf, wout_bf, stage_c, sem_c, stage_r, sem_r = refs
    n_chunks = D_FF // FF_CHUNK
    wo_jobs = _column_jobs(wo_hbm, wo_index, wo_bf, FF_CHUNK)
    in_jobs = []
    for c in range(n_chunks):
        for c0 in (c * FF_CHUNK, D_FF + c * FF_CHUNK):
            in_jobs.append((win_hbm.at[layer, :, pl.ds(c0, FF_CHUNK)], win_bf.at[:, pl.ds(c0, FF_CHUNK)]))
    cols = _WeightStream(wo_jobs + in_jobs, stage_c, sem_c)
    rows = _WeightStream([(wout_hbm.at[layer, pl.ds(c * FF_CHUNK, FF_CHUNK), :],
                           wout_bf.at[pl.ds(c * FF_CHUNK, FF_CHUNK), :]) for c in range(n_chunks)],
                         stage_r, sem_r)

    def body(first):
        if first:
            cols.prime()
            rows.prime()
        i = pl.program_id(0)
        ctx = i < NT_P
        m = mod_ref[pl.ds(_mod_row(i), 1), :]
        x = jnp.where(ctx, x_refs[0][...], x_refs[1][...]) if split_x else x_refs[0][...]
        o = jnp.where(ctx, op_ref[...], os_ref[...])
        if first:
            for k in range(len(wo_jobs)):
                cols.take(k)
        x = x + m[:, 2 * D:3 * D] * _dot(o, wo_bf[...])
        h = _adanorm(x, g_ref[...], m[:, 3 * D:4 * D], m[:, 4 * D:5 * D]).astype(BF16)
        acc = None
        for c in range(n_chunks):
            if first:
                cols.take(len(wo_jobs) + 2 * c)
                cols.take(len(wo_jobs) + 2 * c + 1)
                rows.take(c)
            gt = _dot(h, win_bf[:, c * FF_CHUNK:(c + 1) * FF_CHUNK])
            up = _dot(h, win_bf[:, D_FF + c * FF_CHUNK:D_FF + (c + 1) * FF_CHUNK])
            part = _dot((_silu(gt) * up).astype(BF16), wout_bf[c * FF_CHUNK:(c + 1) * FF_CHUNK, :])
            acc = part if acc is None else acc + part
        x = x + m[:, 5 * D:6 * D] * acc
        if not final:
            out_refs[0][...] = x
            return
        y = x * lax.rsqrt(jnp.mean(x * x, axis=-1, keepdims=True) + EPS) * gf_ref[...]
        if first:
            out_refs[0][...] = y
            return

        @pl.when(ctx)
        def _():
            out_refs[0][...] = y

        @pl.when(jnp.logical_not(ctx))
        def _():
            out_refs[1][...] = y

    _first_step_or_not(body)


def _post_ffn(xs, op, os_, mod_l, w_o, wo_index, g, w_in, w_out, layer, g_final=None):
    tile = pl.BlockSpec((TM, D), lambda i: (i, 0))
    p_tile = pl.BlockSpec((TM, D), lambda i: (jnp.minimum(i, NT_P - 1), 0))
    s_tile = pl.BlockSpec((TM, D), lambda i: (jnp.maximum(i - NT_P, 0), 0))
    split_x = len(xs) == 2
    final = g_final is not None
    in_specs = ([p_tile, s_tile] if split_x else [tile]) + [
        p_tile, s_tile, _const_spec((MOD_ROWS, N_MOD * D)), HBM_SPEC, _const_spec((1, D)), HBM_SPEC, HBM_SPEC]
    args = list(xs) + [op, os_, mod_l, w_o, g, w_in, w_out]
    if final:
        in_specs.append(_const_spec((1, D)))
        args.append(g_final)
        out_specs = [p_tile, s_tile]
        out_shape = [jax.ShapeDtypeStruct((N_P, D), F32), jax.ShapeDtypeStruct((N_S, D), F32)]
    else:
        out_specs = tile
        out_shape = jax.ShapeDtypeStruct((N_TOK, D), F32)
    kern = lambda *refs: _post_ffn_kernel(*refs, layer=layer, wo_index=wo_index, split_x=split_x, final=final)
    return pl.pallas_call(
        kern,
        grid=(N_TOK // TM,),
        in_specs=in_specs,
        out_specs=out_specs,
        out_shape=out_shape,
        scratch_shapes=([pltpu.VMEM((D, D), BF16), pltpu.VMEM((D, 2 * D_FF), BF16), pltpu.VMEM((D_FF, D), BF16)]
                        + _stream_scratch((D, FF_CHUNK), FF_DEPTH_COLS)
                        + _stream_scratch((FF_CHUNK, D), FF_DEPTH_ROWS)),
        compiler_params=_cparams(1),
        name="post_ffn",
    )(*args)


def kernel(x_prompt, x_sample, cache_attn_k, cache_attn_v, state_hgrn, c, c_ctx, w_ada, b_ada, g_norm_mix,
           g_norm_ffn, w_qkv_attn, lam_attn, g_subln_attn, w_o_attn, w_in_rec, lb_logits_rec, g_out_rec,
           w_o_rec, w_four, w_ffn_in, w_ffn_out, g_final):
    xs = (x_prompt.reshape(N_P, D), x_sample.reshape(N_S, D))
    cond = jnp.concatenate([c_ctx.reshape(1, D), c, jnp.zeros((MOD_ROWS - 1 - DEC_BATCH, D), F32)], axis=0)
    mod = _modulation(cond, w_ada, b_ada)
    cache_kt = jnp.transpose(cache_attn_k, (0, 1, 3, 4, 5, 2)).reshape(DEC_BATCH, -1, HEADS, HD, PAST)
    cache_v = cache_attn_v.reshape(DEC_BATCH, -1, PAST, D)
    rope_tabs = _rope_tables()
    gla_tabs = _gla_tables()
    gla_tabs = (gla_tabs[0], gla_tabs[1].astype(BF16))
    kv_ctx, new_s = None, []
    for i in range(DEPTH):
        kind, j = i % N_MIXERS, i // N_MIXERS
        g_mix = g_norm_mix[i].reshape(1, D)
        x_p = xs[0]
        x_s, s_tile0 = (xs[1], 0) if len(xs) == 2 else (xs[0], NT_P)
        if kind == 0:
            lam_init = 0.8 - 0.6 * math.exp(-0.3 * i)
            g_sub = g_subln_attn[j].reshape(1, HD)
            qs, ks, vs = _qkv_sample(x_s, s_tile0, mod[i], g_mix, w_qkv_attn, j, rope_tabs)
            os_ = _attn_sample(qs, ks, vs, cache_kt, cache_v, j, lam_attn[j], g_sub, lam_init)
            kt_all, v_all, op = _attn_prompt(x_p, mod[i], g_mix, w_qkv_attn, j, lam_attn[j], g_sub, lam_init,
                                             prev=kv_ctx)
            kv_ctx = (kt_all, v_all)
            w_o = w_o_attn
        elif kind == 1:
            proj = _rec_proj(xs[0], mod[i], g_mix, w_in_rec, j)
            g_out = g_out_rec[j].reshape(1, HD)
            op, s_ctx = _gla(*proj, lb_logits_rec, g_out, gla_tabs, row0=0, nb=BATCH, ln=SEQ, layer=i)
            new_s.append(s_ctx)
            os_ = _gla(*proj, lb_logits_rec, g_out, gla_tabs, row0=N_P, nb=DEC_BATCH, ln=DEC_SEQ, layer=i,
                       s0=state_hgrn[:, j])[0]
            w_o = w_o_rec
        else:
            op = _fourier(xs[0], mod[i], g_mix, row0=0, nb=BATCH, ln=SEQ, row_of=lambda b: 0)
            os_ = _fourier(xs[0], mod[i], g_mix, row0=N_P, nb=DEC_BATCH, ln=DEC_SEQ, row_of=lambda b: 1 + b)
            w_o = w_four
        out = _post_ffn(xs, op, os_, mod[i], w_o, j, g_norm_ffn[i].reshape(1, D), w_ffn_in, w_ffn_out, i,
                        g_final.reshape(1, D) if i == DEPTH - 1 else None)
        xs = tuple(out) if i == DEPTH - 1 else (out,)
    y_prompt = xs[0].reshape(BATCH, SEQ, D)
    y_sample = xs[1].reshape(DEC_BATCH, DEC_SEQ, D)
    new_k = jnp.transpose(kv_ctx[0].reshape(BATCH, -1, HEADS, 2, DH, SEQ), (0, 1, 5, 2, 3, 4))
    new_v = kv_ctx[1].reshape(BATCH, -1, SEQ, HEADS, HD)
    return (y_prompt, y_sample, new_k, new_v, jnp.stack(new_s, axis=1))
```

```python
import math

import numpy as np
import jax
import jax.numpy as jnp
from jax import lax
from jax.experimental import pallas as pl
from jax.experimental.pallas import tpu as pltpu

F32 = jnp.float32
BF16 = jnp.bfloat16

D = 1024
BATCH = 16
SEQ = 256
DEPTH = 4
DEC_BATCH = 4
DEC_SEQ = 1024
PAST = 512
GRID_W = 64
N_MIXERS = 3
HEADS = 8
DH = 64
HD = 128
ROPE_THETA = 10000.0
D_FF = 2816
N_MOD = 6
EPS = 1e-6
LOG2E = 1.0 / math.log(2.0)
FOUR_GROUPS = 4
FOUR_DG = 256

N_P = BATCH * SEQ
N_S = DEC_BATCH * DEC_SEQ
N_TOK = N_P + N_S
TM = 512
NT_P = N_P // TM
NT_S = N_S // TM
TILES_PER_DEC_SEQ = DEC_SEQ // TM
SEQ_PER_TILE = TM // SEQ
MOD_ROWS = 8
FF_CHUNK = 256
ATT_TQ = 512
ATT_HP = 2
ATT_P_WIDTH = 2
ATT_SEQ_SKEW = 5
ATT_SKEW = 1
GLA_HP = 8
GLA_GROUP = 256
GLA_BLK = 8
GLA_LEVELS = (8, 16, 32, 64, 128)
VMEM_LIMIT = 60 * 1024 * 1024


def _cparams(n_axes):
    return pltpu.CompilerParams(dimension_semantics=("arbitrary",) * n_axes,
                                vmem_limit_bytes=VMEM_LIMIT)


def _const_spec(shape):
    nd = len(shape)
    return pl.BlockSpec(shape, lambda *_: (0,) * nd, pipeline_mode=pl.Buffered(1))


def _sigmoid(x):
    return 1.0 / (1.0 + jnp.exp(-x))


def _silu(x):
    return x * _sigmoid(x)


def _adanorm(x, g, shift, scale):
    ms = jnp.mean(x * x, axis=-1, keepdims=True)
    return x * lax.rsqrt(ms + EPS) * (g * (1.0 + scale)) + shift


def _mod_row(tile):
    return jnp.where(tile < NT_P, 0, 1 + (tile - NT_P) // TILES_PER_DEC_SEQ)


def _dot(a, b):
    return jnp.dot(a, b, preferred_element_type=F32)


def _dot_nt(a, b):
    return lax.dot_general(a, b, (((1,), (1,)), ((), ())), preferred_element_type=F32)


def _dot_tn(a, b):
    return lax.dot_general(a, b, (((0,), (0,)), ((), ())), preferred_element_type=F32)


def _split2(x):
    hi = x.astype(BF16)
    return hi, (x - hi.astype(F32)).astype(BF16)


def _interleaved(generators, skew=0):
    results = [None] * len(generators)
    active = list(range(len(generators)))
    rnd = 0
    while active:
        for i in list(active):
            if rnd < i * skew:
                continue
            try:
                next(generators[i])
            except StopIteration as stop:
                results[i] = stop.value
                active.remove(i)
        rnd += 1
        yield
    return results


def _interleave(generators, skew=0):
    rounds = _interleaved(generators, skew)
    while True:
        try:
            next(rounds)
        except StopIteration as stop:
            return stop.value


class _WeightStream:
    def __init__(self, jobs, stage_ref, sem_ref):
        self.jobs, self.stage, self.sem = jobs, stage_ref, sem_ref
        self.depth = stage_ref.shape[0]

    def _copy(self, k):
        slot = k % self.depth
        return pltpu.make_async_copy(self.jobs[k][0], self.stage.at[slot], self.sem.at[slot])

    def prime(self):
        for k in range(min(self.depth, len(self.jobs))):
            self._copy(k).start()

    def take(self, k):
        self._copy(k).wait()
        self.jobs[k][1][...] = self.stage[k % self.depth].astype(BF16)
        if k + self.depth < len(self.jobs):
            self._copy(k + self.depth).start()


def _stream_scratch(piece_shape, depth):
    return [pltpu.VMEM((depth,) + tuple(piece_shape), F32), pltpu.SemaphoreType.DMA((depth,))]


def _column_jobs(w_hbm, index, w_bf, width):
    return [(w_hbm.at[index, :, pl.ds(c, width)], w_bf.at[:, pl.ds(c, width)])
            for c in range(0, w_bf.shape[1], width)]


def _first_step_or_not(body):
    first = pl.program_id(0) == 0
    pl.when(first)(lambda: body(True))
    pl.when(jnp.logical_not(first))(lambda: body(False))


HBM_SPEC = pl.BlockSpec(memory_space=pl.ANY)
W_PIECE = 512
W_DEPTH = 3


MOD_TK = 256


def _mod_kernel(cond_ref, w_ref, b_ref, o_ref):
    kk = pl.program_id(1)
    s = _silu(cond_ref[...])
    s_hi = s.astype(BF16)
    s_lo = (s - s_hi.astype(F32)).astype(BF16)
    w = w_ref[0]
    w_hi = w.astype(BF16)
    w_lo = (w - w_hi.astype(F32)).astype(BF16)
    r = _dot(jnp.concatenate([s_hi, s_lo], axis=0), w_hi)
    part = r[:MOD_ROWS] + r[MOD_ROWS:] + _dot(s_hi, w_lo)

    @pl.when(kk == 0)
    def _():
        o_ref[0] = part + b_ref[0]

    @pl.when(kk != 0)
    def _():
        o_ref[0] = o_ref[0] + part


def _modulation(cond, w_ada, b_ada):
    return pl.pallas_call(
        _mod_kernel,
        grid=(DEPTH, D // MOD_TK),
        in_specs=[
            pl.BlockSpec((MOD_ROWS, MOD_TK), lambda l, k: (0, k)),
            pl.BlockSpec((1, MOD_TK, N_MOD * D), lambda l, k: (l, k, 0)),
            pl.BlockSpec((1, 1, N_MOD * D), lambda l, k: (l, 0, 0)),
        ],
        out_specs=pl.BlockSpec((1, MOD_ROWS, N_MOD * D), lambda l, k: (l, 0, 0)),
        out_shape=jax.ShapeDtypeStruct((DEPTH, MOD_ROWS, N_MOD * D), F32),
        compiler_params=_cparams(2),
        name="modulation",
    )(cond, w_ada, b_ada.reshape(DEPTH, 1, N_MOD * D))


def _rope_tables():
    t = np.arange(DEC_SEQ)
    row, col = t // GRID_W, t % GRID_W
    lane = np.arange(HD) % DH
    use_row = lane < DH // 2
    idx = (lane % (DH // 2)) % (DH // 4)
    inv = ROPE_THETA ** (-idx.astype(np.float64) / (DH // 4))
    pos = np.where(use_row[None, :], row[:, None], col[:, None]).astype(np.float64)
    ang = pos * inv[None, :]
    first = (lane % (DH // 2)) < DH // 4
    cos = np.cos(ang)
    sin = np.sin(ang)
    sin_minus = np.where(first[None, :], -sin, 0.0)
    sin_plus = np.where(first[None, :], 0.0, sin)
    return (jnp.asarray(cos, F32), jnp.asarray(sin_minus, F32), jnp.asarray(sin_plus, F32))


def _rope_tile(t, cos, sin_minus, sin_plus):
    return (t * cos + pltpu.roll(t, HD - DH // 4, 1) * sin_minus
            + pltpu.roll(t, DH // 4, 1) * sin_plus)


def _lambda(lam_ref, lam_init):
    lp = lam_ref[...]
    return (jnp.exp(jnp.sum(lp[0:1] * lp[1:2], axis=-1, keepdims=True))
            - jnp.exp(jnp.sum(lp[2:3] * lp[3:4], axis=-1, keepdims=True)) + lam_init)


def _with_ones(v):
    return jnp.concatenate([v, jnp.ones_like(v)], axis=1)


def _diff_attn_head(q, score_fns, values, lam, g_scaled):
    tq = q.shape[0]
    lane = lax.broadcasted_iota(jnp.int32, q.shape, 1)
    zero = jnp.zeros_like(q)
    qq = jnp.concatenate([jnp.where(lane < DH, q, zero), jnp.where(lane >= DH, q, zero)], axis=0)
    scores = [fn(qq) for fn in score_fns]
    yield
    mx = scores[0].max(axis=-1, keepdims=True)
    for s in scores[1:]:
        mx = jnp.maximum(mx, s.max(axis=-1, keepdims=True))
    yield
    acc = None
    for s, v in zip(scores, values):
        c = _dot(jnp.exp2(s - mx).astype(BF16), v)
        acc = c if acc is None else acc + c
        yield
    r = acc[:, 0:HD] / acc[:, HD:2 * HD]
    o = r[:tq] - lam * r[tq:]
    o = o * lax.rsqrt(jnp.mean(o * o, axis=-1, keepdims=True) + EPS) * g_scaled
    return o.astype(BF16)


def _attn_prompt_kernel(*refs, j, n_layers, lam_init):
    x_ref, mod_ref, g_ref, w_hbm, lam_ref, gsub_ref = refs[:6]
    kt_ref, v_ref, o_ref, w_bf, stage, sem = refs[-6:]
    owns_all = j == 0
    stream = _WeightStream(_column_jobs(w_hbm, j, w_bf, W_PIECE), stage, sem)
    per_mat = D // W_PIECE

    def body(first):
        if first:
            stream.prime()
        m = mod_ref[pl.ds(0, 1), :]
        lam = _lambda(lam_ref, lam_init)
        g_scaled = gsub_ref[...] * (1.0 - lam_init)

        def sequence(s, loads):
            rows = slice(s * SEQ, (s + 1) * SEQ)
            h = _adanorm(x_ref[rows, :], g_ref[...], m[:, 0:D], m[:, D:2 * D]).astype(BF16)
            yield
            mats = []
            for n in range(3):
                if loads:
                    for p in range(per_mat):
                        stream.take(n * per_mat + p)
                mats.append(_dot(h, w_bf[:, n * D:(n + 1) * D]))
                yield
            q = (mats[0] * (DH ** -0.5 * LOG2E)).astype(BF16)
            k, v = mats[1], mats[2]
            kb, vb = k.astype(BF16), v.astype(BF16)
            if owns_all:
                for jj in range(n_layers):
                    kt_ref[s, jj] = k.T if jj == j else jnp.zeros((D, SEQ), F32)
                    v_ref[s, jj] = v if jj == j else jnp.zeros((SEQ, D), F32)
            else:
                kt_ref[s] = k.T
                v_ref[s] = v
            yield
            for h0 in range(0, HEADS, ATT_P_WIDTH):
                heads = []
                for hh in range(h0, h0 + ATT_P_WIDTH):
                    cols = slice(hh * HD, (hh + 1) * HD)
                    kh = kb[:, cols]
                    heads.append(_diff_attn_head(q[:, cols], [lambda qq, kh=kh: _dot_nt(qq, kh)],
                                                 [_with_ones(vb[:, cols])], lam, g_scaled))
                outs = yield from _interleaved(heads, ATT_SKEW)
                for hh, o in zip(range(h0, h0 + ATT_P_WIDTH), outs):
                    o_ref[rows, hh * HD:(hh + 1) * HD] = o

        _interleave([sequence(s, first and s == 0) for s in range(SEQ_PER_TILE)], skew=ATT_SEQ_SKEW)

    _first_step_or_not(body)


def _attn_prompt(x, mod_l, g, w_qkv, j, lam_p, g_sub, lam_init, prev=None):
    n_layers = w_qkv.shape[0]
    assert (prev is None) == (j == 0)
    tile = pl.BlockSpec((TM, D), lambda i: (i, 0))
    in_specs = [tile, _const_spec((MOD_ROWS, N_MOD * D)), _const_spec((1, D)), HBM_SPEC,
                _const_spec((4, DH)), _const_spec((1, HD))]
    args = [x, mod_l, g, w_qkv, lam_p, g_sub]
    if prev is None:
        kt_spec = pl.BlockSpec((SEQ_PER_TILE, n_layers, D, SEQ), lambda i: (i, 0, 0, 0))
        v_spec = pl.BlockSpec((SEQ_PER_TILE, n_layers, SEQ, D), lambda i: (i, 0, 0, 0))
        aliases = {}
    else:
        kt_spec = pl.BlockSpec((SEQ_PER_TILE, None, D, SEQ), lambda i: (i, j, 0, 0))
        v_spec = pl.BlockSpec((SEQ_PER_TILE, None, SEQ, D), lambda i: (i, j, 0, 0))
        aliases = {len(args): 0, len(args) + 1: 1}
        in_specs += [HBM_SPEC, HBM_SPEC]
        args += list(prev)
    kern = lambda *refs: _attn_prompt_kernel(*refs, j=j, n_layers=n_layers, lam_init=lam_init)
    return pl.pallas_call(
        kern,
        grid=(NT_P,),
        in_specs=in_specs,
        out_specs=[kt_spec, v_spec, tile],
        out_shape=[jax.ShapeDtypeStruct((BATCH, n_layers, D, SEQ), F32),
                   jax.ShapeDtypeStruct((BATCH, n_layers, SEQ, D), F32), jax.ShapeDtypeStruct((N_P, D), BF16)],
        input_output_aliases=aliases,
        scratch_shapes=[pltpu.VMEM((D, 3 * D), BF16)] + _stream_scratch((D, W_PIECE), W_DEPTH),
        compiler_params=_cparams(1),
        name="attn_prompt",
    )(*args)


def _qkv_sample_kernel(x_ref, mod_ref, g_ref, w_hbm, cos_ref, sm_ref, sp_ref, q_ref, k_ref, v_ref,
                       w_bf, stage, sem, *, j):
    stream = _WeightStream(_column_jobs(w_hbm, j, w_bf, W_PIECE), stage, sem)
    per_mat = D // W_PIECE

    def body(first):
        if first:
            stream.prime()
        i = pl.program_id(0)
        m = mod_ref[pl.ds(1 + i // TILES_PER_DEC_SEQ, 1), :]

        def half(rows, loads):
            h = _adanorm(x_ref[rows, :], g_ref[...], m[:, 0:D], m[:, D:2 * D]).astype(BF16)
            cos, sm, sp = cos_ref[rows, :], sm_ref[rows, :], sp_ref[rows, :]
            yield
            for n, (o_ref, scale) in enumerate(((q_ref, DH ** -0.5 * LOG2E), (k_ref, None), (v_ref, None))):
                if loads:
                    for p in range(per_mat):
                        stream.take(n * per_mat + p)
                y = _dot(h, w_bf[:, n * D:(n + 1) * D])
                if o_ref is v_ref:
                    o_ref[rows, :] = y.astype(BF16)
                else:
                    if scale is not None:
                        y = y * scale
                    for hh in range(HEADS):
                        t = _rope_tile(y[:, hh * HD:(hh + 1) * HD], cos, sm, sp)
                        o_ref[rows, hh * HD:(hh + 1) * HD] = t.astype(BF16)
                yield

        _interleave([half(slice(r, r + TM // 2), first and r == 0) for r in (0, TM // 2)], skew=1)

    _first_step_or_not(body)


def _qkv_sample(x, tile0, mod_l, g, w_qkv, j, tabs):
    tile = pl.BlockSpec((TM, D), lambda i: (i, 0))
    tab = pl.BlockSpec((TM, HD), lambda i: (i % TILES_PER_DEC_SEQ, 0))
    kern = lambda *refs: _qkv_sample_kernel(*refs, j=j)
    return pl.pallas_call(
        kern,
        grid=(NT_S,),
        in_specs=[pl.BlockSpec((TM, D), lambda i: (i + tile0, 0)), _const_spec((MOD_ROWS, N_MOD * D)),
                  _const_spec((1, D)), HBM_SPEC, tab, tab, tab],
        out_specs=[tile, tile, tile],
        out_shape=[jax.ShapeDtypeStruct((N_S, D), BF16)] * 3,
        scratch_shapes=[pltpu.VMEM((D, 3 * D), BF16)] + _stream_scratch((D, W_PIECE), W_DEPTH),
        compiler_params=_cparams(1),
        name="qkv_sample",
    )(x, mod_l, g, w_qkv, *tabs)


def _attn_sample_kernel(*refs, lam_init):
    q_ref, kn_ref, vn_ref, kct_ref, vc_ref, lam_ref, gsub_ref, o_ref = refs
    lam = _lambda(lam_ref, lam_init)
    g_scaled = gsub_ref[...] * (1.0 - lam_init)
    for h0 in range(0, HEADS, ATT_HP):
        heads = []
        for hh in range(h0, h0 + ATT_HP):
            cols = slice(hh * HD, (hh + 1) * HD)
            kn = kn_ref[:, cols]
            kct = kct_ref[hh].astype(BF16)
            vc = vc_ref[:, cols].astype(BF16)
            heads.append(_diff_attn_head(
                q_ref[:, cols], [lambda qq, kn=kn: _dot_nt(qq, kn), lambda qq, kct=kct: _dot(qq, kct)],
                [_with_ones(vn_ref[:, cols]), _with_ones(vc)], lam, g_scaled))
        for hh, o in zip(range(h0, h0 + ATT_HP), _interleave(heads, ATT_SKEW)):
            o_ref[:, hh * HD:(hh + 1) * HD] = o


def _attn_sample(q, kn, vn, cache_kt, cache_v, j, lam_p, g_sub, lam_init):
    nq = DEC_SEQ // ATT_TQ
    seq = pl.BlockSpec((DEC_SEQ, D), lambda b, i: (b, 0))
    blk = pl.BlockSpec((ATT_TQ, D), lambda b, i: (b * nq + i, 0))
    in_specs = [blk, seq, seq,
                pl.BlockSpec((None, None, HEADS, HD, PAST), lambda b, i: (b, j, 0, 0, 0)),
                pl.BlockSpec((None, None, PAST, D), lambda b, i: (b, j, 0, 0)),
                pl.BlockSpec((4, DH), lambda b, i: (0, 0)), pl.BlockSpec((1, HD), lambda b, i: (0, 0))]
    kern = lambda *refs: _attn_sample_kernel(*refs, lam_init=lam_init)
    return pl.pallas_call(
        kern,
        grid=(DEC_BATCH, nq),
        in_specs=in_specs,
        out_specs=blk,
        out_shape=jax.ShapeDtypeStruct((N_S, D), BF16),
        compiler_params=_cparams(2),
        name="attn_sample",
    )(q, kn, vn, cache_kt, cache_v, lam_p, g_sub)


def _rec_proj_kernel(x_ref, mod_ref, g_ref, w_hbm, q_ref, v_ref, gate_ref, zf_ref, zb_ref,
                     w_bf, stage, sem, *, j):
    stream = _WeightStream(_column_jobs(w_hbm, j, w_bf, W_PIECE), stage, sem)
    per_mat = D // W_PIECE

    def body(first):
        if first:
            stream.prime()
        m = mod_ref[pl.ds(_mod_row(pl.program_id(0)), 1), :]

        def half(rows, loads):
            h = _adanorm(x_ref[rows, :], g_ref[...], m[:, 0:D], m[:, D:2 * D]).astype(BF16)
            yield
            for n, o_ref in enumerate((q_ref, v_ref, gate_ref, zf_ref, zb_ref)):
                if loads:
                    for p in range(per_mat):
                        stream.take(n * per_mat + p)
                y = _dot(h, w_bf[:, n * D:(n + 1) * D])
                if o_ref is q_ref or o_ref is gate_ref:
                    y = _silu(y)
                o_ref[rows, :] = y.astype(o_ref.dtype)
                yield

        _interleave([half(slice(r, r + TM // 2), first and r == 0) for r in (0, TM // 2)], skew=1)

    _first_step_or_not(body)


def _rec_proj(x, mod_l, g, w_in, j):
    tile = pl.BlockSpec((TM, D), lambda i: (i, 0))
    kern = lambda *refs: _rec_proj_kernel(*refs, j=j)
    return pl.pallas_call(
        kern,
        grid=(N_TOK // TM,),
        in_specs=[tile, _const_spec((MOD_ROWS, N_MOD * D)), _const_spec((1, D)), HBM_SPEC],
        out_specs=[tile] * 5,
        out_shape=[jax.ShapeDtypeStruct((N_TOK, D), BF16)] * 3 + [jax.ShapeDtypeStruct((N_TOK, D), F32)] * 2,
        scratch_shapes=[pltpu.VMEM((D, 5 * D), BF16)] + _stream_scratch((D, W_PIECE), W_DEPTH),
        compiler_params=_cparams(1),
        name="rec_proj",
    )(x, mod_l, g, w_in)


def _gla_tables():
    t = np.arange(GLA_GROUP)[:, None]
    s = np.arange(GLA_GROUP)[None, :]
    ids = np.full((GLA_GROUP, GLA_GROUP), -1, np.int32)
    for l, b in reversed(list(enumerate(GLA_LEVELS, 1))):
        ids = np.where((t // (2 * b) == s // (2 * b)) & (s < t), l, ids)
    ids = np.where((t // GLA_BLK == s // GLA_BLK) & (s <= t), 0, ids)
    ids = np.stack([ids, ids.T]).astype(np.int32)
    e = np.zeros((GLA_BLK * HD, GLA_GROUP), np.float32)
    for j in range(GLA_BLK):
        e[j * HD:(j + 1) * HD, j::GLA_BLK] = 1.0
    return jnp.asarray(ids), jnp.asarray(e)


def _bcast_row(x, blk, idx):
    n, w = x.shape
    r = x.reshape(n // blk, blk, w)[:, idx:idx + 1, :]
    return jnp.broadcast_to(r, (n // blk, blk, w)).reshape(n, w)


def _gla_group(q, v, z, lb, ids, e_mat, st, backward):
    n = GLA_GROUP
    sig = 0.5 * jnp.tanh(0.5 * z) + 0.5
    f = lb + (1.0 - lb) * sig
    k = (1.0 - lb) * (1.0 - sig)
    lf = jnp.log(f) * LOG2E
    row = lax.broadcasted_iota(jnp.int32, (n, n), 0)
    col = lax.broadcasted_iota(jnp.int32, (n, n), 1)
    tri = jnp.where((col >= row) if backward else (col <= row), 1.0, 0.0).astype(BF16)
    c2 = _dot(tri, jnp.concatenate(_split2(lf), axis=1))
    cum = c2[:, 0:HD] + c2[:, HD:2 * HD]
    ck = jnp.log(k) * LOG2E - cum
    yield

    prods = []
    for j in range(GLA_BLK):
        prods.append((q * jnp.exp2(jnp.minimum(cum + _bcast_row(ck, GLA_BLK, j), 0.0))).astype(BF16))
        yield
    half = n // 2
    ids_q = ids[0:half, 0:half]
    diag = lambda x, h: x[h * half:(h + 1) * half, h * half:(h + 1) * half]
    pair = _dot(jnp.concatenate(prods, axis=1), e_mat)
    quads = [jnp.where(ids_q == 0, diag(pair, h), 0.0) for h in range(2)]
    yield

    for l, b in enumerate(GLA_LEVELS[:-1], 1):
        edge = _bcast_row(cum, 2 * b, b if backward else b - 1)
        ql = (q * jnp.exp2(cum - edge)).astype(BF16)
        kl = jnp.exp2(edge + ck).astype(BF16)
        x = _dot_nt(ql, kl)
        quads = [jnp.where(ids_q == l, diag(x, h), quads[h]) for h in range(2)]
        yield

    assert GLA_LEVELS[-1] == half
    lo, hi = slice(0, half), slice(half, n)
    qs, ks = (lo, hi) if backward else (hi, lo)
    edge = cum[half:half + 1, :] if backward else cum[half - 1:half, :]
    cross = _dot_nt((q[qs] * jnp.exp2(cum[qs] - edge)).astype(BF16),
                    jnp.exp2(edge + ck[ks]).astype(BF16)).astype(BF16)
    o_halves = [_dot(quads[h].astype(BF16), v[h * half:(h + 1) * half]) for h in range(2)]
    o_halves[0 if backward else 1] += _dot(cross, v[ks])
    o = jnp.concatenate(o_halves, axis=0)
    yield

    total = cum[0:1, :] if backward else cum[n - 1:n, :]
    if st is not None:
        o = o + _dot_nt((q * jnp.exp2(cum)).astype(BF16), st.astype(BF16))
    kdec = jnp.exp2(total + ck).astype(BF16)
    upd = _dot_tn(v, kdec)
    st_new = upd if st is None else st * jnp.exp2(total) + upd
    return o, st_new


def _lower_bound(x, layer):
    rows = [x[l:l + 1, :] for l in range(DEPTH)]
    mx = rows[0]
    for r in rows[1:]:
        mx = jnp.maximum(mx, r)
    ex = [jnp.exp(r - mx) for r in rows]
    tot = ex[0]
    for e in ex[1:]:
        tot = tot + e
    acc = ex[0] * 0.0
    for l in range(1, layer + 1):
        acc = acc + ex[l]
    return acc / tot


def _gla_kernel(*refs, n_groups, layer, has_state):
    if has_state:
        (q_ref, v_ref, gate_ref, zf_ref, zb_ref, lbl_ref, g_ref, ids_ref, e_ref, s0_ref,
         o_ref, acc_ref) = refs
    else:
        (q_ref, v_ref, gate_ref, zf_ref, zb_ref, lbl_ref, g_ref, ids_ref, e_ref,
         o_ref, sout_ref, acc_ref) = refs
    e_mat = e_ref[...]
    z_refs = (zf_ref, zb_ref)
    chains = [(hh, d) for hh in range(GLA_HP) for d in range(2)]
    cols = [slice(hh * HD, (hh + 1) * HD) for hh in range(GLA_HP)]
    lbs = {(hh, d): _lower_bound(lbl_ref[d, :, cols[hh]], layer) for hh, d in chains}

    def step(gi, sts):
        rows, gens = [], []
        for d in range(2):
            g = gi if d == 0 else n_groups - 1 - gi
            start = g * GLA_GROUP
            rows.append(pl.ds(start if isinstance(start, int) else pl.multiple_of(start, GLA_GROUP), GLA_GROUP))
        for c, (hh, d) in enumerate(chains):
            gens.append(_gla_group(q_ref[rows[d], cols[hh]].astype(F32), v_ref[rows[d], cols[hh]],
                                   z_refs[d][rows[d], cols[hh]], lbs[hh, d], ids_ref[d], e_mat, sts[c],
                                   backward=(d == 1)))
        new = []
        for (hh, d), (o, st) in zip(chains, _interleave(gens)):
            acc_ref[rows[d], cols[hh]] = acc_ref[rows[d], cols[hh]] + o
            new.append(st)
        return tuple(new)

    acc_ref[...] = jnp.zeros_like(acc_ref)
    if has_state:
        init = tuple(s0_ref[d, hh].T for hh, d in chains)
        lax.fori_loop(0, n_groups, step, init)
    else:
        sts = step(0, (None,) * len(chains))
        for (hh, d), st in zip(chains, sts):
            sout_ref[d, hh] = st.T
    for hh in range(GLA_HP):
        o = acc_ref[:, cols[hh]]
        y = o * lax.rsqrt(jnp.mean(o * o, axis=-1, keepdims=True) + EPS) * g_ref[...]
        o_ref[:, cols[hh]] = (y * gate_ref[:, cols[hh]].astype(F32)).astype(BF16)


def _gla(q, v, gate, zf, zb, lb_logits, g_out, tabs, *, row0, nb, ln, layer, s0=None):
    ids, e_mat = tabs
    assert ln % GLA_GROUP == 0 and (s0 is not None or ln == GLA_GROUP)
    b0 = row0 // ln
    w = GLA_HP * HD
    seq = pl.BlockSpec((ln, w), lambda b, h: (b + b0, h))
    in_specs = [seq] * 5 + [
        pl.BlockSpec((2, DEPTH, w), lambda b, h: (0, 0, h)),
        pl.BlockSpec((1, HD), lambda b, h: (0, 0)),
        pl.BlockSpec((2, GLA_GROUP, GLA_GROUP), lambda b, h: (0, 0, 0)),
        pl.BlockSpec((GLA_BLK * HD, GLA_GROUP), lambda b, h: (0, 0)),
    ]
    args = [q, v, gate, zf, zb, lb_logits, g_out, ids, e_mat]
    out_specs = [pl.BlockSpec((ln, w), lambda b, h: (b, h))]
    out_shape = [jax.ShapeDtypeStruct((nb * ln, D), BF16)]
    state_spec = pl.BlockSpec((None, 2, GLA_HP, HD, HD), lambda b, h: (b, 0, h, 0, 0))
    if s0 is not None:
        in_specs.append(state_spec)
        args.append(s0)
    else:
        out_specs.append(state_spec)
        out_shape.append(jax.ShapeDtypeStruct((nb, 2, HEADS, HD, HD), F32))
    kern = lambda *refs: _gla_kernel(*refs, n_groups=ln // GLA_GROUP, layer=layer, has_state=s0 is not None)
    return pl.pallas_call(
        kern,
        grid=(nb, HEADS // GLA_HP),
        in_specs=in_specs,
        out_specs=out_specs,
        out_shape=out_shape,
        scratch_shapes=[pltpu.VMEM((ln, w), F32)],
        compiler_params=_cparams(2),
        name="gla_sample" if s0 is not None else "gla_prompt",
    )(*args)


def _dft_tables(n):
    c = np.arange(FOUR_DG)
    ang_c = 2.0 * np.pi * ((c[:, None] * c[None, :]) % FOUR_DG) / FOUR_DG
    cs = np.concatenate([np.cos(ang_c), np.sin(ang_c)], axis=1)
    t = np.arange(n)
    ang_n = 2.0 * np.pi * ((t[:, None] * t[None, :]) % n) / n
    cn = np.concatenate([np.cos(ang_n), -np.sin(ang_n)], axis=1) / math.sqrt(n * FOUR_DG)
    return jnp.asarray(cs, F32).astype(BF16), jnp.asarray(cn, F32).astype(BF16)


def _fourier_kernel(x_ref, mod_ref, g_ref, cs_ref, cn_ref, o_ref, *, row_of):
    m = mod_ref[pl.ds(row_of(pl.program_id(0)), 1), :]
    h = _adanorm(x_ref[...], g_ref[...], m[:, 0:D], m[:, D:2 * D]).astype(BF16)
    cs = cs_ref[...]
    cn = cn_ref[...]
    for g in range(FOUR_GROUPS):
        xcs = _dot(h[:, g * FOUR_DG:(g + 1) * FOUR_DG], cs)
        stacked = jnp.concatenate([xcs[:, 0:FOUR_DG], xcs[:, FOUR_DG:]], axis=0).astype(BF16)
        o_ref[:, g * FOUR_DG:(g + 1) * FOUR_DG] = _dot(cn, stacked).astype(BF16)


def _fourier(x, mod_l, g, *, row0, nb, ln, row_of):
    cs, cn = _dft_tables(ln)
    b0 = row0 // ln
    kern = lambda *refs: _fourier_kernel(*refs, row_of=row_of)
    return pl.pallas_call(
        kern,
        grid=(nb,),
        in_specs=[pl.BlockSpec((ln, D), lambda b: (b + b0, 0)), _const_spec((MOD_ROWS, N_MOD * D)),
                  _const_spec((1, D)), _const_spec((FOUR_DG, 2 * FOUR_DG)), _const_spec((ln, 2 * ln))],
        out_specs=pl.BlockSpec((ln, D), lambda b: (b, 0)),
        out_shape=jax.ShapeDtypeStruct((nb * ln, D), BF16),
        compiler_params=_cparams(1),
        name="fourier_%d" % ln,
    )(x, mod_l, g, cs, cn)


FF_DEPTH_COLS = 4
FF_DEPTH_ROWS = 2


def _post_ffn_kernel(*refs, layer, wo_index, split_x, final):
    refs = list(refs)
    x_refs = [refs.pop(0) for _ in range(2 if split_x else 1)]
    op_ref, os_ref, mod_ref, wo_hbm, g_ref, win_hbm, wout_hbm = refs[:7]
    refs = refs[7:]
    gf_ref = refs.pop(0) if final else None
    out_refs = [refs.pop(0) for _ in range(2 if final else 1)]
    wo_bf, win_bf, wout_bf, stage_c, sem_c, stage_r, sem_r = refs
    n_chunks = D_FF // FF_CHUNK
    wo_jobs = _column_jobs(wo_hbm, wo_index, wo_bf, FF_CHUNK)
    in_jobs = []
    for c in range(n_chunks):
        for c0 in (c * FF_CHUNK, D_FF + c * FF_CHUNK):
            in_jobs.append((win_hbm.at[layer, :, pl.ds(c0, FF_CHUNK)], win_bf.at[:, pl.ds(c0, FF_CHUNK)]))
    cols = _WeightStream(wo_jobs + in_jobs, stage_c, sem_c)
    rows = _WeightStream([(wout_hbm.at[layer, pl.ds(c * FF_CHUNK, FF_CHUNK), :],
                           wout_bf.at[pl.ds(c * FF_CHUNK, FF_CHUNK), :]) for c in range(n_chunks)],
                         stage_r, sem_r)

    def body(first):
        if first:
            cols.prime()
            rows.prime()
        i = pl.program_id(0)
        ctx = i < NT_P
        m = mod_ref[pl.ds(_mod_row(i), 1), :]
        x = jnp.where(ctx, x_refs[0][...], x_refs[1][...]) if split_x else x_refs[0][...]
        o = jnp.where(ctx, op_ref[...], os_ref[...])
        if first:
            for k in range(len(wo_jobs)):
                cols.take(k)
        x = x + m[:, 2 * D:3 * D] * _dot(o, wo_bf[...])
        h = _adanorm(x, g_ref[...], m[:, 3 * D:4 * D], m[:, 4 * D:5 * D]).astype(BF16)
        acc = None
        for c in range(n_chunks):
            if first:
                cols.take(len(wo_jobs) + 2 * c)
                cols.take(len(wo_jobs) + 2 * c + 1)
                rows.take(c)
            gt = _dot(h, win_bf[:, c * FF_CHUNK:(c + 1) * FF_CHUNK])
            up = _dot(h, win_bf[:, D_FF + c * FF_CHUNK:D_FF + (c + 1) * FF_CHUNK])
            part = _dot((_silu(gt) * up).astype(BF16), wout_bf[c * FF_CHUNK:(c + 1) * FF_CHUNK, :])
            acc = part if acc is None else acc + part
        x = x + m[:, 5 * D:6 * D] * acc
        if not final:
            out_refs[0][...] = x
            return
        y = x * lax.rsqrt(jnp.mean(x * x, axis=-1, keepdims=True) + EPS) * gf_ref[...]
        if first:
            out_refs[0][...] = y
            return

        @pl.when(ctx)
        def _():
            out_refs[0][...] = y

        @pl.when(jnp.logical_not(ctx))
        def _():
            out_refs[1][...] = y

    _first_step_or_not(body)


def _post_ffn(xs, op, os_, mod_l, w_o, wo_index, g, w_in, w_out, layer, g_final=None):
    tile = pl.BlockSpec((TM, D), lambda i: (i, 0))
    p_tile = pl.BlockSpec((TM, D), lambda i: (jnp.minimum(i, NT_P - 1), 0))
    s_tile = pl.BlockSpec((TM, D), lambda i: (jnp.maximum(i - NT_P, 0), 0))
    split_x = len(xs) == 2
    final = g_final is not None
    in_specs = ([p_tile, s_tile] if split_x else [tile]) + [
        p_tile, s_tile, _const_spec((MOD_ROWS, N_MOD * D)), HBM_SPEC, _const_spec((1, D)), HBM_SPEC, HBM_SPEC]
    args = list(xs) + [op, os_, mod_l, w_o, g, w_in, w_out]
    if final:
        in_specs.append(_const_spec((1, D)))
        args.append(g_final)
        out_specs = [p_tile, s_tile]
        out_shape = [jax.ShapeDtypeStruct((N_P, D), F32), jax.ShapeDtypeStruct((N_S, D), F32)]
    else:
        out_specs = tile
        out_shape = jax.ShapeDtypeStruct((N_TOK, D), F32)
    kern = lambda *refs: _post_ffn_kernel(*refs, layer=layer, wo_index=wo_index, split_x=split_x, final=final)
    return pl.pallas_call(
        kern,
        grid=(N_TOK // TM,),
        in_specs=in_specs,
        out_specs=out_specs,
        out_shape=out_shape,
        scratch_shapes=([pltpu.VMEM((D, D), BF16), pltpu.VMEM((D, 2 * D_FF), BF16), pltpu.VMEM((D_FF, D), BF16)]
                        + _stream_scratch((D, FF_CHUNK), FF_DEPTH_COLS)
                        + _stream_scratch((FF_CHUNK, D), FF_DEPTH_ROWS)),
        compiler_params=_cparams(1),
        name="post_ffn",
    )(*args)


def kernel(x_prompt, x_sample, cache_attn_k, cache_attn_v, state_hgrn, c, c_ctx, w_ada, b_ada, g_norm_mix,
           g_norm_ffn, w_qkv_attn, lam_attn, g_subln_attn, w_o_attn, w_in_rec, lb_logits_rec, g_out_rec,
           w_o_rec, w_four, w_ffn_in, w_ffn_out, g_final):
    xs = (x_prompt.reshape(N_P, D), x_sample.reshape(N_S, D))
    cond = jnp.concatenate([c_ctx.reshape(1, D), c, jnp.zeros((MOD_ROWS - 1 - DEC_BATCH, D), F32)], axis=0)
    mod = _modulation(cond, w_ada, b_ada)
    cache_kt = jnp.transpose(cache_attn_k, (0, 1, 3, 4, 5, 2)).reshape(DEC_BATCH, -1, HEADS, HD, PAST)
    cache_v = cache_attn_v.reshape(DEC_BATCH, -1, PAST, D)
    rope_tabs = _rope_tables()
    gla_tabs = _gla_tables()
    gla_tabs = (gla_tabs[0], gla_tabs[1].astype(BF16))
    kv_ctx, new_s = None, []
    for i in range(DEPTH):
        kind, j = i % N_MIXERS, i // N_MIXERS
        g_mix = g_norm_mix[i].reshape(1, D)
        x_p = xs[0]
        x_s, s_tile0 = (xs[1], 0) if len(xs) == 2 else (xs[0], NT_P)
        if kind == 0:
            lam_init = 0.8 - 0.6 * math.exp(-0.3 * i)
            g_sub = g_subln_attn[j].reshape(1, HD)
            qs, ks, vs = _qkv_sample(x_s, s_tile0, mod[i], g_mix, w_qkv_attn, j, rope_tabs)
            os_ = _attn_sample(qs, ks, vs, cache_kt, cache_v, j, lam_attn[j], g_sub, lam_init)
            kt_all, v_all, op = _attn_prompt(x_p, mod[i], g_mix, w_qkv_attn, j, lam_attn[j], g_sub, lam_init,
                                             prev=kv_ctx)
            kv_ctx = (kt_all, v_all)
            w_o = w_o_attn
        elif kind == 1:
            proj = _rec_proj(xs[0], mod[i], g_mix, w_in_rec, j)
            g_out = g_out_rec[j].reshape(1, HD)
            op, s_ctx = _gla(*proj, lb_logits_rec, g_out, gla_tabs, row0=0, nb=BATCH, ln=SEQ, layer=i)
            new_s.append(s_ctx)
            os_ = _gla(*proj, lb_logits_rec, g_out, gla_tabs, row0=N_P, nb=DEC_BATCH, ln=DEC_SEQ, layer=i,
                       s0=state_hgrn[:, j])[0]
            w_o = w_o_rec
        else:
            op = _fourier(xs[0], mod[i], g_mix, row0=0, nb=BATCH, ln=SEQ, row_of=lambda b: 0)
            os_ = _fourier(xs[0], mod[i], g_mix, row0=N_P, nb=DEC_BATCH, ln=DEC_SEQ, row_of=lambda b: 1 + b)
            w_o = w_four
        out = _post_ffn(xs, op, os_, mod[i], w_o, j, g_norm_ffn[i].reshape(1, D), w_ffn_in, w_ffn_out, i,
                        g_final.reshape(1, D) if i == DEPTH - 1 else None)
        xs = tuple(out) if i == DEPTH - 1 else (out,)
    y_prompt = xs[0].reshape(BATCH, SEQ, D)
    y_sample = xs[1].reshape(DEC_BATCH, DEC_SEQ, D)
    new_k = jnp.transpose(kv_ctx[0].reshape(BATCH, -1, HEADS, 2, DH, SEQ), (0, 1, 5, 2, 3, 4))
    new_v = kv_ctx[1].reshape(BATCH, -1, SEQ, HEADS, HD)
    return (y_prompt, y_sample, new_k, new_v, jnp.stack(new_s, axis=1))
```

```python
import math

import numpy as np
import jax
import jax.numpy as jnp
from jax import lax
from jax.experimental import pallas as pl
from jax.experimental.pallas import tpu as pltpu

F32 = jnp.float32
BF16 = jnp.bfloat16

D = 1024
BATCH = 16
SEQ = 256
DEPTH = 4
DEC_BATCH = 4
DEC_SEQ = 1024
PAST = 512
GRID_W = 64
N_MIXERS = 3
HEADS = 8
DH = 64
HD = 128
ROPE_THETA = 10000.0
D_FF = 2816
N_MOD = 6
EPS = 1e-6
LOG2E = 1.0 / math.log(2.0)
FOUR_GROUPS = 4
FOUR_DG = 256

N_P = BATCH * SEQ
N_S = DEC_BATCH * DEC_SEQ
N_TOK = N_P + N_S
TM = 512
NT_P = N_P // TM
NT_S = N_S // TM
TILES_PER_DEC_SEQ = DEC_SEQ // TM
SEQ_PER_TILE = TM // SEQ
MOD_ROWS = 8
FF_CHUNK = 256
ATT_TQ = 512
ATT_HP = 2
ATT_P_WIDTH = 2
ATT_SEQ_SKEW = 5
ATT_SKEW = 1
GLA_HP = 8
GLA_GROUP = 256
GLA_BLK = 8
GLA_LEVELS = (8, 16, 32, 64, 128)
VMEM_LIMIT = 60 * 1024 * 1024


def _cparams(n_axes):
    return pltpu.CompilerParams(dimension_semantics=("arbitrary",) * n_axes,
                                vmem_limit_bytes=VMEM_LIMIT)


def _const_spec(shape):
    nd = len(shape)
    return pl.BlockSpec(shape, lambda *_: (0,) * nd, pipeline_mode=pl.Buffered(1))


def _sigmoid(x):
    return 1.0 / (1.0 + jnp.exp(-x))


def _silu(x):
    return x * _sigmoid(x)


def _adanorm(x, g, shift, scale):
    ms = jnp.mean(x * x, axis=-1, keepdims=True)
    return x * lax.rsqrt(ms + EPS) * (g * (1.0 + scale)) + shift


def _mod_row(tile):
    return jnp.where(tile < NT_P, 0, 1 + (tile - NT_P) // TILES_PER_DEC_SEQ)


def _dot(a, b):
    return jnp.dot(a, b, preferred_element_type=F32)


def _dot_nt(a, b):
    return lax.dot_general(a, b, (((1,), (1,)), ((), ())), preferred_element_type=F32)


def _dot_tn(a, b):
    return lax.dot_general(a, b, (((0,), (0,)), ((), ())), preferred_element_type=F32)


def _split2(x):
    hi = x.astype(BF16)
    return hi, (x - hi.astype(F32)).astype(BF16)


def _interleaved(generators, skew=0):
    results = [None] * len(generators)
    active = list(range(len(generators)))
    rnd = 0
    while active:
        for i in list(active):
            if rnd < i * skew:
                continue
            try:
                next(generators[i])
            except StopIteration as stop:
                results[i] = stop.value
                active.remove(i)
        rnd += 1
        yield
    return results


def _interleave(generators, skew=0):
    rounds = _interleaved(generators, skew)
    while True:
        try:
            next(rounds)
        except StopIteration as stop:
            return stop.value


class _WeightStream:
    def __init__(self, jobs, stage_ref, sem_ref):
        self.jobs, self.stage, self.sem = jobs, stage_ref, sem_ref
        self.depth = stage_ref.shape[0]

    def _copy(self, k):
        slot = k % self.depth
        return pltpu.make_async_copy(self.jobs[k][0], self.stage.at[slot], self.sem.at[slot])

    def prime(self):
        for k in range(min(self.depth, len(self.jobs))):
            self._copy(k).start()

    def take(self, k):
        self._copy(k).wait()
        self.jobs[k][1][...] = self.stage[k % self.depth].astype(BF16)
        if k + self.depth < len(self.jobs):
            self._copy(k + self.depth).start()


def _stream_scratch(piece_shape, depth):
    return [pltpu.VMEM((depth,) + tuple(piece_shape), F32), pltpu.SemaphoreType.DMA((depth,))]


def _column_jobs(w_hbm, index, w_bf, width):
    return [(w_hbm.at[index, :, pl.ds(c, width)], w_bf.at[:, pl.ds(c, width)])
            for c in range(0, w_bf.shape[1], width)]


def _first_step_or_not(body):
    first = pl.program_id(0) == 0
    pl.when(first)(lambda: body(True))
    pl.when(jnp.logical_not(first))(lambda: body(False))


HBM_SPEC = pl.BlockSpec(memory_space=pl.ANY)
W_PIECE = 512
W_DEPTH = 3


MOD_TK = 256


def _mod_kernel(cond_ref, w_ref, b_ref, o_ref):
    kk = pl.program_id(1)
    s = _silu(cond_ref[...])
    s_hi = s.astype(BF16)
    s_lo = (s - s_hi.astype(F32)).astype(BF16)
    w = w_ref[0]
    w_hi = w.astype(BF16)
    w_lo = (w - w_hi.astype(F32)).astype(BF16)
    r = _dot(jnp.concatenate([s_hi, s_lo], axis=0), w_hi)
    part = r[:MOD_ROWS] + r[MOD_ROWS:] + _dot(s_hi, w_lo)

    @pl.when(kk == 0)
    def _():
        o_ref[0] = part + b_ref[0]

    @pl.when(kk != 0)
    def _():
        o_ref[0] = o_ref[0] + part


def _modulation(cond, w_ada, b_ada):
    return pl.pallas_call(
        _mod_kernel,
        grid=(DEPTH, D // MOD_TK),
        in_specs=[
            pl.BlockSpec((MOD_ROWS, MOD_TK), lambda l, k: (0, k)),
            pl.BlockSpec((1, MOD_TK, N_MOD * D), lambda l, k: (l, k, 0)),
            pl.BlockSpec((1, 1, N_MOD * D), lambda l, k: (l, 0, 0)),
        ],
        out_specs=pl.BlockSpec((1, MOD_ROWS, N_MOD * D), lambda l, k: (l, 0, 0)),
        out_shape=jax.ShapeDtypeStruct((DEPTH, MOD_ROWS, N_MOD * D), F32),
        compiler_params=_cparams(2),
        name="modulation",
    )(cond, w_ada, b_ada.reshape(DEPTH, 1, N_MOD * D))


def _rope_tables():
    t = np.arange(DEC_SEQ)
    row, col = t // GRID_W, t % GRID_W
    lane = np.arange(HD) % DH
    use_row = lane < DH // 2
    idx = (lane % (DH // 2)) % (DH // 4)
    inv = ROPE_THETA ** (-idx.astype(np.float64) / (DH // 4))
    pos = np.where(use_row[None, :], row[:, None], col[:, None]).astype(np.float64)
    ang = pos * inv[None, :]
    first = (lane % (DH // 2)) < DH // 4
    cos = np.cos(ang)
    sin = np.sin(ang)
    sin_minus = np.where(first[None, :], -sin, 0.0)
    sin_plus = np.where(first[None, :], 0.0, sin)
    return (jnp.asarray(cos, F32), jnp.asarray(sin_minus, F32), jnp.asarray(sin_plus, F32))


def _rope_tile(t, cos, sin_minus, sin_plus):
    return (t * cos + pltpu.roll(t, HD - DH // 4, 1) * sin_minus
            + pltpu.roll(t, DH // 4, 1) * sin_plus)


def _lambda(lam_ref, lam_init):
    lp = lam_ref[...]
    return (jnp.exp(jnp.sum(lp[0:1] * lp[1:2], axis=-1, keepdims=True))
            - jnp.exp(jnp.sum(lp[2:3] * lp[3:4], axis=-1, keepdims=True)) + lam_init)


def _with_ones(v):
    return jnp.concatenate([v, jnp.ones_like(v)], axis=1)


def _diff_attn_head(q, score_fns, values, lam, g_scaled):
    tq = q.shape[0]
    lane = lax.broadcasted_iota(jnp.int32, q.shape, 1)
    zero = jnp.zeros_like(q)
    qq = jnp.concatenate([jnp.where(lane < DH, q, zero), jnp.where(lane >= DH, q, zero)], axis=0)
    scores = [fn(qq) for fn in score_fns]
    yield
    mx = scores[0].max(axis=-1, keepdims=True)
    for s in scores[1:]:
        mx = jnp.maximum(mx, s.max(axis=-1, keepdims=True))
    yield
    acc = None
    for s, v in zip(scores, values):
        c = _dot(jnp.exp2(s - mx).astype(BF16), v)
        acc = c if acc is None else acc + c
        yield
    r = acc[:, 0:HD] / acc[:, HD:2 * HD]
    o = r[:tq] - lam * r[tq:]
    o = o * lax.rsqrt(jnp.mean(o * o, axis=-1, keepdims=True) + EPS) * g_scaled
    return o.astype(BF16)


def _attn_prompt_kernel(*refs, j, n_layers, lam_init):
    x_ref, mod_ref, g_ref, w_hbm, lam_ref, gsub_ref = refs[:6]
    kt_ref, v_ref, o_ref, w_bf, stage, sem = refs[-6:]
    owns_all = j == 0
    stream = _WeightStream(_column_jobs(w_hbm, j, w_bf, W_PIECE), stage, sem)
    per_mat = D // W_PIECE

    def body(first):
        if first:
            stream.prime()
        m = mod_ref[pl.ds(0, 1), :]
        lam = _lambda(lam_ref, lam_init)
        g_scaled = gsub_ref[...] * (1.0 - lam_init)

        def sequence(s, loads):
            rows = slice(s * SEQ, (s + 1) * SEQ)
            h = _adanorm(x_ref[rows, :], g_ref[...], m[:, 0:D], m[:, D:2 * D]).astype(BF16)
            yield
            mats = []
            for n in range(3):
                if loads:
                    for p in range(per_mat):
                        stream.take(n * per_mat + p)
                mats.append(_dot(h, w_bf[:, n * D:(n + 1) * D]))
                yield
            q = (mats[0] * (DH ** -0.5 * LOG2E)).astype(BF16)
            k, v = mats[1], mats[2]
            kb, vb = k.astype(BF16), v.astype(BF16)
            if owns_all:
                for jj in range(n_layers):
                    kt_ref[s, jj] = k.T if jj == j else jnp.zeros((D, SEQ), F32)
                    v_ref[s, jj] = v if jj == j else jnp.zeros((SEQ, D), F32)
            else:
                kt_ref[s] = k.T
                v_ref[s] = v
            yield
            for h0 in range(0, HEADS, ATT_P_WIDTH):
                heads = []
                for hh in range(h0, h0 + ATT_P_WIDTH):
                    cols = slice(hh * HD, (hh + 1) * HD)
                    kh = kb[:, cols]
                    heads.append(_diff_attn_head(q[:, cols], [lambda qq, kh=kh: _dot_nt(qq, kh)],
                                                 [_with_ones(vb[:, cols])], lam, g_scaled))
                outs = yield from _interleaved(heads, ATT_SKEW)
                for hh, o in zip(range(h0, h0 + ATT_P_WIDTH), outs):
                    o_ref[rows, hh * HD:(hh + 1) * HD] = o

        _interleave([sequence(s, first and s == 0) for s in range(SEQ_PER_TILE)], skew=ATT_SEQ_SKEW)

    _first_step_or_not(body)


def _attn_prompt(x, mod_l, g, w_qkv, j, lam_p, g_sub, lam_init, prev=None, after=None):
    n_layers = w_qkv.shape[0]
    assert (prev is None) == (j == 0)
    tile = pl.BlockSpec((TM, D), lambda i: (i, 0))
    in_specs = [tile, _const_spec((MOD_ROWS, N_MOD * D)), _const_spec((1, D)), HBM_SPEC,
                _const_spec((4, DH)), _const_spec((1, HD))]
    args = [x, mod_l, g, w_qkv, lam_p, g_sub]
    if prev is None:
        kt_spec = pl.BlockSpec((SEQ_PER_TILE, n_layers, D, SEQ), lambda i: (i, 0, 0, 0))
        v_spec = pl.BlockSpec((SEQ_PER_TILE, n_layers, SEQ, D), lambda i: (i, 0, 0, 0))
        aliases = {}
    else:
        kt_spec = pl.BlockSpec((SEQ_PER_TILE, None, D, SEQ), lambda i: (i, j, 0, 0))
        v_spec = pl.BlockSpec((SEQ_PER_TILE, None, SEQ, D), lambda i: (i, j, 0, 0))
        aliases = {len(args): 0, len(args) + 1: 1}
        in_specs += [HBM_SPEC, HBM_SPEC]
        args += list(prev)
    if after is not None:
        in_specs.append(HBM_SPEC)
        args.append(after)
    kern = lambda *refs: _attn_prompt_kernel(*refs, j=j, n_layers=n_layers, lam_init=lam_init)
    return pl.pallas_call(
        kern,
        grid=(NT_P,),
        in_specs=in_specs,
        out_specs=[kt_spec, v_spec, tile],
        out_shape=[jax.ShapeDtypeStruct((BATCH, n_layers, D, SEQ), F32),
                   jax.ShapeDtypeStruct((BATCH, n_layers, SEQ, D), F32), jax.ShapeDtypeStruct((N_P, D), BF16)],
        input_output_aliases=aliases,
        scratch_shapes=[pltpu.VMEM((D, 3 * D), BF16)] + _stream_scratch((D, W_PIECE), W_DEPTH),
        compiler_params=_cparams(1),
        name="attn_prompt",
    )(*args)


def _qkv_sample_kernel(x_ref, mod_ref, g_ref, w_hbm, cos_ref, sm_ref, sp_ref, q_ref, k_ref, v_ref,
                       w_bf, stage, sem, *, j):
    stream = _WeightStream(_column_jobs(w_hbm, j, w_bf, W_PIECE), stage, sem)
    per_mat = D // W_PIECE

    def body(first):
        if first:
            stream.prime()
        i = pl.program_id(0)
        m = mod_ref[pl.ds(1 + i // TILES_PER_DEC_SEQ, 1), :]

        def half(rows, loads):
            h = _adanorm(x_ref[rows, :], g_ref[...], m[:, 0:D], m[:, D:2 * D]).astype(BF16)
            cos, sm, sp = cos_ref[rows, :], sm_ref[rows, :], sp_ref[rows, :]
            yield
            for n, (o_ref, scale) in enumerate(((q_ref, DH ** -0.5 * LOG2E), (k_ref, None), (v_ref, None))):
                if loads:
                    for p in range(per_mat):
                        stream.take(n * per_mat + p)
                y = _dot(h, w_bf[:, n * D:(n + 1) * D])
                if o_ref is v_ref:
                    o_ref[rows, :] = y.astype(BF16)
                else:
                    if scale is not None:
                        y = y * scale
                    for hh in range(HEADS):
                        t = _rope_tile(y[:, hh * HD:(hh + 1) * HD], cos, sm, sp)
                        o_ref[rows, hh * HD:(hh + 1) * HD] = t.astype(BF16)
                yield

        _interleave([half(slice(r, r + TM // 2), first and r == 0) for r in (0, TM // 2)], skew=1)

    _first_step_or_not(body)


def _qkv_sample(x, tile0, mod_l, g, w_qkv, j, tabs):
    tile = pl.BlockSpec((TM, D), lambda i: (i, 0))
    tab = pl.BlockSpec((TM, HD), lambda i: (i % TILES_PER_DEC_SEQ, 0))
    kern = lambda *refs: _qkv_sample_kernel(*refs, j=j)
    return pl.pallas_call(
        kern,
        grid=(NT_S,),
        in_specs=[pl.BlockSpec((TM, D), lambda i: (i + tile0, 0)), _const_spec((MOD_ROWS, N_MOD * D)),
                  _const_spec((1, D)), HBM_SPEC, tab, tab, tab],
        out_specs=[tile, tile, tile],
        out_shape=[jax.ShapeDtypeStruct((N_S, D), BF16)] * 3,
        scratch_shapes=[pltpu.VMEM((D, 3 * D), BF16)] + _stream_scratch((D, W_PIECE), W_DEPTH),
        compiler_params=_cparams(1),
        name="qkv_sample",
    )(x, mod_l, g, w_qkv, *tabs)


def _attn_sample_kernel(*refs, lam_init):
    q_ref, kn_ref, vn_ref, kct_ref, vc_ref, lam_ref, gsub_ref, o_ref = refs
    lam = _lambda(lam_ref, lam_init)
    g_scaled = gsub_ref[...] * (1.0 - lam_init)
    for h0 in range(0, HEADS, ATT_HP):
        heads = []
        for hh in range(h0, h0 + ATT_HP):
            cols = slice(hh * HD, (hh + 1) * HD)
            kn = kn_ref[:, cols]
            kct = kct_ref[hh].astype(BF16)
            vc = vc_ref[:, cols].astype(BF16)
            heads.append(_diff_attn_head(
                q_ref[:, cols], [lambda qq, kn=kn: _dot_nt(qq, kn), lambda qq, kct=kct: _dot(qq, kct)],
                [_with_ones(vn_ref[:, cols]), _with_ones(vc)], lam, g_scaled))
        for hh, o in zip(range(h0, h0 + ATT_HP), _interleave(heads, ATT_SKEW)):
            o_ref[:, hh * HD:(hh + 1) * HD] = o


def _attn_sample(q, kn, vn, cache_kt, cache_v, j, lam_p, g_sub, lam_init):
    nq = DEC_SEQ // ATT_TQ
    seq = pl.BlockSpec((DEC_SEQ, D), lambda b, i: (b, 0))
    blk = pl.BlockSpec((ATT_TQ, D), lambda b, i: (b * nq + i, 0))
    in_specs = [blk, seq, seq,
                pl.BlockSpec((None, None, HEADS, HD, PAST), lambda b, i: (b, j, 0, 0, 0)),
                pl.BlockSpec((None, None, PAST, D), lambda b, i: (b, j, 0, 0)),
                pl.BlockSpec((4, DH), lambda b, i: (0, 0)), pl.BlockSpec((1, HD), lambda b, i: (0, 0))]
    kern = lambda *refs: _attn_sample_kernel(*refs, lam_init=lam_init)
    return pl.pallas_call(
        kern,
        grid=(DEC_BATCH, nq),
        in_specs=in_specs,
        out_specs=blk,
        out_shape=jax.ShapeDtypeStruct((N_S, D), BF16),
        compiler_params=_cparams(2),
        name="attn_sample",
    )(q, kn, vn, cache_kt, cache_v, lam_p, g_sub)


def _rec_proj_kernel(x_ref, mod_ref, g_ref, w_hbm, q_ref, v_ref, gate_ref, zf_ref, zb_ref,
                     w_bf, stage, sem, *, j):
    stream = _WeightStream(_column_jobs(w_hbm, j, w_bf, W_PIECE), stage, sem)
    per_mat = D // W_PIECE

    def body(first):
        if first:
            stream.prime()
        m = mod_ref[pl.ds(_mod_row(pl.program_id(0)), 1), :]

        def half(rows, loads):
            h = _adanorm(x_ref[rows, :], g_ref[...], m[:, 0:D], m[:, D:2 * D]).astype(BF16)
            yield
            for n, o_ref in enumerate((q_ref, v_ref, gate_ref, zf_ref, zb_ref)):
                if loads:
                    for p in range(per_mat):
                        stream.take(n * per_mat + p)
                y = _dot(h, w_bf[:, n * D:(n + 1) * D])
                if o_ref is q_ref or o_ref is gate_ref:
                    y = _silu(y)
                o_ref[rows, :] = y.astype(o_ref.dtype)
                yield

        _interleave([half(slice(r, r + TM // 2), first and r == 0) for r in (0, TM // 2)], skew=1)

    _first_step_or_not(body)


def _rec_proj(x, mod_l, g, w_in, j):
    tile = pl.BlockSpec((TM, D), lambda i: (i, 0))
    kern = lambda *refs: _rec_proj_kernel(*refs, j=j)
    return pl.pallas_call(
        kern,
        grid=(N_TOK // TM,),
        in_specs=[tile, _const_spec((MOD_ROWS, N_MOD * D)), _const_spec((1, D)), HBM_SPEC],
        out_specs=[tile] * 5,
        out_shape=[jax.ShapeDtypeStruct((N_TOK, D), BF16)] * 3 + [jax.ShapeDtypeStruct((N_TOK, D), F32)] * 2,
        scratch_shapes=[pltpu.VMEM((D, 5 * D), BF16)] + _stream_scratch((D, W_PIECE), W_DEPTH),
        compiler_params=_cparams(1),
        name="rec_proj",
    )(x, mod_l, g, w_in)


def _gla_tables():
    t = np.arange(GLA_GROUP)[:, None]
    s = np.arange(GLA_GROUP)[None, :]
    ids = np.full((GLA_GROUP, GLA_GROUP), -1, np.int32)
    for l, b in reversed(list(enumerate(GLA_LEVELS, 1))):
        ids = np.where((t // (2 * b) == s // (2 * b)) & (s < t), l, ids)
    ids = np.where((t // GLA_BLK == s // GLA_BLK) & (s <= t), 0, ids)
    ids = np.stack([ids, ids.T]).astype(np.int32)
    e = np.zeros((GLA_BLK * HD, GLA_GROUP), np.float32)
    for j in range(GLA_BLK):
        e[j * HD:(j + 1) * HD, j::GLA_BLK] = 1.0
    return jnp.asarray(ids), jnp.asarray(e)


def _bcast_row(x, blk, idx):
    n, w = x.shape
    r = x.reshape(n // blk, blk, w)[:, idx:idx + 1, :]
    return jnp.broadcast_to(r, (n // blk, blk, w)).reshape(n, w)


def _gla_group(q, v, z, lb, ids, e_mat, st, backward):
    n = GLA_GROUP
    sig = 0.5 * jnp.tanh(0.5 * z) + 0.5
    f = lb + (1.0 - lb) * sig
    k = (1.0 - lb) * (1.0 - sig)
    lf = jnp.log(f) * LOG2E
    row = lax.broadcasted_iota(jnp.int32, (n, n), 0)
    col = lax.broadcasted_iota(jnp.int32, (n, n), 1)
    tri = jnp.where((col >= row) if backward else (col <= row), 1.0, 0.0).astype(BF16)
    c2 = _dot(tri, jnp.concatenate(_split2(lf), axis=1))
    cum = c2[:, 0:HD] + c2[:, HD:2 * HD]
    ck = jnp.log(k) * LOG2E - cum
    yield

    prods = []
    for j in range(GLA_BLK):
        prods.append((q * jnp.exp2(jnp.minimum(cum + _bcast_row(ck, GLA_BLK, j), 0.0))).astype(BF16))
        yield
    half = n // 2
    ids_q = ids[0:half, 0:half]
    diag = lambda x, h: x[h * half:(h + 1) * half, h * half:(h + 1) * half]
    pair = _dot(jnp.concatenate(prods, axis=1), e_mat)
    quads = [jnp.where(ids_q == 0, diag(pair, h), 0.0) for h in range(2)]
    yield

    for l, b in enumerate(GLA_LEVELS[:-1], 1):
        edge = _bcast_row(cum, 2 * b, b if backward else b - 1)
        ql = (q * jnp.exp2(cum - edge)).astype(BF16)
        kl = jnp.exp2(edge + ck).astype(BF16)
        x = _dot_nt(ql, kl)
        quads = [jnp.where(ids_q == l, diag(x, h), quads[h]) for h in range(2)]
        yield

    assert GLA_LEVELS[-1] == half
    lo, hi = slice(0, half), slice(half, n)
    qs, ks = (lo, hi) if backward else (hi, lo)
    edge = cum[half:half + 1, :] if backward else cum[half - 1:half, :]
    cross = _dot_nt((q[qs] * jnp.exp2(cum[qs] - edge)).astype(BF16),
                    jnp.exp2(edge + ck[ks]).astype(BF16)).astype(BF16)
    o_halves = [_dot(quads[h].astype(BF16), v[h * half:(h + 1) * half]) for h in range(2)]
    o_halves[0 if backward else 1] += _dot(cross, v[ks])
    o = jnp.concatenate(o_halves, axis=0)
    yield

    total = cum[0:1, :] if backward else cum[n - 1:n, :]
    if st is not None:
        o = o + _dot_nt((q * jnp.exp2(cum)).astype(BF16), st.astype(BF16))
    kdec = jnp.exp2(total + ck).astype(BF16)
    upd = _dot_tn(v, kdec)
    st_new = upd if st is None else st * jnp.exp2(total) + upd
    return o, st_new


def _lower_bound(x, layer):
    rows = [x[l:l + 1, :] for l in range(DEPTH)]
    mx = rows[0]
    for r in rows[1:]:
        mx = jnp.maximum(mx, r)
    ex = [jnp.exp(r - mx) for r in rows]
    tot = ex[0]
    for e in ex[1:]:
        tot = tot + e
    acc = ex[0] * 0.0
    for l in range(1, layer + 1):
        acc = acc + ex[l]
    return acc / tot


def _gla_kernel(*refs, n_groups, layer, has_state):
    if has_state:
        (q_ref, v_ref, gate_ref, zf_ref, zb_ref, lbl_ref, g_ref, ids_ref, e_ref, s0_ref,
         o_ref, acc_ref) = refs
    else:
        (q_ref, v_ref, gate_ref, zf_ref, zb_ref, lbl_ref, g_ref, ids_ref, e_ref,
         o_ref, sout_ref, acc_ref) = refs
    e_mat = e_ref[...]
    z_refs = (zf_ref, zb_ref)
    chains = [(hh, d) for hh in range(GLA_HP) for d in range(2)]
    cols = [slice(hh * HD, (hh + 1) * HD) for hh in range(GLA_HP)]
    lbs = {(hh, d): _lower_bound(lbl_ref[d, :, cols[hh]], layer) for hh, d in chains}

    def step(gi, sts):
        rows, gens = [], []
        for d in range(2):
            g = gi if d == 0 else n_groups - 1 - gi
            start = g * GLA_GROUP
            rows.append(pl.ds(start if isinstance(start, int) else pl.multiple_of(start, GLA_GROUP), GLA_GROUP))
        for c, (hh, d) in enumerate(chains):
            gens.append(_gla_group(q_ref[rows[d], cols[hh]].astype(F32), v_ref[rows[d], cols[hh]],
                                   z_refs[d][rows[d], cols[hh]], lbs[hh, d], ids_ref[d], e_mat, sts[c],
                                   backward=(d == 1)))
        new = []
        for (hh, d), (o, st) in zip(chains, _interleave(gens)):
            acc_ref[rows[d], cols[hh]] = acc_ref[rows[d], cols[hh]] + o
            new.append(st)
        return tuple(new)

    acc_ref[...] = jnp.zeros_like(acc_ref)
    if has_state:
        init = tuple(s0_ref[d, hh].T for hh, d in chains)
        lax.fori_loop(0, n_groups, step, init)
    else:
        sts = step(0, (None,) * len(chains))
        for (hh, d), st in zip(chains, sts):
            sout_ref[d, hh] = st.T
    for hh in range(GLA_HP):
        o = acc_ref[:, cols[hh]]
        y = o * lax.rsqrt(jnp.mean(o * o, axis=-1, keepdims=True) + EPS) * g_ref[...]
        o_ref[:, cols[hh]] = (y * gate_ref[:, cols[hh]].astype(F32)).astype(BF16)


def _gla(q, v, gate, zf, zb, lb_logits, g_out, tabs, *, row0, nb, ln, layer, s0=None):
    ids, e_mat = tabs
    assert ln % GLA_GROUP == 0 and (s0 is not None or ln == GLA_GROUP)
    b0 = row0 // ln
    w = GLA_HP * HD
    seq = pl.BlockSpec((ln, w), lambda b, h: (b + b0, h))
    in_specs = [seq] * 5 + [
        pl.BlockSpec((2, DEPTH, w), lambda b, h: (0, 0, h)),
        pl.BlockSpec((1, HD), lambda b, h: (0, 0)),
        pl.BlockSpec((2, GLA_GROUP, GLA_GROUP), lambda b, h: (0, 0, 0)),
        pl.BlockSpec((GLA_BLK * HD, GLA_GROUP), lambda b, h: (0, 0)),
    ]
    args = [q, v, gate, zf, zb, lb_logits, g_out, ids, e_mat]
    out_specs = [pl.BlockSpec((ln, w), lambda b, h: (b, h))]
    out_shape = [jax.ShapeDtypeStruct((nb * ln, D), BF16)]
    state_spec = pl.BlockSpec((None, 2, GLA_HP, HD, HD), lambda b, h: (b, 0, h, 0, 0))
    if s0 is not None:
        in_specs.append(state_spec)
        args.append(s0)
    else:
        out_specs.append(state_spec)
        out_shape.append(jax.ShapeDtypeStruct((nb, 2, HEADS, HD, HD), F32))
    kern = lambda *refs: _gla_kernel(*refs, n_groups=ln // GLA_GROUP, layer=layer, has_state=s0 is not None)
    return pl.pallas_call(
        kern,
        grid=(nb, HEADS // GLA_HP),
        in_specs=in_specs,
        out_specs=out_specs,
        out_shape=out_shape,
        scratch_shapes=[pltpu.VMEM((ln, w), F32)],
        compiler_params=_cparams(2),
        name="gla_sample" if s0 is not None else "gla_prompt",
    )(*args)


def _dft_tables(n):
    c = np.arange(FOUR_DG)
    ang_c = 2.0 * np.pi * ((c[:, None] * c[None, :]) % FOUR_DG) / FOUR_DG
    cs = np.concatenate([np.cos(ang_c), np.sin(ang_c)], axis=1)
    t = np.arange(n)
    ang_n = 2.0 * np.pi * ((t[:, None] * t[None, :]) % n) / n
    cn = np.concatenate([np.cos(ang_n), -np.sin(ang_n)], axis=1) / math.sqrt(n * FOUR_DG)
    return jnp.asarray(cs, F32).astype(BF16), jnp.asarray(cn, F32).astype(BF16)


def _fourier_kernel(x_ref, mod_ref, g_ref, cs_ref, cn_ref, o_ref, *, row_of):
    m = mod_ref[pl.ds(row_of(pl.program_id(0)), 1), :]
    h = _adanorm(x_ref[...], g_ref[...], m[:, 0:D], m[:, D:2 * D]).astype(BF16)
    cs = cs_ref[...]
    cn = cn_ref[...]
    for g in range(FOUR_GROUPS):
        xcs = _dot(h[:, g * FOUR_DG:(g + 1) * FOUR_DG], cs)
        stacked = jnp.concatenate([xcs[:, 0:FOUR_DG], xcs[:, FOUR_DG:]], axis=0).astype(BF16)
        o_ref[:, g * FOUR_DG:(g + 1) * FOUR_DG] = _dot(cn, stacked).astype(BF16)


def _fourier(x, mod_l, g, *, row0, nb, ln, row_of):
    cs, cn = _dft_tables(ln)
    b0 = row0 // ln
    kern = lambda *refs: _fourier_kernel(*refs, row_of=row_of)
    return pl.pallas_call(
        kern,
        grid=(nb,),
        in_specs=[pl.BlockSpec((ln, D), lambda b: (b + b0, 0)), _const_spec((MOD_ROWS, N_MOD * D)),
                  _const_spec((1, D)), _const_spec((FOUR_DG, 2 * FOUR_DG)), _const_spec((ln, 2 * ln))],
        out_specs=pl.BlockSpec((ln, D), lambda b: (b, 0)),
        out_shape=jax.ShapeDtypeStruct((nb * ln, D), BF16),
        compiler_params=_cparams(1),
        name="fourier_%d" % ln,
    )(x, mod_l, g, cs, cn)


FF_DEPTH_COLS = 4
FF_DEPTH_ROWS = 2


def _post_ffn_kernel(*refs, layer, wo_index, split_x, final):
    refs = list(refs)
    x_refs = [refs.pop(0) for _ in range(2 if split_x else 1)]
    op_ref, os_ref, mod_ref, wo_hbm, g_ref, win_hbm, wout_hbm = refs[:7]
    refs = refs[7:]
    gf_ref = refs.pop(0) if final else None
    out_refs = [refs.pop(0) for _ in range(2 if final else 1)]
    wo_bf, win_bf, wout_bf, stage_c, sem_c, stage_r, sem_r = refs
    n_chunks = D_FF // FF_CHUNK
    wo_jobs = _column_jobs(wo_hbm, wo_index, wo_bf, FF_CHUNK)
    in_jobs = []
    for c in range(n_chunks):
        for c0 in (c * FF_CHUNK, D_FF + c * FF_CHUNK):
            in_jobs.append((win_hbm.at[layer, :, pl.ds(c0, FF_CHUNK)], win_bf.at[:, pl.ds(c0, FF_CHUNK)]))
    cols = _WeightStream(wo_jobs + in_jobs, stage_c, sem_c)
    rows = _WeightStream([(wout_hbm.at[layer, pl.ds(c * FF_CHUNK, FF_CHUNK), :],
                           wout_bf.at[pl.ds(c * FF_CHUNK, FF_CHUNK), :]) for c in range(n_chunks)],
                         stage_r, sem_r)

    def body(first):
        if first:
            cols.prime()
            rows.prime()
        i = pl.program_id(0)
        ctx = i < NT_P
        m = mod_ref[pl.ds(_mod_row(i), 1), :]
        x = jnp.where(ctx, x_refs[0][...], x_refs[1][...]) if split_x else x_refs[0][...]
        o = jnp.where(ctx, op_ref[...], os_ref[...])
        if first:
            for k in range(len(wo_jobs)):
                cols.take(k)
        x = x + m[:, 2 * D:3 * D] * _dot(o, wo_bf[...])
        h = _adanorm(x, g_ref[...], m[:, 3 * D:4 * D], m[:, 4 * D:5 * D]).astype(BF16)
        acc = None
        for c in range(n_chunks):
            if first:
                cols.take(len(wo_jobs) + 2 * c)
                cols.take(len(wo_jobs) + 2 * c + 1)
                rows.take(c)
            gt = _dot(h, win_bf[:, c * FF_CHUNK:(c + 1) * FF_CHUNK])
            up = _dot(h, win_bf[:, D_FF + c * FF_CHUNK:D_FF + (c + 1) * FF_CHUNK])
            part = _dot((_silu(gt) * up).astype(BF16), wout_bf[c * FF_CHUNK:(c + 1) * FF_CHUNK, :])
            acc = part if acc is None else acc + part
        x = x + m[:, 5 * D:6 * D] * acc
        if not final:
            out_refs[0][...] = x
            return
        y = x * lax.rsqrt(jnp.mean(x * x, axis=-1, keepdims=True) + EPS) * gf_ref[...]
        if first:
            out_refs[0][...] = y
            return

        @pl.when(ctx)
        def _():
            out_refs[0][...] = y

        @pl.when(jnp.logical_not(ctx))
        def _():
            out_refs[1][...] = y

    _first_step_or_not(body)


def _post_ffn(xs, op, os_, mod_l, w_o, wo_index, g, w_in, w_out, layer, g_final=None):
    tile = pl.BlockSpec((TM, D), lambda i: (i, 0))
    p_tile = pl.BlockSpec((TM, D), lambda i: (jnp.minimum(i, NT_P - 1), 0))
    s_tile = pl.BlockSpec((TM, D), lambda i: (jnp.maximum(i - NT_P, 0), 0))
    split_x = len(xs) == 2
    final = g_final is not None
    in_specs = ([p_tile, s_tile] if split_x else [tile]) + [
        p_tile, s_tile, _const_spec((MOD_ROWS, N_MOD * D)), HBM_SPEC, _const_spec((1, D)), HBM_SPEC, HBM_SPEC]
    args = list(xs) + [op, os_, mod_l, w_o, g, w_in, w_out]
    if final:
        in_specs.append(_const_spec((1, D)))
        args.append(g_final)
        out_specs = [p_tile, s_tile]
        out_shape = [jax.ShapeDtypeStruct((N_P, D), F32), jax.ShapeDtypeStruct((N_S, D), F32)]
    else:
        out_specs = tile
        out_shape = jax.ShapeDtypeStruct((N_TOK, D), F32)
    kern = lambda *refs: _post_ffn_kernel(*refs, layer=layer, wo_index=wo_index, split_x=split_x, final=final)
    return pl.pallas_call(
        kern,
        grid=(N_TOK // TM,),
        in_specs=in_specs,
        out_specs=out_specs,
        out_shape=out_shape,
        scratch_shapes=([pltpu.VMEM((D, D), BF16), pltpu.VMEM((D, 2 * D_FF), BF16), pltpu.VMEM((D_FF, D), BF16)]
                        + _stream_scratch((D, FF_CHUNK), FF_DEPTH_COLS)
                        + _stream_scratch((FF_CHUNK, D), FF_DEPTH_ROWS)),
        compiler_params=_cparams(1),
        name="post_ffn",
    )(*args)


def kernel(x_prompt, x_sample, cache_attn_k, cache_attn_v, state_hgrn, c, c_ctx, w_ada, b_ada, g_norm_mix,
           g_norm_ffn, w_qkv_attn, lam_attn, g_subln_attn, w_o_attn, w_in_rec, lb_logits_rec, g_out_rec,
           w_o_rec, w_four, w_ffn_in, w_ffn_out, g_final):
    xs = (x_prompt.reshape(N_P, D), x_sample.reshape(N_S, D))
    cond = jnp.concatenate([c_ctx.reshape(1, D), c, jnp.zeros((MOD_ROWS - 1 - DEC_BATCH, D), F32)], axis=0)
    mod = _modulation(cond, w_ada, b_ada)
    cache_kt = jnp.transpose(cache_attn_k, (0, 1, 3, 4, 5, 2)).reshape(DEC_BATCH, -1, HEADS, HD, PAST)
    cache_v = cache_attn_v.reshape(DEC_BATCH, -1, PAST, D)
    rope_tabs = _rope_tables()
    gla_tabs = _gla_tables()
    gla_tabs = (gla_tabs[0], gla_tabs[1].astype(BF16))
    kv_ctx, new_s = None, []
    for i in range(DEPTH):
        kind, j = i % N_MIXERS, i // N_MIXERS
        g_mix = g_norm_mix[i].reshape(1, D)
        x_p = xs[0]
        x_s, s_tile0 = (xs[1], 0) if len(xs) == 2 else (xs[0], NT_P)
        if kind == 0:
            lam_init = 0.8 - 0.6 * math.exp(-0.3 * i)
            g_sub = g_subln_attn[j].reshape(1, HD)
            qs, ks, vs = _qkv_sample(x_s, s_tile0, mod[i], g_mix, w_qkv_attn, j, rope_tabs)
            os_ = _attn_sample(qs, ks, vs, cache_kt, cache_v, j, lam_attn[j], g_sub, lam_init)
            kt_all, v_all, op = _attn_prompt(x_p, mod[i], g_mix, w_qkv_attn, j, lam_attn[j], g_sub, lam_init,
                                             prev=kv_ctx, after=qs)
            kv_ctx = (kt_all, v_all)
            w_o = w_o_attn
        elif kind == 1:
            proj = _rec_proj(xs[0], mod[i], g_mix, w_in_rec, j)
            g_out = g_out_rec[j].reshape(1, HD)
            op, s_ctx = _gla(*proj, lb_logits_rec, g_out, gla_tabs, row0=0, nb=BATCH, ln=SEQ, layer=i)
            new_s.append(s_ctx)
            os_ = _gla(*proj, lb_logits_rec, g_out, gla_tabs, row0=N_P, nb=DEC_BATCH, ln=DEC_SEQ, layer=i,
                       s0=state_hgrn[:, j])[0]
            w_o = w_o_rec
        else:
            op = _fourier(xs[0], mod[i], g_mix, row0=0, nb=BATCH, ln=SEQ, row_of=lambda b: 0)
            os_ = _fourier(xs[0], mod[i], g_mix, row0=N_P, nb=DEC_BATCH, ln=DEC_SEQ, row_of=lambda b: 1 + b)
            w_o = w_four
        out = _post_ffn(xs, op, os_, mod[i], w_o, j, g_norm_ffn[i].reshape(1, D), w_ffn_in, w_ffn_out, i,
                        g_final.reshape(1, D) if i == DEPTH - 1 else None)
        xs = tuple(out) if i == DEPTH - 1 else (out,)
    y_prompt = xs[0].reshape(BATCH, SEQ, D)
    y_sample = xs[1].reshape(DEC_BATCH, DEC_SEQ, D)
    new_k = jnp.transpose(kv_ctx[0].reshape(BATCH, -1, HEADS, 2, DH, SEQ), (0, 1, 5, 2, 3, 4))
    new_v = kv_ctx[1].reshape(BATCH, -1, SEQ, HEADS, HD)
    return (y_prompt, y_sample, new_k, new_v, jnp.stack(new_s, axis=1))
```

```python
import math

import numpy as np
import jax
import jax.numpy as jnp
from jax import lax
from jax.experimental import pallas as pl
from jax.experimental.pallas import tpu as pltpu

F32 = jnp.float32
BF16 = jnp.bfloat16

D = 1024
BATCH = 16
SEQ = 256
DEPTH = 4
DEC_BATCH = 4
DEC_SEQ = 1024
PAST = 512
GRID_W = 64
N_MIXERS = 3
HEADS = 8
DH = 64
HD = 128
ROPE_THETA = 10000.0
D_FF = 2816
N_MOD = 6
EPS = 1e-6
LOG2E = 1.0 / math.log(2.0)
FOUR_GROUPS = 4
FOUR_DG = 256

N_P = BATCH * SEQ
N_S = DEC_BATCH * DEC_SEQ
N_TOK = N_P + N_S
TM = 512
NT_P = N_P // TM
NT_S = N_S // TM
TILES_PER_DEC_SEQ = DEC_SEQ // TM
SEQ_PER_TILE = TM // SEQ
MOD_ROWS = 8
FF_CHUNK = 256
ATT_TQ = 512
ATT_HP = 2
ATT_P_WIDTH = 2
ATT_SEQ_SKEW = 5
ATT_SKEW = 1
FOUR_CTX_PER_STEP = 4
GLA_HP = 8
GLA_HP_CARRY = 4
GLA_GROUP = 256
GLA_BLK = 8
GLA_LEVELS = (8, 16, 32, 64, 128)
VMEM_LIMIT = 60 * 1024 * 1024


def _cparams(n_axes):
    return pltpu.CompilerParams(dimension_semantics=("arbitrary",) * n_axes,
                                vmem_limit_bytes=VMEM_LIMIT)


def _const_spec(shape):
    nd = len(shape)
    return pl.BlockSpec(shape, lambda *_: (0,) * nd, pipeline_mode=pl.Buffered(1))


def _sigmoid(x):
    return 1.0 / (1.0 + jnp.exp(-x))


def _silu(x):
    return x * _sigmoid(x)


def _adanorm(x, g, shift, scale):
    ms = jnp.mean(x * x, axis=-1, keepdims=True)
    return x * lax.rsqrt(ms + EPS) * (g * (1.0 + scale)) + shift


def _mod_row(tile):
    return jnp.where(tile < NT_P, 0, 1 + (tile - NT_P) // TILES_PER_DEC_SEQ)


def _dot(a, b):
    return jnp.dot(a, b, preferred_element_type=F32)


def _dot_nt(a, b):
    return lax.dot_general(a, b, (((1,), (1,)), ((), ())), preferred_element_type=F32)


def _dot_tn(a, b):
    return lax.dot_general(a, b, (((0,), (0,)), ((), ())), preferred_element_type=F32)


def _split2(x):
    hi = x.astype(BF16)
    return hi, (x - hi.astype(F32)).astype(BF16)


def _interleaved(generators, skew=0):
    results = [None] * len(generators)
    active = list(range(len(generators)))
    rnd = 0
    while active:
        for i in list(active):
            if rnd < i * skew:
                continue
            try:
                next(generators[i])
            except StopIteration as stop:
                results[i] = stop.value
                active.remove(i)
        rnd += 1
        yield
    return results


def _interleave(generators, skew=0):
    rounds = _interleaved(generators, skew)
    while True:
        try:
            next(rounds)
        except StopIteration as stop:
            return stop.value


class _WeightStream:
    def __init__(self, jobs, stage_ref, sem_ref):
        self.jobs, self.stage, self.sem = jobs, stage_ref, sem_ref
        self.depth = stage_ref.shape[0]

    def _copy(self, k):
        slot = k % self.depth
        return pltpu.make_async_copy(self.jobs[k][0], self.stage.at[slot], self.sem.at[slot])

    def prime(self):
        for k in range(min(self.depth, len(self.jobs))):
            self._copy(k).start()

    def take(self, k):
        self._copy(k).wait()
        self.jobs[k][1][...] = self.stage[k % self.depth].astype(BF16)
        if k + self.depth < len(self.jobs):
            self._copy(k + self.depth).start()


def _stream_scratch(piece_shape, depth):
    return [pltpu.VMEM((depth,) + tuple(piece_shape), F32), pltpu.SemaphoreType.DMA((depth,))]


def _column_jobs(w_hbm, index, w_bf, width):
    return [(w_hbm.at[index, :, pl.ds(c, width)], w_bf.at[:, pl.ds(c, width)])
            for c in range(0, w_bf.shape[1], width)]


def _first_step_or_not(body):
    first = pl.program_id(0) == 0
    pl.when(first)(lambda: body(True))
    pl.when(jnp.logical_not(first))(lambda: body(False))


HBM_SPEC = pl.BlockSpec(memory_space=pl.ANY)
W_PIECE = 512
W_DEPTH = 3


MOD_TK = 512


def _mod_kernel(cond_ref, w_ref, b_ref, o_ref):
    kk = pl.program_id(1)
    s = _silu(cond_ref[...])
    s_hi = s.astype(BF16)
    s_lo = (s - s_hi.astype(F32)).astype(BF16)
    w = w_ref[0]
    w_hi = w.astype(BF16)
    w_lo = (w - w_hi.astype(F32)).astype(BF16)
    r = _dot(jnp.concatenate([s_hi, s_lo], axis=0), w_hi)
    part = r[:MOD_ROWS] + r[MOD_ROWS:] + _dot(s_hi, w_lo)

    @pl.when(kk == 0)
    def _():
        o_ref[0] = part + b_ref[0]

    @pl.when(kk != 0)
    def _():
        o_ref[0] = o_ref[0] + part


def _modulation(cond, w_ada, b_ada):
    return pl.pallas_call(
        _mod_kernel,
        grid=(DEPTH, D // MOD_TK),
        in_specs=[
            pl.BlockSpec((MOD_ROWS, MOD_TK), lambda l, k: (0, k)),
            pl.BlockSpec((1, MOD_TK, N_MOD * D), lambda l, k: (l, k, 0)),
            pl.BlockSpec((1, 1, N_MOD * D), lambda l, k: (l, 0, 0)),
        ],
        out_specs=pl.BlockSpec((1, MOD_ROWS, N_MOD * D), lambda l, k: (l, 0, 0)),
        out_shape=jax.ShapeDtypeStruct((DEPTH, MOD_ROWS, N_MOD * D), F32),
        compiler_params=_cparams(2),
        name="modulation",
    )(cond, w_ada, b_ada.reshape(DEPTH, 1, N_MOD * D))


def _rope_tables():
    t = np.arange(DEC_SEQ)
    row, col = t // GRID_W, t % GRID_W
    lane = np.arange(HD) % DH
    use_row = lane < DH // 2
    idx = (lane % (DH // 2)) % (DH // 4)
    inv = ROPE_THETA ** (-idx.astype(np.float64) / (DH // 4))
    pos = np.where(use_row[None, :], row[:, None], col[:, None]).astype(np.float64)
    ang = pos * inv[None, :]
    first = (lane % (DH // 2)) < DH // 4
    cos = np.cos(ang)
    sin = np.sin(ang)
    sin_minus = np.where(first[None, :], -sin, 0.0)
    sin_plus = np.where(first[None, :], 0.0, sin)
    return (jnp.asarray(cos, F32), jnp.asarray(sin_minus, F32), jnp.asarray(sin_plus, F32))


def _rope_tile(t, cos, sin_minus, sin_plus):
    return (t * cos + pltpu.roll(t, HD - DH // 4, 1) * sin_minus
            + pltpu.roll(t, DH // 4, 1) * sin_plus)


def _lambda(lam_ref, lam_init):
    lp = lam_ref[...]
    return (jnp.exp(jnp.sum(lp[0:1] * lp[1:2], axis=-1, keepdims=True))
            - jnp.exp(jnp.sum(lp[2:3] * lp[3:4], axis=-1, keepdims=True)) + lam_init)


def _with_ones(v):
    return jnp.concatenate([v, jnp.ones_like(v)], axis=1)


def _diff_attn_head(q, score_fns, values, lam, g_scaled):
    tq = q.shape[0]
    lane = lax.broadcasted_iota(jnp.int32, q.shape, 1)
    zero = jnp.zeros_like(q)
    qq = jnp.concatenate([jnp.where(lane < DH, q, zero), jnp.where(lane >= DH, q, zero)], axis=0)
    scores = [fn(qq) for fn in score_fns]
    yield
    mx = scores[0].max(axis=-1, keepdims=True)
    for s in scores[1:]:
        mx = jnp.maximum(mx, s.max(axis=-1, keepdims=True))
    yield
    acc = None
    for s, v in zip(scores, values):
        c = _dot(jnp.exp2(s - mx).astype(BF16), v)
        acc = c if acc is None else acc + c
        yield
    r = acc[:, 0:HD] / acc[:, HD:2 * HD]
    o = r[:tq] - lam * r[tq:]
    o = o * lax.rsqrt(jnp.mean(o * o, axis=-1, keepdims=True) + EPS) * g_scaled
    return o.astype(BF16)


def _attn_prompt_kernel(*refs, j, n_layers, lam_init):
    x_ref, mod_ref, g_ref, w_hbm, lam_ref, gsub_ref = refs[:6]
    kt_ref, v_ref, o_ref, w_bf, stage, sem = refs[-6:]
    owns_all = j == 0
    stream = _WeightStream(_column_jobs(w_hbm, j, w_bf, W_PIECE), stage, sem)
    per_mat = D // W_PIECE

    def body(first):
        if first:
            stream.prime()
        m = mod_ref[pl.ds(0, 1), :]
        lam = _lambda(lam_ref, lam_init)
        g_scaled = gsub_ref[...] * (1.0 - lam_init)

        def sequence(s, loads):
            rows = slice(s * SEQ, (s + 1) * SEQ)
            h = _adanorm(x_ref[rows, :], g_ref[...], m[:, 0:D], m[:, D:2 * D]).astype(BF16)
            yield
            mats = []
            for n in range(3):
                if loads:
                    for p in range(per_mat):
                        stream.take(n * per_mat + p)
                mats.append(_dot(h, w_bf[:, n * D:(n + 1) * D]))
                yield
            q = (mats[0] * (DH ** -0.5 * LOG2E)).astype(BF16)
            k, v = mats[1], mats[2]
            kb, vb = k.astype(BF16), v.astype(BF16)
            if owns_all:
                for jj in range(n_layers):
                    kt_ref[s, jj] = k.T if jj == j else jnp.zeros((D, SEQ), F32)
                    v_ref[s, jj] = v if jj == j else jnp.zeros((SEQ, D), F32)
            else:
                kt_ref[s] = k.T
                v_ref[s] = v
            yield
            for h0 in range(0, HEADS, ATT_P_WIDTH):
                heads = []
                for hh in range(h0, h0 + ATT_P_WIDTH):
                    cols = slice(hh * HD, (hh + 1) * HD)
                    kh = kb[:, cols]
                    heads.append(_diff_attn_head(q[:, cols], [lambda qq, kh=kh: _dot_nt(qq, kh)],
                                                 [_with_ones(vb[:, cols])], lam, g_scaled))
                outs = yield from _interleaved(heads, ATT_SKEW)
                for hh, o in zip(range(h0, h0 + ATT_P_WIDTH), outs):
                    o_ref[rows, hh * HD:(hh + 1) * HD] = o

        _interleave([sequence(s, first and s == 0) for s in range(SEQ_PER_TILE)], skew=ATT_SEQ_SKEW)

    _first_step_or_not(body)


def _attn_prompt(x, mod_l, g, w_qkv, j, lam_p, g_sub, lam_init, prev=None, after=None):
    n_layers = w_qkv.shape[0]
    assert (prev is None) == (j == 0)
    tile = pl.BlockSpec((TM, D), lambda i: (i, 0))
    in_specs = [tile, _const_spec((MOD_ROWS, N_MOD * D)), _const_spec((1, D)), HBM_SPEC,
                _const_spec((4, DH)), _const_spec((1, HD))]
    args = [x, mod_l, g, w_qkv, lam_p, g_sub]
    if prev is None:
        kt_spec = pl.BlockSpec((SEQ_PER_TILE, n_layers, D, SEQ), lambda i: (i, 0, 0, 0))
        v_spec = pl.BlockSpec((SEQ_PER_TILE, n_layers, SEQ, D), lambda i: (i, 0, 0, 0))
        aliases = {}
    else:
        kt_spec = pl.BlockSpec((SEQ_PER_TILE, None, D, SEQ), lambda i: (i, j, 0, 0))
        v_spec = pl.BlockSpec((SEQ_PER_TILE, None, SEQ, D), lambda i: (i, j, 0, 0))
        aliases = {len(args): 0, len(args) + 1: 1}
        in_specs += [HBM_SPEC, HBM_SPEC]
        args += list(prev)
    if after is not None:
        in_specs.append(HBM_SPEC)
        args.append(after)
    kern = lambda *refs: _attn_prompt_kernel(*refs, j=j, n_layers=n_layers, lam_init=lam_init)
    return pl.pallas_call(
        kern,
        grid=(NT_P,),
        in_specs=in_specs,
        out_specs=[kt_spec, v_spec, tile],
        out_shape=[jax.ShapeDtypeStruct((BATCH, n_layers, D, SEQ), F32),
                   jax.ShapeDtypeStruct((BATCH, n_layers, SEQ, D), F32), jax.ShapeDtypeStruct((N_P, D), BF16)],
        input_output_aliases=aliases,
        scratch_shapes=[pltpu.VMEM((D, 3 * D), BF16)] + _stream_scratch((D, W_PIECE), W_DEPTH),
        compiler_params=_cparams(1),
        name="attn_prompt",
    )(*args)


def _qkv_sample_kernel(x_ref, mod_ref, g_ref, w_hbm, cos_ref, sm_ref, sp_ref, q_ref, k_ref, v_ref,
                       w_bf, stage, sem, *, j):
    stream = _WeightStream(_column_jobs(w_hbm, j, w_bf, W_PIECE), stage, sem)
    per_mat = D // W_PIECE

    def body(first):
        if first:
            stream.prime()
        i = pl.program_id(0)
        m = mod_ref[pl.ds(1 + i // TILES_PER_DEC_SEQ, 1), :]

        def half(rows, loads):
            h = _adanorm(x_ref[rows, :], g_ref[...], m[:, 0:D], m[:, D:2 * D]).astype(BF16)
            cos, sm, sp = cos_ref[rows, :], sm_ref[rows, :], sp_ref[rows, :]
            yield
            for n, (o_ref, scale) in enumerate(((q_ref, DH ** -0.5 * LOG2E), (k_ref, None), (v_ref, None))):
                if loads:
                    for p in range(per_mat):
                        stream.take(n * per_mat + p)
                y = _dot(h, w_bf[:, n * D:(n + 1) * D])
                if o_ref is v_ref:
                    o_ref[rows, :] = y.astype(BF16)
                else:
                    if scale is not None:
                        y = y * scale
                    for hh in range(HEADS):
                        t = _rope_tile(y[:, hh * HD:(hh + 1) * HD], cos, sm, sp)
                        o_ref[rows, hh * HD:(hh + 1) * HD] = t.astype(BF16)
                yield

        _interleave([half(slice(r, r + TM // 2), first and r == 0) for r in (0, TM // 2)], skew=1)

    _first_step_or_not(body)


def _qkv_sample(x, tile0, mod_l, g, w_qkv, j, tabs):
    tile = pl.BlockSpec((TM, D), lambda i: (i, 0))
    tab = pl.BlockSpec((TM, HD), lambda i: (i % TILES_PER_DEC_SEQ, 0))
    kern = lambda *refs: _qkv_sample_kernel(*refs, j=j)
    return pl.pallas_call(
        kern,
        grid=(NT_S,),
        in_specs=[pl.BlockSpec((TM, D), lambda i: (i + tile0, 0)), _const_spec((MOD_ROWS, N_MOD * D)),
                  _const_spec((1, D)), HBM_SPEC, tab, tab, tab],
        out_specs=[tile, tile, tile],
        out_shape=[jax.ShapeDtypeStruct((N_S, D), BF16)] * 3,
        scratch_shapes=[pltpu.VMEM((D, 3 * D), BF16)] + _stream_scratch((D, W_PIECE), W_DEPTH),
        compiler_params=_cparams(1),
        name="qkv_sample",
    )(x, mod_l, g, w_qkv, *tabs)


def _attn_sample_kernel(*refs, lam_init):
    q_ref, kn_ref, vn_ref, kct_ref, vc_ref, lam_ref, gsub_ref, o_ref = refs
    lam = _lambda(lam_ref, lam_init)
    g_scaled = gsub_ref[...] * (1.0 - lam_init)
    for h0 in range(0, HEADS, ATT_HP):
        heads = []
        for hh in range(h0, h0 + ATT_HP):
            cols = slice(hh * HD, (hh + 1) * HD)
            kn = kn_ref[:, cols]
            kct = kct_ref[hh].astype(BF16)
            vc = vc_ref[:, cols].astype(BF16)
            heads.append(_diff_attn_head(
                q_ref[:, cols], [lambda qq, kn=kn: _dot_nt(qq, kn), lambda qq, kct=kct: _dot(qq, kct)],
                [_with_ones(vn_ref[:, cols]), _with_ones(vc)], lam, g_scaled))
        for hh, o in zip(range(h0, h0 + ATT_HP), _interleave(heads, ATT_SKEW)):
            o_ref[:, hh * HD:(hh + 1) * HD] = o


def _attn_sample(q, kn, vn, cache_kt, cache_v, j, lam_p, g_sub, lam_init):
    nq = DEC_SEQ // ATT_TQ
    seq = pl.BlockSpec((DEC_SEQ, D), lambda b, i: (b, 0))
    blk = pl.BlockSpec((ATT_TQ, D), lambda b, i: (b * nq + i, 0))
    in_specs = [blk, seq, seq,
                pl.BlockSpec((None, None, HEADS, HD, PAST), lambda b, i: (b, j, 0, 0, 0)),
                pl.BlockSpec((None, None, PAST, D), lambda b, i: (b, j, 0, 0)),
                pl.BlockSpec((4, DH), lambda b, i: (0, 0)), pl.BlockSpec((1, HD), lambda b, i: (0, 0))]
    kern = lambda *refs: _attn_sample_kernel(*refs, lam_init=lam_init)
    return pl.pallas_call(
        kern,
        grid=(DEC_BATCH, nq),
        in_specs=in_specs,
        out_specs=blk,
        out_shape=jax.ShapeDtypeStruct((N_S, D), BF16),
        compiler_params=_cparams(2),
        name="attn_sample",
    )(q, kn, vn, cache_kt, cache_v, lam_p, g_sub)


def _rec_proj_kernel(x_ref, mod_ref, g_ref, w_hbm, q_ref, v_ref, gate_ref, zf_ref, zb_ref,
                     w_bf, stage, sem, *, j):
    stream = _WeightStream(_column_jobs(w_hbm, j, w_bf, W_PIECE), stage, sem)
    per_mat = D // W_PIECE

    def body(first):
        if first:
            stream.prime()
        m = mod_ref[pl.ds(_mod_row(pl.program_id(0)), 1), :]

        def half(rows, loads):
            h = _adanorm(x_ref[rows, :], g_ref[...], m[:, 0:D], m[:, D:2 * D]).astype(BF16)
            yield
            for n, o_ref in enumerate((q_ref, v_ref, gate_ref, zf_ref, zb_ref)):
                if loads:
                    for p in range(per_mat):
                        stream.take(n * per_mat + p)
                y = _dot(h, w_bf[:, n * D:(n + 1) * D])
                if o_ref is q_ref or o_ref is gate_ref:
                    y = _silu(y)
                o_ref[rows, :] = y.astype(o_ref.dtype)
                yield

        _interleave([half(slice(r, r + TM // 2), first and r == 0) for r in (0, TM // 2)], skew=1)

    _first_step_or_not(body)


def _rec_proj(x, mod_l, g, w_in, j):
    tile = pl.BlockSpec((TM, D), lambda i: (i, 0))
    kern = lambda *refs: _rec_proj_kernel(*refs, j=j)
    return pl.pallas_call(
        kern,
        grid=(N_TOK // TM,),
        in_specs=[tile, _const_spec((MOD_ROWS, N_MOD * D)), _const_spec((1, D)), HBM_SPEC],
        out_specs=[tile] * 5,
        out_shape=[jax.ShapeDtypeStruct((N_TOK, D), BF16)] * 3 + [jax.ShapeDtypeStruct((N_TOK, D), F32)] * 2,
        scratch_shapes=[pltpu.VMEM((D, 5 * D), BF16)] + _stream_scratch((D, W_PIECE), W_DEPTH),
        compiler_params=_cparams(1),
        name="rec_proj",
    )(x, mod_l, g, w_in)


def _gla_tables():
    t = np.arange(GLA_GROUP)[:, None]
    s = np.arange(GLA_GROUP)[None, :]
    ids = np.full((GLA_GROUP, GLA_GROUP), -1, np.int32)
    for l, b in reversed(list(enumerate(GLA_LEVELS, 1))):
        ids = np.where((t // (2 * b) == s // (2 * b)) & (s < t), l, ids)
    ids = np.where((t // GLA_BLK == s // GLA_BLK) & (s <= t), 0, ids)
    ids = np.stack([ids, ids.T]).astype(np.int32)
    e = np.zeros((GLA_BLK * HD, GLA_GROUP), np.float32)
    for j in range(GLA_BLK):
        e[j * HD:(j + 1) * HD, j::GLA_BLK] = 1.0
    return jnp.asarray(ids), jnp.asarray(e)


def _bcast_row(x, blk, idx):
    n, w = x.shape
    r = x.reshape(n // blk, blk, w)[:, idx:idx + 1, :]
    return jnp.broadcast_to(r, (n // blk, blk, w)).reshape(n, w)


def _gla_group(q, v, z, lb, ids, e_mat, st, backward):
    n = GLA_GROUP
    sig = 0.5 * jnp.tanh(0.5 * z) + 0.5
    f = lb + (1.0 - lb) * sig
    k = (1.0 - lb) * (1.0 - sig)
    lf = jnp.log(f) * LOG2E
    row = lax.broadcasted_iota(jnp.int32, (n, n), 0)
    col = lax.broadcasted_iota(jnp.int32, (n, n), 1)
    tri = jnp.where((col >= row) if backward else (col <= row), 1.0, 0.0).astype(BF16)
    c2 = _dot(tri, jnp.concatenate(_split2(lf), axis=1))
    cum = c2[:, 0:HD] + c2[:, HD:2 * HD]
    ck = jnp.log(k) * LOG2E - cum
    yield

    prods = []
    for j in range(GLA_BLK):
        prods.append((q * jnp.exp2(jnp.minimum(cum + _bcast_row(ck, GLA_BLK, j), 0.0))).astype(BF16))
        yield
    half = n // 2
    ids_q = ids[0:half, 0:half]
    diag = lambda x, h: x[h * half:(h + 1) * half, h * half:(h + 1) * half]
    pair = _dot(jnp.concatenate(prods, axis=1), e_mat)
    quads = [jnp.where(ids_q == 0, diag(pair, h), 0.0) for h in range(2)]
    yield

    for l, b in enumerate(GLA_LEVELS[:-1], 1):
        edge = _bcast_row(cum, 2 * b, b if backward else b - 1)
        ql = (q * jnp.exp2(cum - edge)).astype(BF16)
        kl = jnp.exp2(edge + ck).astype(BF16)
        x = _dot_nt(ql, kl)
        quads = [jnp.where(ids_q == l, diag(x, h), quads[h]) for h in range(2)]
        yield

    assert GLA_LEVELS[-1] == half
    lo, hi = slice(0, half), slice(half, n)
    qs, ks = (lo, hi) if backward else (hi, lo)
    edge = cum[half:half + 1, :] if backward else cum[half - 1:half, :]
    cross = _dot_nt((q[qs] * jnp.exp2(cum[qs] - edge)).astype(BF16),
                    jnp.exp2(edge + ck[ks]).astype(BF16)).astype(BF16)
    o_halves = [_dot(quads[h].astype(BF16), v[h * half:(h + 1) * half]) for h in range(2)]
    o_halves[0 if backward else 1] += _dot(cross, v[ks])
    o = jnp.concatenate(o_halves, axis=0)
    yield

    total = cum[0:1, :] if backward else cum[n - 1:n, :]
    if st is not None:
        o = o + _dot_nt((q * jnp.exp2(cum)).astype(BF16), st.astype(BF16))
    kdec = jnp.exp2(total + ck).astype(BF16)
    upd = _dot_tn(v, kdec)
    st_new = upd if st is None else st * jnp.exp2(total) + upd
    return o, st_new


def _lower_bound(x, layer):
    rows = [x[l:l + 1, :] for l in range(DEPTH)]
    mx = rows[0]
    for r in rows[1:]:
        mx = jnp.maximum(mx, r)
    ex = [jnp.exp(r - mx) for r in rows]
    tot = ex[0]
    for e in ex[1:]:
        tot = tot + e
    acc = ex[0] * 0.0
    for l in range(1, layer + 1):
        acc = acc + ex[l]
    return acc / tot


def _gla_kernel(*refs, n_groups, layer, has_state, hp):
    if has_state:
        (q_ref, v_ref, gate_ref, zf_ref, zb_ref, lbl_ref, g_ref, ids_ref, e_ref, s0_ref,
         o_ref, acc_ref) = refs
    else:
        (q_ref, v_ref, gate_ref, zf_ref, zb_ref, lbl_ref, g_ref, ids_ref, e_ref,
         o_ref, sout_ref, acc_ref) = refs
    e_mat = e_ref[...]
    z_refs = (zf_ref, zb_ref)
    chains = [(hh, d) for hh in range(hp) for d in range(2)]
    cols = [slice(hh * HD, (hh + 1) * HD) for hh in range(hp)]
    lbs = {(hh, d): _lower_bound(lbl_ref[d, :, cols[hh]], layer) for hh, d in chains}

    def step(gi, sts):
        rows, gens = [], []
        for d in range(2):
            g = gi if d == 0 else n_groups - 1 - gi
            start = g * GLA_GROUP
            rows.append(pl.ds(start if isinstance(start, int) else pl.multiple_of(start, GLA_GROUP), GLA_GROUP))
        for c, (hh, d) in enumerate(chains):
            gens.append(_gla_group(q_ref[rows[d], cols[hh]].astype(F32), v_ref[rows[d], cols[hh]],
                                   z_refs[d][rows[d], cols[hh]], lbs[hh, d], ids_ref[d], e_mat, sts[c],
                                   backward=(d == 1)))
        new = []
        for (hh, d), (o, st) in zip(chains, _interleave(gens)):
            acc_ref[rows[d], cols[hh]] = acc_ref[rows[d], cols[hh]] + o
            new.append(st)
        return tuple(new)

    acc_ref[...] = jnp.zeros_like(acc_ref)
    if has_state:
        init = tuple(s0_ref[d, hh].T for hh, d in chains)
        lax.fori_loop(0, n_groups, step, init)
    else:
        sts = step(0, (None,) * len(chains))
        for (hh, d), st in zip(chains, sts):
            sout_ref[d, hh] = st.T
    for hh in range(hp):
        o = acc_ref[:, cols[hh]]
        y = o * lax.rsqrt(jnp.mean(o * o, axis=-1, keepdims=True) + EPS) * g_ref[...]
        o_ref[:, cols[hh]] = (y * gate_ref[:, cols[hh]].astype(F32)).astype(BF16)


def _gla(q, v, gate, zf, zb, lb_logits, g_out, tabs, *, row0, nb, ln, layer, s0=None):
    ids, e_mat = tabs
    assert ln % GLA_GROUP == 0 and (s0 is not None or ln == GLA_GROUP)
    b0 = row0 // ln
    hp = GLA_HP if s0 is None else GLA_HP_CARRY
    w = hp * HD
    seq = pl.BlockSpec((ln, w), lambda b, h: (b + b0, h))
    in_specs = [seq] * 5 + [
        pl.BlockSpec((2, DEPTH, w), lambda b, h: (0, 0, h)),
        pl.BlockSpec((1, HD), lambda b, h: (0, 0)),
        pl.BlockSpec((2, GLA_GROUP, GLA_GROUP), lambda b, h: (0, 0, 0)),
        pl.BlockSpec((GLA_BLK * HD, GLA_GROUP), lambda b, h: (0, 0)),
    ]
    args = [q, v, gate, zf, zb, lb_logits, g_out, ids, e_mat]
    out_specs = [pl.BlockSpec((ln, w), lambda b, h: (b, h))]
    out_shape = [jax.ShapeDtypeStruct((nb * ln, D), BF16)]
    state_spec = pl.BlockSpec((None, 2, hp, HD, HD), lambda b, h: (b, 0, h, 0, 0))
    if s0 is not None:
        in_specs.append(state_spec)
        args.append(s0)
    else:
        out_specs.append(state_spec)
        out_shape.append(jax.ShapeDtypeStruct((nb, 2, HEADS, HD, HD), F32))
    kern = lambda *refs: _gla_kernel(*refs, n_groups=ln // GLA_GROUP, layer=layer, has_state=s0 is not None,
                                      hp=hp)
    return pl.pallas_call(
        kern,
        grid=(nb, HEADS // hp),
        in_specs=in_specs,
        out_specs=out_specs,
        out_shape=out_shape,
        scratch_shapes=[pltpu.VMEM((ln, w), F32)],
        compiler_params=_cparams(2),
        name="gla_sample" if s0 is not None else "gla_prompt",
    )(*args)


def _dft_tables(n):
    c = np.arange(FOUR_DG)
    ang_c = 2.0 * np.pi * ((c[:, None] * c[None, :]) % FOUR_DG) / FOUR_DG
    cs = np.concatenate([np.cos(ang_c), np.sin(ang_c)], axis=1)
    t = np.arange(n)
    ang_n = 2.0 * np.pi * ((t[:, None] * t[None, :]) % n) / n
    cn = np.concatenate([np.cos(ang_n), -np.sin(ang_n)], axis=1) / math.sqrt(n * FOUR_DG)
    return jnp.asarray(cs, F32).astype(BF16), jnp.asarray(cn, F32).astype(BF16)


def _fourier_kernel(x_ref, mod_ref, g_ref, cs_ref, cn_ref, o_ref, *, row_of, ln):
    m = mod_ref[pl.ds(row_of(pl.program_id(0)), 1), :]
    h = _adanorm(x_ref[...], g_ref[...], m[:, 0:D], m[:, D:2 * D]).astype(BF16)
    cs = cs_ref[...]
    cn = cn_ref[...]
    seqs = [slice(r, r + ln) for r in range(0, h.shape[0], ln)]
    for g in range(FOUR_GROUPS):
        gc = slice(g * FOUR_DG, (g + 1) * FOUR_DG)
        xcs = _dot(h[:, gc], cs)
        rhs = jnp.concatenate([jnp.concatenate([xcs[rs, 0:FOUR_DG], xcs[rs, FOUR_DG:]], axis=0) for rs in seqs],
                              axis=1).astype(BF16)
        f = _dot(cn, rhs)
        for s, rs in enumerate(seqs):
            o_ref[rs, gc] = f[:, s * FOUR_DG:(s + 1) * FOUR_DG].astype(BF16)


def _fourier(x, mod_l, g, *, row0, nb, ln, per_step, row_of):
    cs, cn = _dft_tables(ln)
    rows = per_step * ln
    b0 = row0 // rows
    kern = lambda *refs: _fourier_kernel(*refs, row_of=row_of, ln=ln)
    return pl.pallas_call(
        kern,
        grid=(nb // per_step,),
        in_specs=[pl.BlockSpec((rows, D), lambda b: (b + b0, 0)), _const_spec((MOD_ROWS, N_MOD * D)),
                  _const_spec((1, D)), _const_spec((FOUR_DG, 2 * FOUR_DG)), _const_spec((ln, 2 * ln))],
        out_specs=pl.BlockSpec((rows, D), lambda b: (b, 0)),
        out_shape=jax.ShapeDtypeStruct((nb * ln, D), BF16),
        compiler_params=_cparams(1),
        name="fourier_%d" % ln,
    )(x, mod_l, g, cs, cn)


FF_DEPTH_COLS = 4
FF_DEPTH_ROWS = 2


def _post_ffn_kernel(*refs, layer, wo_index, split_x, final):
    refs = list(refs)
    x_refs = [refs.pop(0) for _ in range(2 if split_x else 1)]
    op_ref, os_ref, mod_ref, wo_hbm, g_ref, win_hbm, wout_hbm = refs[:7]
    refs = refs[7:]
    gf_ref = refs.pop(0) if final else None
    out_refs = [refs.pop(0) for _ in range(2 if final else 1)]
    wo_bf, win_bf, wout_bf, stage_c, sem_c, stage_r, sem_r = refs
    n_chunks = D_FF // FF_CHUNK
    wo_jobs = _column_jobs(wo_hbm, wo_index, wo_bf, FF_CHUNK)
    in_jobs = []
    for c in range(n_chunks):
        for c0 in (c * FF_CHUNK, D_FF + c * FF_CHUNK):
            in_jobs.append((win_hbm.at[layer, :, pl.ds(c0, FF_CHUNK)], win_bf.at[:, pl.ds(c0, FF_CHUNK)]))
    cols = _WeightStream(wo_jobs + in_jobs, stage_c, sem_c)
    rows = _WeightStream([(wout_hbm.at[layer, pl.ds(c * FF_CHUNK, FF_CHUNK), :],
                           wout_bf.at[pl.ds(c * FF_CHUNK, FF_CHUNK), :]) for c in range(n_chunks)],
                         stage_r, sem_r)

    def body(first):
        if first:
            cols.prime()
            rows.prime()
        i = pl.program_id(0)
        ctx = i < NT_P
        m = mod_ref[pl.ds(_mod_row(i), 1), :]
        x = jnp.where(ctx, x_refs[0][...], x_refs[1][...]) if split_x else x_refs[0][...]
        o = jnp.where(ctx, op_ref[...], os_ref[...])
        if first:
            for k in range(len(wo_jobs)):
                cols.take(k)
        x = x + m[:, 2 * D:3 * D] * _dot(o, wo_bf[...])
        h = _adanorm(x, g_ref[...], m[:, 3 * D:4 * D], m[:, 4 * D:5 * D]).astype(BF16)
        acc = None
        for c in range(n_chunks):
            if first:
                cols.take(len(wo_jobs) + 2 * c)
                cols.take(len(wo_jobs) + 2 * c + 1)
                rows.take(c)
            gt = _dot(h, win_bf[:, c * FF_CHUNK:(c + 1) * FF_CHUNK])
            up = _dot(h, win_bf[:, D_FF + c * FF_CHUNK:D_FF + (c + 1) * FF_CHUNK])
            part = _dot((_silu(gt) * up).astype(BF16), wout_bf[c * FF_CHUNK:(c + 1) * FF_CHUNK, :])
            acc = part if acc is None else acc + part
        x = x + m[:, 5 * D:6 * D] * acc
        if not final:
            out_refs[0][...] = x
            return
        y = x * lax.rsqrt(jnp.mean(x * x, axis=-1, keepdims=True) + EPS) * gf_ref[...]
        if first:
            out_refs[0][...] = y
            return

        @pl.when(ctx)
        def _():
            out_refs[0][...] = y

        @pl.when(jnp.logical_not(ctx))
        def _():
            out_refs[1][...] = y

    _first_step_or_not(body)


def _post_ffn(xs, op, os_, mod_l, w_o, wo_index, g, w_in, w_out, layer, g_final=None):
    tile = pl.BlockSpec((TM, D), lambda i: (i, 0))
    p_tile = pl.BlockSpec((TM, D), lambda i: (jnp.minimum(i, NT_P - 1), 0))
    s_tile = pl.BlockSpec((TM, D), lambda i: (jnp.maximum(i - NT_P, 0), 0))
    split_x = len(xs) == 2
    final = g_final is not None
    in_specs = ([p_tile, s_tile] if split_x else [tile]) + [
        p_tile, s_tile, _const_spec((MOD_ROWS, N_MOD * D)), HBM_SPEC, _const_spec((1, D)), HBM_SPEC, HBM_SPEC]
    args = list(xs) + [op, os_, mod_l, w_o, g, w_in, w_out]
    if final:
        in_specs.append(_const_spec((1, D)))
        args.append(g_final)
        out_specs = [p_tile, s_tile]
        out_shape = [jax.ShapeDtypeStruct((N_P, D), F32), jax.ShapeDtypeStruct((N_S, D), F32)]
    else:
        out_specs = tile
        out_shape = jax.ShapeDtypeStruct((N_TOK, D), F32)
    kern = lambda *refs: _post_ffn_kernel(*refs, layer=layer, wo_index=wo_index, split_x=split_x, final=final)
    return pl.pallas_call(
        kern,
        grid=(N_TOK // TM,),
        in_specs=in_specs,
        out_specs=out_specs,
        out_shape=out_shape,
        scratch_shapes=([pltpu.VMEM((D, D), BF16), pltpu.VMEM((D, 2 * D_FF), BF16), pltpu.VMEM((D_FF, D), BF16)]
                        + _stream_scratch((D, FF_CHUNK), FF_DEPTH_COLS)
                        + _stream_scratch((FF_CHUNK, D), FF_DEPTH_ROWS)),
        compiler_params=_cparams(1),
        name="post_ffn",
    )(*args)


def kernel(x_prompt, x_sample, cache_attn_k, cache_attn_v, state_hgrn, c, c_ctx, w_ada, b_ada, g_norm_mix,
           g_norm_ffn, w_qkv_attn, lam_attn, g_subln_attn, w_o_attn, w_in_rec, lb_logits_rec, g_out_rec,
           w_o_rec, w_four, w_ffn_in, w_ffn_out, g_final):
    xs = (x_prompt.reshape(N_P, D), x_sample.reshape(N_S, D))
    cond = jnp.concatenate([c_ctx.reshape(1, D), c, jnp.zeros((MOD_ROWS - 1 - DEC_BATCH, D), F32)], axis=0)
    mod = _modulation(cond, w_ada, b_ada)
    cache_kt = jnp.transpose(cache_attn_k, (0, 1, 3, 4, 5, 2)).reshape(DEC_BATCH, -1, HEADS, HD, PAST)
    cache_v = cache_attn_v.reshape(DEC_BATCH, -1, PAST, D)
    rope_tabs = _rope_tables()
    gla_tabs = _gla_tables()
    gla_tabs = (gla_tabs[0], gla_tabs[1].astype(BF16))
    kv_ctx, new_s = None, []
    for i in range(DEPTH):
        kind, j = i % N_MIXERS, i // N_MIXERS
        g_mix = g_norm_mix[i].reshape(1, D)
        x_p = xs[0]
        x_s, s_tile0 = (xs[1], 0) if len(xs) == 2 else (xs[0], NT_P)
        if kind == 0:
            lam_init = 0.8 - 0.6 * math.exp(-0.3 * i)
            g_sub = g_subln_attn[j].reshape(1, HD)
            qs, ks, vs = _qkv_sample(x_s, s_tile0, mod[i], g_mix, w_qkv_attn, j, rope_tabs)
            os_ = _attn_sample(qs, ks, vs, cache_kt, cache_v, j, lam_attn[j], g_sub, lam_init)
            kt_all, v_all, op = _attn_prompt(x_p, mod[i], g_mix, w_qkv_attn, j, lam_attn[j], g_sub, lam_init,
                                             prev=kv_ctx, after=qs)
            kv_ctx = (kt_all, v_all)
            w_o = w_o_attn
        elif kind == 1:
            proj = _rec_proj(xs[0], mod[i], g_mix, w_in_rec, j)
            g_out = g_out_rec[j].reshape(1, HD)
            op, s_ctx = _gla(*proj, lb_logits_rec, g_out, gla_tabs, row0=0, nb=BATCH, ln=SEQ, layer=i)
            new_s.append(s_ctx)
            os_ = _gla(*proj, lb_logits_rec, g_out, gla_tabs, row0=N_P, nb=DEC_BATCH, ln=DEC_SEQ, layer=i,
                       s0=state_hgrn[:, j])[0]
            w_o = w_o_rec
        else:
            op = _fourier(xs[0], mod[i], g_mix, row0=0, nb=BATCH, ln=SEQ, per_step=FOUR_CTX_PER_STEP,
                          row_of=lambda b: 0)
            os_ = _fourier(xs[0], mod[i], g_mix, row0=N_P, nb=DEC_BATCH, ln=DEC_SEQ, per_step=1,
                           row_of=lambda b: 1 + b)
            w_o = w_four
        out = _post_ffn(xs, op, os_, mod[i], w_o, j, g_norm_ffn[i].reshape(1, D), w_ffn_in, w_ffn_out, i,
                        g_final.reshape(1, D) if i == DEPTH - 1 else None)
        xs = tuple(out) if i == DEPTH - 1 else (out,)
    y_prompt = xs[0].reshape(BATCH, SEQ, D)
    y_sample = xs[1].reshape(DEC_BATCH, DEC_SEQ, D)
    new_k = jnp.transpose(kv_ctx[0].reshape(BATCH, -1, HEADS, 2, DH, SEQ), (0, 1, 5, 2, 3, 4))
    new_v = kv_ctx[1].reshape(BATCH, -1, SEQ, HEADS, HD)
    return (y_prompt, y_sample, new_k, new_v, jnp.stack(new_s, axis=1))
```

```python
import math

import numpy as np
import jax
import jax.numpy as jnp
from jax import lax
from jax.experimental import pallas as pl
from jax.experimental.pallas import tpu as pltpu

F32 = jnp.float32
BF16 = jnp.bfloat16

D = 1024
BATCH = 16
SEQ = 256
DEPTH = 4
DEC_BATCH = 4
DEC_SEQ = 1024
PAST = 512
GRID_W = 64
N_MIXERS = 3
HEADS = 8
DH = 64
HD = 128
ROPE_THETA = 10000.0
D_FF = 2816
N_MOD = 6
EPS = 1e-6
LOG2E = 1.0 / math.log(2.0)
FOUR_GROUPS = 4
FOUR_DG = 256

N_P = BATCH * SEQ
N_S = DEC_BATCH * DEC_SEQ
N_TOK = N_P + N_S
TM = 512
NT_P = N_P // TM
NT_S = N_S // TM
TILES_PER_DEC_SEQ = DEC_SEQ // TM
SEQ_PER_TILE = TM // SEQ
MOD_ROWS = 8
FF_CHUNK = 256
ATT_TQ = 512
ATT_HP = 2
ATT_P_WIDTH = 2
ATT_SEQ_SKEW = 5
ATT_SKEW = 1
FOUR_CTX_PER_STEP = 4
GLA_HP = 8
GLA_HP_CARRY = 4
GLA_GROUP = 256
GLA_BLK = 8
GLA_LEVELS = (8, 16, 32, 64, 128)
VMEM_LIMIT = 60 * 1024 * 1024


def _cparams(n_axes):
    return pltpu.CompilerParams(dimension_semantics=("arbitrary",) * n_axes,
                                vmem_limit_bytes=VMEM_LIMIT)


def _const_spec(shape):
    nd = len(shape)
    return pl.BlockSpec(shape, lambda *_: (0,) * nd, pipeline_mode=pl.Buffered(1))


def _sigmoid(x):
    return 1.0 / (1.0 + jnp.exp(-x))


def _silu(x):
    return x * _sigmoid(x)


def _adanorm(x, g, shift, scale):
    ms = jnp.mean(x * x, axis=-1, keepdims=True)
    return x * lax.rsqrt(ms + EPS) * (g * (1.0 + scale)) + shift


def _mod_row(tile):
    return jnp.where(tile < NT_P, 0, 1 + (tile - NT_P) // TILES_PER_DEC_SEQ)


def _dot(a, b):
    return jnp.dot(a, b, preferred_element_type=F32)


def _dot_nt(a, b):
    return lax.dot_general(a, b, (((1,), (1,)), ((), ())), preferred_element_type=F32)


def _dot_tn(a, b):
    return lax.dot_general(a, b, (((0,), (0,)), ((), ())), preferred_element_type=F32)


def _split2(x):
    hi = x.astype(BF16)
    return hi, (x - hi.astype(F32)).astype(BF16)


def _interleaved(generators, skew=0):
    results = [None] * len(generators)
    active = list(range(len(generators)))
    rnd = 0
    while active:
        for i in list(active):
            if rnd < i * skew:
                continue
            try:
                next(generators[i])
            except StopIteration as stop:
                results[i] = stop.value
                active.remove(i)
        rnd += 1
        yield
    return results


def _interleave(generators, skew=0):
    rounds = _interleaved(generators, skew)
    while True:
        try:
            next(rounds)
        except StopIteration as stop:
            return stop.value


class _WeightStream:
    def __init__(self, jobs, stage_ref, sem_ref):
        self.jobs, self.stage, self.sem = jobs, stage_ref, sem_ref
        self.depth = stage_ref.shape[0]

    def _copy(self, k):
        slot = k % self.depth
        return pltpu.make_async_copy(self.jobs[k][0], self.stage.at[slot], self.sem.at[slot])

    def prime(self):
        for k in range(min(self.depth, len(self.jobs))):
            self._copy(k).start()

    def take(self, k):
        self._copy(k).wait()
        self.jobs[k][1][...] = self.stage[k % self.depth].astype(BF16)
        if k + self.depth < len(self.jobs):
            self._copy(k + self.depth).start()


def _stream_scratch(piece_shape, depth):
    return [pltpu.VMEM((depth,) + tuple(piece_shape), F32), pltpu.SemaphoreType.DMA((depth,))]


def _column_jobs(w_hbm, index, w_bf, width):
    return [(w_hbm.at[index, :, pl.ds(c, width)], w_bf.at[:, pl.ds(c, width)])
            for c in range(0, w_bf.shape[1], width)]


def _first_step_or_not(body):
    first = pl.program_id(0) == 0
    pl.when(first)(lambda: body(True))
    pl.when(jnp.logical_not(first))(lambda: body(False))


HBM_SPEC = pl.BlockSpec(memory_space=pl.ANY)
W_PIECE = 512
W_DEPTH = 3


MOD_TK = 512


def _mod_kernel(cond_ref, w_ref, b_ref, o_ref):
    kk = pl.program_id(1)
    s = _silu(cond_ref[...])
    s_hi = s.astype(BF16)
    s_lo = (s - s_hi.astype(F32)).astype(BF16)
    w = w_ref[0]
    w_hi = w.astype(BF16)
    w_lo = (w - w_hi.astype(F32)).astype(BF16)
    r = _dot(jnp.concatenate([s_hi, s_lo], axis=0), w_hi)
    part = r[:MOD_ROWS] + r[MOD_ROWS:] + _dot(s_hi, w_lo)

    @pl.when(kk == 0)
    def _():
        o_ref[0] = part + b_ref[0]

    @pl.when(kk != 0)
    def _():
        o_ref[0] = o_ref[0] + part


def _modulation(cond, w_ada, b_ada):
    return pl.pallas_call(
        _mod_kernel,
        grid=(DEPTH, D // MOD_TK),
        in_specs=[
            pl.BlockSpec((MOD_ROWS, MOD_TK), lambda l, k: (0, k)),
            pl.BlockSpec((1, MOD_TK, N_MOD * D), lambda l, k: (l, k, 0)),
            pl.BlockSpec((1, 1, N_MOD * D), lambda l, k: (l, 0, 0)),
        ],
        out_specs=pl.BlockSpec((1, MOD_ROWS, N_MOD * D), lambda l, k: (l, 0, 0)),
        out_shape=jax.ShapeDtypeStruct((DEPTH, MOD_ROWS, N_MOD * D), F32),
        compiler_params=_cparams(2),
        name="modulation",
    )(cond, w_ada, b_ada.reshape(DEPTH, 1, N_MOD * D))


def _rope_tables():
    t = np.arange(DEC_SEQ)
    row, col = t // GRID_W, t % GRID_W
    lane = np.arange(HD) % DH
    use_row = lane < DH // 2
    idx = (lane % (DH // 2)) % (DH // 4)
    inv = ROPE_THETA ** (-idx.astype(np.float64) / (DH // 4))
    pos = np.where(use_row[None, :], row[:, None], col[:, None]).astype(np.float64)
    ang = pos * inv[None, :]
    first = (lane % (DH // 2)) < DH // 4
    cos = np.cos(ang)
    sin = np.sin(ang)
    sin_minus = np.where(first[None, :], -sin, 0.0)
    sin_plus = np.where(first[None, :], 0.0, sin)
    return (jnp.asarray(cos, F32), jnp.asarray(sin_minus, F32), jnp.asarray(sin_plus, F32))


def _rope_tile(t, cos, sin_minus, sin_plus):
    return (t * cos + pltpu.roll(t, HD - DH // 4, 1) * sin_minus
            + pltpu.roll(t, DH // 4, 1) * sin_plus)


def _lambda(lam_ref, lam_init):
    lp = lam_ref[...]
    return (jnp.exp(jnp.sum(lp[0:1] * lp[1:2], axis=-1, keepdims=True))
            - jnp.exp(jnp.sum(lp[2:3] * lp[3:4], axis=-1, keepdims=True)) + lam_init)


def _with_ones(v):
    return jnp.concatenate([v, jnp.ones_like(v)], axis=1)


def _diff_attn_head(q, score_fns, values, lam, g_scaled):
    tq = q.shape[0]
    lane = lax.broadcasted_iota(jnp.int32, q.shape, 1)
    zero = jnp.zeros_like(q)
    qq = jnp.concatenate([jnp.where(lane < DH, q, zero), jnp.where(lane >= DH, q, zero)], axis=0)
    scores = [fn(qq) for fn in score_fns]
    yield
    mx = scores[0].max(axis=-1, keepdims=True)
    for s in scores[1:]:
        mx = jnp.maximum(mx, s.max(axis=-1, keepdims=True))
    yield
    acc = None
    for s, v in zip(scores, values):
        c = _dot(jnp.exp2(s - mx).astype(BF16), v)
        acc = c if acc is None else acc + c
        yield
    r = acc[:, 0:HD] / acc[:, HD:2 * HD]
    o = r[:tq] - lam * r[tq:]
    o = o * lax.rsqrt(jnp.mean(o * o, axis=-1, keepdims=True) + EPS) * g_scaled
    return o.astype(BF16)


def _attn_prompt_kernel(*refs, j, n_layers, lam_init):
    x_ref, mod_ref, g_ref, w_hbm, lam_ref, gsub_ref = refs[:6]
    kt_ref, v_ref, o_ref, w_bf, stage, sem = refs[-6:]
    owns_all = j == 0
    stream = _WeightStream(_column_jobs(w_hbm, j, w_bf, W_PIECE), stage, sem)
    per_mat = D // W_PIECE

    def body(first):
        if first:
            stream.prime()
        m = mod_ref[pl.ds(0, 1), :]
        lam = _lambda(lam_ref, lam_init)
        g_scaled = gsub_ref[...] * (1.0 - lam_init)

        def sequence(s, loads):
            rows = slice(s * SEQ, (s + 1) * SEQ)
            h = _adanorm(x_ref[rows, :], g_ref[...], m[:, 0:D], m[:, D:2 * D]).astype(BF16)
            yield
            mats = []
            for n in range(3):
                if loads:
                    for p in range(per_mat):
                        stream.take(n * per_mat + p)
                mats.append(_dot(h, w_bf[:, n * D:(n + 1) * D]))
                yield
            q = (mats[0] * (DH ** -0.5 * LOG2E)).astype(BF16)
            k, v = mats[1], mats[2]
            kb, vb = k.astype(BF16), v.astype(BF16)
            if owns_all:
                for jj in range(n_layers):
                    kt_ref[s, jj] = k.T if jj == j else jnp.zeros((D, SEQ), F32)
                    v_ref[s, jj] = v if jj == j else jnp.zeros((SEQ, D), F32)
            else:
                kt_ref[s] = k.T
                v_ref[s] = v
            yield
            for h0 in range(0, HEADS, ATT_P_WIDTH):
                heads = []
                for hh in range(h0, h0 + ATT_P_WIDTH):
                    cols = slice(hh * HD, (hh + 1) * HD)
                    kh = kb[:, cols]
                    heads.append(_diff_attn_head(q[:, cols], [lambda qq, kh=kh: _dot_nt(qq, kh)],
                                                 [_with_ones(vb[:, cols])], lam, g_scaled))
                outs = yield from _interleaved(heads, ATT_SKEW)
                for hh, o in zip(range(h0, h0 + ATT_P_WIDTH), outs):
                    o_ref[rows, hh * HD:(hh + 1) * HD] = o

        _interleave([sequence(s, first and s == 0) for s in range(SEQ_PER_TILE)], skew=ATT_SEQ_SKEW)

    _first_step_or_not(body)


def _attn_prompt(x, mod_l, g, w_qkv, j, lam_p, g_sub, lam_init, prev=None, after=None):
    n_layers = w_qkv.shape[0]
    assert (prev is None) == (j == 0)
    tile = pl.BlockSpec((TM, D), lambda i: (i, 0))
    in_specs = [tile, _const_spec((MOD_ROWS, N_MOD * D)), _const_spec((1, D)), HBM_SPEC,
                _const_spec((4, DH)), _const_spec((1, HD))]
    args = [x, mod_l, g, w_qkv, lam_p, g_sub]
    if prev is None:
        kt_spec = pl.BlockSpec((SEQ_PER_TILE, n_layers, D, SEQ), lambda i: (i, 0, 0, 0))
        v_spec = pl.BlockSpec((SEQ_PER_TILE, n_layers, SEQ, D), lambda i: (i, 0, 0, 0))
        aliases = {}
    else:
        kt_spec = pl.BlockSpec((SEQ_PER_TILE, None, D, SEQ), lambda i: (i, j, 0, 0))
        v_spec = pl.BlockSpec((SEQ_PER_TILE, None, SEQ, D), lambda i: (i, j, 0, 0))
        aliases = {len(args): 0, len(args) + 1: 1}
        in_specs += [HBM_SPEC, HBM_SPEC]
        args += list(prev)
    if after is not None:
        in_specs.append(HBM_SPEC)
        args.append(after)
    kern = lambda *refs: _attn_prompt_kernel(*refs, j=j, n_layers=n_layers, lam_init=lam_init)
    return pl.pallas_call(
        kern,
        grid=(NT_P,),
        in_specs=in_specs,
        out_specs=[kt_spec, v_spec, tile],
        out_shape=[jax.ShapeDtypeStruct((BATCH, n_layers, D, SEQ), F32),
                   jax.ShapeDtypeStruct((BATCH, n_layers, SEQ, D), F32), jax.ShapeDtypeStruct((N_P, D), BF16)],
        input_output_aliases=aliases,
        scratch_shapes=[pltpu.VMEM((D, 3 * D), BF16)] + _stream_scratch((D, W_PIECE), W_DEPTH),
        compiler_params=_cparams(1),
        name="attn_prompt",
    )(*args)


def _qkv_sample_kernel(x_ref, mod_ref, g_ref, w_hbm, cos_ref, sm_ref, sp_ref, q_ref, k_ref, v_ref,
                       w_bf, stage, sem, *, j):
    stream = _WeightStream(_column_jobs(w_hbm, j, w_bf, W_PIECE), stage, sem)
    per_mat = D // W_PIECE

    def body(first):
        if first:
            stream.prime()
        i = pl.program_id(0)
        m = mod_ref[pl.ds(1 + i // TILES_PER_DEC_SEQ, 1), :]

        def half(rows, loads):
            h = _adanorm(x_ref[rows, :], g_ref[...], m[:, 0:D], m[:, D:2 * D]).astype(BF16)
            cos, sm, sp = cos_ref[rows, :], sm_ref[rows, :], sp_ref[rows, :]
            yield
            for n, (o_ref, scale) in enumerate(((q_ref, DH ** -0.5 * LOG2E), (k_ref, None), (v_ref, None))):
                if loads:
                    for p in range(per_mat):
                        stream.take(n * per_mat + p)
                y = _dot(h, w_bf[:, n * D:(n + 1) * D])
                if o_ref is v_ref:
                    o_ref[rows, :] = y.astype(BF16)
                else:
                    if scale is not None:
                        y = y * scale
                    for hh in range(HEADS):
                        t = _rope_tile(y[:, hh * HD:(hh + 1) * HD], cos, sm, sp)
                        o_ref[rows, hh * HD:(hh + 1) * HD] = t.astype(BF16)
                yield

        _interleave([half(slice(r, r + TM // 2), first and r == 0) for r in (0, TM // 2)], skew=1)

    _first_step_or_not(body)


def _qkv_sample(x, tile0, mod_l, g, w_qkv, j, tabs):
    tile = pl.BlockSpec((TM, D), lambda i: (i, 0))
    tab = pl.BlockSpec((TM, HD), lambda i: (i % TILES_PER_DEC_SEQ, 0))
    kern = lambda *refs: _qkv_sample_kernel(*refs, j=j)
    return pl.pallas_call(
        kern,
        grid=(NT_S,),
        in_specs=[pl.BlockSpec((TM, D), lambda i: (i + tile0, 0)), _const_spec((MOD_ROWS, N_MOD * D)),
                  _const_spec((1, D)), HBM_SPEC, tab, tab, tab],
        out_specs=[tile, tile, tile],
        out_shape=[jax.ShapeDtypeStruct((N_S, D), BF16)] * 3,
        scratch_shapes=[pltpu.VMEM((D, 3 * D), BF16)] + _stream_scratch((D, W_PIECE), W_DEPTH),
        compiler_params=_cparams(1),
        name="qkv_sample",
    )(x, mod_l, g, w_qkv, *tabs)


def _attn_sample_kernel(*refs, j, lam_init):
    q_ref, kn_ref, vn_ref, kct_ref, vc_hbm, lam_ref, gsub_ref, o_ref, vc_buf, vc_sem = refs
    b, i = pl.program_id(0), pl.program_id(1)
    nb, nq = pl.num_programs(0), pl.num_programs(1)

    def vc_copies(req):
        slot = req % 2
        return [pltpu.make_async_copy(vc_hbm.at[req, j, :, hh, :], vc_buf.at[slot, hh], vc_sem.at[slot, hh])
                for hh in range(HEADS)]

    @pl.when(jnp.logical_and(b == 0, i == 0))
    def _():
        for cp in vc_copies(b):
            cp.start()

    @pl.when(i == 0)
    def _():
        for cp in vc_copies(b):
            cp.wait()

    @pl.when(jnp.logical_and(i == nq - 1, b + 1 < nb))
    def _():
        for cp in vc_copies(b + 1):
            cp.start()

    slot = b % 2
    lam = _lambda(lam_ref, lam_init)
    g_scaled = gsub_ref[...] * (1.0 - lam_init)
    for h0 in range(0, HEADS, ATT_HP):
        heads = []
        for hh in range(h0, h0 + ATT_HP):
            cols = slice(hh * HD, (hh + 1) * HD)
            kn = kn_ref[:, cols]
            kct = kct_ref[hh].astype(BF16)
            vc = vc_buf[slot, hh].astype(BF16)
            heads.append(_diff_attn_head(
                q_ref[:, cols], [lambda qq, kn=kn: _dot_nt(qq, kn), lambda qq, kct=kct: _dot(qq, kct)],
                [_with_ones(vn_ref[:, cols]), _with_ones(vc)], lam, g_scaled))
        for hh, o in zip(range(h0, h0 + ATT_HP), _interleave(heads, ATT_SKEW)):
            o_ref[:, hh * HD:(hh + 1) * HD] = o


def _attn_sample(q, kn, vn, cache_kt, cache_v, j, lam_p, g_sub, lam_init):
    nq = DEC_SEQ // ATT_TQ
    seq = pl.BlockSpec((DEC_SEQ, D), lambda b, i: (b, 0))
    blk = pl.BlockSpec((ATT_TQ, D), lambda b, i: (b * nq + i, 0))
    in_specs = [blk, seq, seq,
                pl.BlockSpec((None, None, HEADS, HD, PAST), lambda b, i: (b, j, 0, 0, 0)),
                HBM_SPEC,
                pl.BlockSpec((4, DH), lambda b, i: (0, 0)), pl.BlockSpec((1, HD), lambda b, i: (0, 0))]
    kern = lambda *refs: _attn_sample_kernel(*refs, j=j, lam_init=lam_init)
    return pl.pallas_call(
        kern,
        grid=(DEC_BATCH, nq),
        in_specs=in_specs,
        out_specs=blk,
        out_shape=jax.ShapeDtypeStruct((N_S, D), BF16),
        scratch_shapes=[pltpu.VMEM((2, HEADS, PAST, HD), F32), pltpu.SemaphoreType.DMA((2, HEADS))],
        compiler_params=_cparams(2),
        name="attn_sample",
    )(q, kn, vn, cache_kt, cache_v, lam_p, g_sub)


def _rec_proj_kernel(x_ref, mod_ref, g_ref, w_hbm, q_ref, v_ref, gate_ref, zf_ref, zb_ref,
                     w_bf, stage, sem, *, j):
    stream = _WeightStream(_column_jobs(w_hbm, j, w_bf, W_PIECE), stage, sem)
    per_mat = D // W_PIECE

    def body(first):
        if first:
            stream.prime()
        m = mod_ref[pl.ds(_mod_row(pl.program_id(0)), 1), :]

        def half(rows, loads):
            h = _adanorm(x_ref[rows, :], g_ref[...], m[:, 0:D], m[:, D:2 * D]).astype(BF16)
            yield
            for n, o_ref in enumerate((q_ref, v_ref, gate_ref, zf_ref, zb_ref)):
                if loads:
                    for p in range(per_mat):
                        stream.take(n * per_mat + p)
                y = _dot(h, w_bf[:, n * D:(n + 1) * D])
                if o_ref is q_ref or o_ref is gate_ref:
                    y = _silu(y)
                o_ref[rows, :] = y.astype(o_ref.dtype)
                yield

        _interleave([half(slice(r, r + TM // 2), first and r == 0) for r in (0, TM // 2)], skew=1)

    _first_step_or_not(body)


def _rec_proj(x, mod_l, g, w_in, j):
    tile = pl.BlockSpec((TM, D), lambda i: (i, 0))
    kern = lambda *refs: _rec_proj_kernel(*refs, j=j)
    return pl.pallas_call(
        kern,
        grid=(N_TOK // TM,),
        in_specs=[tile, _const_spec((MOD_ROWS, N_MOD * D)), _const_spec((1, D)), HBM_SPEC],
        out_specs=[tile] * 5,
        out_shape=[jax.ShapeDtypeStruct((N_TOK, D), BF16)] * 3 + [jax.ShapeDtypeStruct((N_TOK, D), F32)] * 2,
        scratch_shapes=[pltpu.VMEM((D, 5 * D), BF16)] + _stream_scratch((D, W_PIECE), W_DEPTH),
        compiler_params=_cparams(1),
        name="rec_proj",
    )(x, mod_l, g, w_in)


def _gla_tables():
    t = np.arange(GLA_GROUP)[:, None]
    s = np.arange(GLA_GROUP)[None, :]
    ids = np.full((GLA_GROUP, GLA_GROUP), -1, np.int32)
    for l, b in reversed(list(enumerate(GLA_LEVELS, 1))):
        ids = np.where((t // (2 * b) == s // (2 * b)) & (s < t), l, ids)
    ids = np.where((t // GLA_BLK == s // GLA_BLK) & (s <= t), 0, ids)
    ids = np.stack([ids, ids.T]).astype(np.int32)
    e = np.zeros((GLA_BLK * HD, GLA_GROUP), np.float32)
    for j in range(GLA_BLK):
        e[j * HD:(j + 1) * HD, j::GLA_BLK] = 1.0
    return jnp.asarray(ids), jnp.asarray(e)


def _bcast_row(x, blk, idx):
    n, w = x.shape
    r = x.reshape(n // blk, blk, w)[:, idx:idx + 1, :]
    return jnp.broadcast_to(r, (n // blk, blk, w)).reshape(n, w)


def _gla_group(q, v, z, lb, ids, e_mat, st, backward):
    n = GLA_GROUP
    sig = 0.5 * jnp.tanh(0.5 * z) + 0.5
    f = lb + (1.0 - lb) * sig
    k = (1.0 - lb) * (1.0 - sig)
    lf = jnp.log(f) * LOG2E
    row = lax.broadcasted_iota(jnp.int32, (n, n), 0)
    col = lax.broadcasted_iota(jnp.int32, (n, n), 1)
    tri = jnp.where((col >= row) if backward else (col <= row), 1.0, 0.0).astype(BF16)
    c2 = _dot(tri, jnp.concatenate(_split2(lf), axis=1))
    cum = c2[:, 0:HD] + c2[:, HD:2 * HD]
    ck = jnp.log(k) * LOG2E - cum
    yield

    prods = []
    for j in range(GLA_BLK):
        prods.append((q * jnp.exp2(jnp.minimum(cum + _bcast_row(ck, GLA_BLK, j), 0.0))).astype(BF16))
        yield
    half = n // 2
    ids_q = ids[0:half, 0:half]
    diag = lambda x, h: x[h * half:(h + 1) * half, h * half:(h + 1) * half]
    pair = _dot(jnp.concatenate(prods, axis=1), e_mat)
    quads = [jnp.where(ids_q == 0, diag(pair, h), 0.0) for h in range(2)]
    yield

    for l, b in enumerate(GLA_LEVELS[:-1], 1):
        edge = _bcast_row(cum, 2 * b, b if backward else b - 1)
        ql = (q * jnp.exp2(cum - edge)).astype(BF16)
        kl = jnp.exp2(edge + ck).astype(BF16)
        x = _dot_nt(ql, kl)
        quads = [jnp.where(ids_q == l, diag(x, h), quads[h]) for h in range(2)]
        yield

    assert GLA_LEVELS[-1] == half
    lo, hi = slice(0, half), slice(half, n)
    qs, ks = (lo, hi) if backward else (hi, lo)
    edge = cum[half:half + 1, :] if backward else cum[half - 1:half, :]
    cross = _dot_nt((q[qs] * jnp.exp2(cum[qs] - edge)).astype(BF16),
                    jnp.exp2(edge + ck[ks]).astype(BF16)).astype(BF16)
    o_halves = [_dot(quads[h].astype(BF16), v[h * half:(h + 1) * half]) for h in range(2)]
    o_halves[0 if backward else 1] += _dot(cross, v[ks])
    o = jnp.concatenate(o_halves, axis=0)
    yield

    total = cum[0:1, :] if backward else cum[n - 1:n, :]
    if st is not None:
        o = o + _dot_nt((q * jnp.exp2(cum)).astype(BF16), st.astype(BF16))
    kdec = jnp.exp2(total + ck).astype(BF16)
    upd = _dot_tn(v, kdec)
    st_new = upd if st is None else st * jnp.exp2(total) + upd
    return o, st_new


def _lower_bound(x, layer):
    rows = [x[l:l + 1, :] for l in range(DEPTH)]
    mx = rows[0]
    for r in rows[1:]:
        mx = jnp.maximum(mx, r)
    ex = [jnp.exp(r - mx) for r in rows]
    tot = ex[0]
    for e in ex[1:]:
        tot = tot + e
    acc = ex[0] * 0.0
    for l in range(1, layer + 1):
        acc = acc + ex[l]
    return acc / tot


def _gla_kernel(*refs, n_groups, layer, has_state, hp):
    if has_state:
        (q_ref, v_ref, gate_ref, zf_ref, zb_ref, lbl_ref, g_ref, ids_ref, e_ref, s0_ref,
         o_ref, acc_ref) = refs
    else:
        (q_ref, v_ref, gate_ref, zf_ref, zb_ref, lbl_ref, g_ref, ids_ref, e_ref,
         o_ref, sout_ref, acc_ref) = refs
    e_mat = e_ref[...]
    z_refs = (zf_ref, zb_ref)
    chains = [(hh, d) for hh in range(hp) for d in range(2)]
    cols = [slice(hh * HD, (hh + 1) * HD) for hh in range(hp)]
    lbs = {(hh, d): _lower_bound(lbl_ref[d, :, cols[hh]], layer) for hh, d in chains}

    def step(gi, sts):
        rows, gens = [], []
        for d in range(2):
            g = gi if d == 0 else n_groups - 1 - gi
            start = g * GLA_GROUP
            rows.append(pl.ds(start if isinstance(start, int) else pl.multiple_of(start, GLA_GROUP), GLA_GROUP))
        for c, (hh, d) in enumerate(chains):
            gens.append(_gla_group(q_ref[rows[d], cols[hh]].astype(F32), v_ref[rows[d], cols[hh]],
                                   z_refs[d][rows[d], cols[hh]], lbs[hh, d], ids_ref[d], e_mat, sts[c],
                                   backward=(d == 1)))
        new = []
        for (hh, d), (o, st) in zip(chains, _interleave(gens)):
            acc_ref[rows[d], cols[hh]] = acc_ref[rows[d], cols[hh]] + o
            new.append(st)
        return tuple(new)

    acc_ref[...] = jnp.zeros_like(acc_ref)
    if has_state:
        init = tuple(s0_ref[d, hh].T for hh, d in chains)
        lax.fori_loop(0, n_groups, step, init)
    else:
        sts = step(0, (None,) * len(chains))
        for (hh, d), st in zip(chains, sts):
            sout_ref[d, hh] = st.T
    for hh in range(hp):
        o = acc_ref[:, cols[hh]]
        y = o * lax.rsqrt(jnp.mean(o * o, axis=-1, keepdims=True) + EPS) * g_ref[...]
        o_ref[:, cols[hh]] = (y * gate_ref[:, cols[hh]].astype(F32)).astype(BF16)


def _gla(q, v, gate, zf, zb, lb_logits, g_out, tabs, *, row0, nb, ln, layer, s0=None):
    ids, e_mat = tabs
    assert ln % GLA_GROUP == 0 and (s0 is not None or ln == GLA_GROUP)
    b0 = row0 // ln
    hp = GLA_HP if s0 is None else GLA_HP_CARRY
    w = hp * HD
    seq = pl.BlockSpec((ln, w), lambda b, h: (b + b0, h))
    in_specs = [seq] * 5 + [
        pl.BlockSpec((2, DEPTH, w), lambda b, h: (0, 0, h)),
        pl.BlockSpec((1, HD), lambda b, h: (0, 0)),
        pl.BlockSpec((2, GLA_GROUP, GLA_GROUP), lambda b, h: (0, 0, 0)),
        pl.BlockSpec((GLA_BLK * HD, GLA_GROUP), lambda b, h: (0, 0)),
    ]
    args = [q, v, gate, zf, zb, lb_logits, g_out, ids, e_mat]
    out_specs = [pl.BlockSpec((ln, w), lambda b, h: (b, h))]
    out_shape = [jax.ShapeDtypeStruct((nb * ln, D), BF16)]
    state_spec = pl.BlockSpec((None, 2, hp, HD, HD), lambda b, h: (b, 0, h, 0, 0))
    if s0 is not None:
        in_specs.append(state_spec)
        args.append(s0)
    else:
        out_specs.append(state_spec)
        out_shape.append(jax.ShapeDtypeStruct((nb, 2, HEADS, HD, HD), F32))
    kern = lambda *refs: _gla_kernel(*refs, n_groups=ln // GLA_GROUP, layer=layer, has_state=s0 is not None,
                                      hp=hp)
    return pl.pallas_call(
        kern,
        grid=(nb, HEADS // hp),
        in_specs=in_specs,
        out_specs=out_specs,
        out_shape=out_shape,
        scratch_shapes=[pltpu.VMEM((ln, w), F32)],
        compiler_params=_cparams(2),
        name="gla_sample" if s0 is not None else "gla_prompt",
    )(*args)


def _dft_tables(n):
    c = np.arange(FOUR_DG)
    ang_c = 2.0 * np.pi * ((c[:, None] * c[None, :]) % FOUR_DG) / FOUR_DG
    cs = np.concatenate([np.cos(ang_c), np.sin(ang_c)], axis=1)
    t = np.arange(n)
    ang_n = 2.0 * np.pi * ((t[:, None] * t[None, :]) % n) / n
    cn = np.concatenate([np.cos(ang_n), -np.sin(ang_n)], axis=1) / math.sqrt(n * FOUR_DG)
    return jnp.asarray(cs, F32).astype(BF16), jnp.asarray(cn, F32).astype(BF16)


def _fourier_kernel(x_ref, mod_ref, g_ref, cs_ref, cn_ref, o_ref, *, row_of, ln):
    m = mod_ref[pl.ds(row_of(pl.program_id(0)), 1), :]
    h = _adanorm(x_ref[...], g_ref[...], m[:, 0:D], m[:, D:2 * D]).astype(BF16)
    cs = cs_ref[...]
    cn = cn_ref[...]
    seqs = [slice(r, r + ln) for r in range(0, h.shape[0], ln)]
    for g in range(FOUR_GROUPS):
        gc = slice(g * FOUR_DG, (g + 1) * FOUR_DG)
        xcs = _dot(h[:, gc], cs)
        rhs = jnp.concatenate([jnp.concatenate([xcs[rs, 0:FOUR_DG], xcs[rs, FOUR_DG:]], axis=0) for rs in seqs],
                              axis=1).astype(BF16)
        f = _dot(cn, rhs)
        for s, rs in enumerate(seqs):
            o_ref[rs, gc] = f[:, s * FOUR_DG:(s + 1) * FOUR_DG].astype(BF16)


def _fourier(x, mod_l, g, *, row0, nb, ln, per_step, row_of):
    cs, cn = _dft_tables(ln)
    rows = per_step * ln
    b0 = row0 // rows
    kern = lambda *refs: _fourier_kernel(*refs, row_of=row_of, ln=ln)
    return pl.pallas_call(
        kern,
        grid=(nb // per_step,),
        in_specs=[pl.BlockSpec((rows, D), lambda b: (b + b0, 0)), _const_spec((MOD_ROWS, N_MOD * D)),
                  _const_spec((1, D)), _const_spec((FOUR_DG, 2 * FOUR_DG)), _const_spec((ln, 2 * ln))],
        out_specs=pl.BlockSpec((rows, D), lambda b: (b, 0)),
        out_shape=jax.ShapeDtypeStruct((nb * ln, D), BF16),
        compiler_params=_cparams(1),
        name="fourier_%d" % ln,
    )(x, mod_l, g, cs, cn)


FF_DEPTH_COLS = 4
FF_DEPTH_ROWS = 2


def _post_ffn_kernel(*refs, layer, wo_index, split_x, final):
    refs = list(refs)
    x_refs = [refs.pop(0) for _ in range(2 if split_x else 1)]
    op_ref, os_ref, mod_ref, wo_hbm, g_ref, win_hbm, wout_hbm = refs[:7]
    refs = refs[7:]
    gf_ref = refs.pop(0) if final else None
    out_refs = [refs.pop(0) for _ in range(2 if final else 1)]
    wo_bf, win_bf, wout_bf, stage_c, sem_c, stage_r, sem_r = refs
    n_chunks = D_FF // FF_CHUNK
    wo_jobs = _column_jobs(wo_hbm, wo_index, wo_bf, FF_CHUNK)
    in_jobs = []
    for c in range(n_chunks):
        for c0 in (c * FF_CHUNK, D_FF + c * FF_CHUNK):
            in_jobs.append((win_hbm.at[layer, :, pl.ds(c0, FF_CHUNK)], win_bf.at[:, pl.ds(c0, FF_CHUNK)]))
    cols = _WeightStream(wo_jobs + in_jobs, stage_c, sem_c)
    rows = _WeightStream([(wout_hbm.at[layer, pl.ds(c * FF_CHUNK, FF_CHUNK), :],
                           wout_bf.at[pl.ds(c * FF_CHUNK, FF_CHUNK), :]) for c in range(n_chunks)],
                         stage_r, sem_r)

    def body(first):
        if first:
            cols.prime()
            rows.prime()
        i = pl.program_id(0)
        ctx = i < NT_P
        m = mod_ref[pl.ds(_mod_row(i), 1), :]
        x = jnp.where(ctx, x_refs[0][...], x_refs[1][...]) if split_x else x_refs[0][...]
        o = jnp.where(ctx, op_ref[...], os_ref[...])
        if first:
            for k in range(len(wo_jobs)):
                cols.take(k)
        x = x + m[:, 2 * D:3 * D] * _dot(o, wo_bf[...])
        h = _adanorm(x, g_ref[...], m[:, 3 * D:4 * D], m[:, 4 * D:5 * D]).astype(BF16)
        acc = None
        for c in range(n_chunks):
            if first:
                cols.take(len(wo_jobs) + 2 * c)
                cols.take(len(wo_jobs) + 2 * c + 1)
                rows.take(c)
            gt = _dot(h, win_bf[:, c * FF_CHUNK:(c + 1) * FF_CHUNK])
            up = _dot(h, win_bf[:, D_FF + c * FF_CHUNK:D_FF + (c + 1) * FF_CHUNK])
            part = _dot((_silu(gt) * up).astype(BF16), wout_bf[c * FF_CHUNK:(c + 1) * FF_CHUNK, :])
            acc = part if acc is None else acc + part
        x = x + m[:, 5 * D:6 * D] * acc
        if not final:
            out_refs[0][...] = x
            return
        y = x * lax.rsqrt(jnp.mean(x * x, axis=-1, keepdims=True) + EPS) * gf_ref[...]
        if first:
            out_refs[0][...] = y
            return

        @pl.when(ctx)
        def _():
            out_refs[0][...] = y

        @pl.when(jnp.logical_not(ctx))
        def _():
            out_refs[1][...] = y

    _first_step_or_not(body)


def _post_ffn(xs, op, os_, mod_l, w_o, wo_index, g, w_in, w_out, layer, g_final=None):
    tile = pl.BlockSpec((TM, D), lambda i: (i, 0))
    p_tile = pl.BlockSpec((TM, D), lambda i: (jnp.minimum(i, NT_P - 1), 0))
    s_tile = pl.BlockSpec((TM, D), lambda i: (jnp.maximum(i - NT_P, 0), 0))
    split_x = len(xs) == 2
    final = g_final is not None
    in_specs = ([p_tile, s_tile] if split_x else [tile]) + [
        p_tile, s_tile, _const_spec((MOD_ROWS, N_MOD * D)), HBM_SPEC, _const_spec((1, D)), HBM_SPEC, HBM_SPEC]
    args = list(xs) + [op, os_, mod_l, w_o, g, w_in, w_out]
    if final:
        in_specs.append(_const_spec((1, D)))
        args.append(g_final)
        out_specs = [p_tile, s_tile]
        out_shape = [jax.ShapeDtypeStruct((N_P, D), F32), jax.ShapeDtypeStruct((N_S, D), F32)]
    else:
        out_specs = tile
        out_shape = jax.ShapeDtypeStruct((N_TOK, D), F32)
    kern = lambda *refs: _post_ffn_kernel(*refs, layer=layer, wo_index=wo_index, split_x=split_x, final=final)
    return pl.pallas_call(
        kern,
        grid=(N_TOK // TM,),
        in_specs=in_specs,
        out_specs=out_specs,
        out_shape=out_shape,
        scratch_shapes=([pltpu.VMEM((D, D), BF16), pltpu.VMEM((D, 2 * D_FF), BF16), pltpu.VMEM((D_FF, D), BF16)]
                        + _stream_scratch((D, FF_CHUNK), FF_DEPTH_COLS)
                        + _stream_scratch((FF_CHUNK, D), FF_DEPTH_ROWS)),
        compiler_params=_cparams(1),
        name="post_ffn",
    )(*args)


def kernel(x_prompt, x_sample, cache_attn_k, cache_attn_v, state_hgrn, c, c_ctx, w_ada, b_ada, g_norm_mix,
           g_norm_ffn, w_qkv_attn, lam_attn, g_subln_attn, w_o_attn, w_in_rec, lb_logits_rec, g_out_rec,
           w_o_rec, w_four, w_ffn_in, w_ffn_out, g_final):
    xs = (x_prompt.reshape(N_P, D), x_sample.reshape(N_S, D))
    cond = jnp.concatenate([c_ctx.reshape(1, D), c, jnp.zeros((MOD_ROWS - 1 - DEC_BATCH, D), F32)], axis=0)
    mod = _modulation(cond, w_ada, b_ada)
    cache_kt = jnp.transpose(cache_attn_k, (0, 1, 3, 4, 5, 2)).reshape(DEC_BATCH, -1, HEADS, HD, PAST)
    rope_tabs = _rope_tables()
    gla_tabs = _gla_tables()
    gla_tabs = (gla_tabs[0], gla_tabs[1].astype(BF16))
    kv_ctx, new_s = None, []
    for i in range(DEPTH):
        kind, j = i % N_MIXERS, i // N_MIXERS
        g_mix = g_norm_mix[i].reshape(1, D)
        x_p = xs[0]
        x_s, s_tile0 = (xs[1], 0) if len(xs) == 2 else (xs[0], NT_P)
        if kind == 0:
            lam_init = 0.8 - 0.6 * math.exp(-0.3 * i)
            g_sub = g_subln_attn[j].reshape(1, HD)
            qs, ks, vs = _qkv_sample(x_s, s_tile0, mod[i], g_mix, w_qkv_attn, j, rope_tabs)
            os_ = _attn_sample(qs, ks, vs, cache_kt, cache_attn_v, j, lam_attn[j], g_sub, lam_init)
            kt_all, v_all, op = _attn_prompt(x_p, mod[i], g_mix, w_qkv_attn, j, lam_attn[j], g_sub, lam_init,
                                             prev=kv_ctx, after=qs)
            kv_ctx = (kt_all, v_all)
            w_o = w_o_attn
        elif kind == 1:
            proj = _rec_proj(xs[0], mod[i], g_mix, w_in_rec, j)
            g_out = g_out_rec[j].reshape(1, HD)
            op, s_ctx = _gla(*proj, lb_logits_rec, g_out, gla_tabs, row0=0, nb=BATCH, ln=SEQ, layer=i)
            new_s.append(s_ctx)
            os_ = _gla(*proj, lb_logits_rec, g_out, gla_tabs, row0=N_P, nb=DEC_BATCH, ln=DEC_SEQ, layer=i,
                       s0=state_hgrn[:, j])[0]
            w_o = w_o_rec
        else:
            op = _fourier(xs[0], mod[i], g_mix, row0=0, nb=BATCH, ln=SEQ, per_step=FOUR_CTX_PER_STEP,
                          row_of=lambda b: 0)
            os_ = _fourier(xs[0], mod[i], g_mix, row0=N_P, nb=DEC_BATCH, ln=DEC_SEQ, per_step=1,
                           row_of=lambda b: 1 + b)
            w_o = w_four
        out = _post_ffn(xs, op, os_, mod[i], w_o, j, g_norm_ffn[i].reshape(1, D), w_ffn_in, w_ffn_out, i,
                        g_final.reshape(1, D) if i == DEPTH - 1 else None)
        xs = tuple(out) if i == DEPTH - 1 else (out,)
    y_prompt = xs[0].reshape(BATCH, SEQ, D)
    y_sample = xs[1].reshape(DEC_BATCH, DEC_SEQ, D)
    new_k = jnp.transpose(kv_ctx[0].reshape(BATCH, -1, HEADS, 2, DH, SEQ), (0, 1, 5, 2, 3, 4))
    new_v = kv_ctx[1].reshape(BATCH, -1, SEQ, HEADS, HD)
    return (y_prompt, y_sample, new_k, new_v, jnp.stack(new_s, axis=1))
```

```python
import math

import numpy as np
import jax
import jax.numpy as jnp
from jax import lax
from jax.experimental import pallas as pl
from jax.experimental.pallas import tpu as pltpu

F32 = jnp.float32
BF16 = jnp.bfloat16

D = 1024
BATCH = 16
SEQ = 256
DEPTH = 4
DEC_BATCH = 4
DEC_SEQ = 1024
PAST = 512
GRID_W = 64
N_MIXERS = 3
HEADS = 8
DH = 64
HD = 128
ROPE_THETA = 10000.0
D_FF = 2816
N_MOD = 6
EPS = 1e-6
LOG2E = 1.0 / math.log(2.0)
FOUR_GROUPS = 4
FOUR_DG = 256

N_P = BATCH * SEQ
N_S = DEC_BATCH * DEC_SEQ
N_TOK = N_P + N_S
TM = 512
NT_P = N_P // TM
NT_S = N_S // TM
TILES_PER_DEC_SEQ = DEC_SEQ // TM
SEQ_PER_TILE = TM // SEQ
MOD_ROWS = 8
FF_CHUNK = 256
ATT_TQ = 512
ATT_HP = 2
ATT_P_WIDTH = 2
ATT_SEQ_SKEW = 5
ATT_SKEW = 1
FOUR_CTX_PER_STEP = 4
GLA_HP = 8
GLA_HP_CARRY = 4
GLA_GROUP = 256
GLA_BLK = 8
GLA_LEVELS = (8, 16, 32, 64, 128)
VMEM_LIMIT = 60 * 1024 * 1024


def _cparams(n_axes):
    return pltpu.CompilerParams(dimension_semantics=("arbitrary",) * n_axes,
                                vmem_limit_bytes=VMEM_LIMIT)


def _const_spec(shape):
    nd = len(shape)
    return pl.BlockSpec(shape, lambda *_: (0,) * nd, pipeline_mode=pl.Buffered(1))


def _sigmoid(x):
    return 1.0 / (1.0 + jnp.exp(-x))


def _silu(x):
    return x * _sigmoid(x)


def _adanorm(x, g, shift, scale):
    ms = jnp.mean(x * x, axis=-1, keepdims=True)
    return x * lax.rsqrt(ms + EPS) * (g * (1.0 + scale)) + shift


def _mod_row(tile):
    return jnp.where(tile < NT_P, 0, 1 + (tile - NT_P) // TILES_PER_DEC_SEQ)


def _dot(a, b):
    return jnp.dot(a, b, preferred_element_type=F32)


def _dot_nt(a, b):
    return lax.dot_general(a, b, (((1,), (1,)), ((), ())), preferred_element_type=F32)


def _dot_tn(a, b):
    return lax.dot_general(a, b, (((0,), (0,)), ((), ())), preferred_element_type=F32)


def _split2(x):
    hi = x.astype(BF16)
    return hi, (x - hi.astype(F32)).astype(BF16)


def _interleaved(generators, skew=0):
    results = [None] * len(generators)
    active = list(range(len(generators)))
    rnd = 0
    while active:
        for i in list(active):
            if rnd < i * skew:
                continue
            try:
                next(generators[i])
            except StopIteration as stop:
                results[i] = stop.value
                active.remove(i)
        rnd += 1
        yield
    return results


def _interleave(generators, skew=0):
    rounds = _interleaved(generators, skew)
    while True:
        try:
            next(rounds)
        except StopIteration as stop:
            return stop.value


class _WeightStream:
    def __init__(self, jobs, stage_ref, sem_ref):
        self.jobs, self.stage, self.sem = jobs, stage_ref, sem_ref
        self.depth = stage_ref.shape[0]

    def _copy(self, k):
        slot = k % self.depth
        return pltpu.make_async_copy(self.jobs[k][0], self.stage.at[slot], self.sem.at[slot])

    def prime(self):
        for k in range(min(self.depth, len(self.jobs))):
            self._copy(k).start()

    def take(self, k):
        self._copy(k).wait()
        self.jobs[k][1][...] = self.stage[k % self.depth].astype(BF16)
        if k + self.depth < len(self.jobs):
            self._copy(k + self.depth).start()


def _stream_scratch(piece_shape, depth):
    return [pltpu.VMEM((depth,) + tuple(piece_shape), F32), pltpu.SemaphoreType.DMA((depth,))]


def _column_jobs(w_hbm, index, w_bf, width):
    return [(w_hbm.at[index, :, pl.ds(c, width)], w_bf.at[:, pl.ds(c, width)])
            for c in range(0, w_bf.shape[1], width)]


def _first_step_or_not(body):
    first = pl.program_id(0) == 0
    pl.when(first)(lambda: body(True))
    pl.when(jnp.logical_not(first))(lambda: body(False))


HBM_SPEC = pl.BlockSpec(memory_space=pl.ANY)
W_PIECE = 512
W_DEPTH = 3


MOD_TK = 512


def _mod_kernel(cond_ref, w_ref, b_ref, o_ref):
    kk = pl.program_id(1)
    s = _silu(cond_ref[...])
    s_hi = s.astype(BF16)
    s_lo = (s - s_hi.astype(F32)).astype(BF16)
    w = w_ref[0]
    w_hi = w.astype(BF16)
    w_lo = (w - w_hi.astype(F32)).astype(BF16)
    r = _dot(jnp.concatenate([s_hi, s_lo], axis=0), w_hi)
    part = r[:MOD_ROWS] + r[MOD_ROWS:] + _dot(s_hi, w_lo)

    @pl.when(kk == 0)
    def _():
        o_ref[0] = part + b_ref[0]

    @pl.when(kk != 0)
    def _():
        o_ref[0] = o_ref[0] + part


def _modulation(cond, w_ada, b_ada):
    return pl.pallas_call(
        _mod_kernel,
        grid=(DEPTH, D // MOD_TK),
        in_specs=[
            pl.BlockSpec((MOD_ROWS, MOD_TK), lambda l, k: (0, k)),
            pl.BlockSpec((1, MOD_TK, N_MOD * D), lambda l, k: (l, k, 0)),
            pl.BlockSpec((1, 1, N_MOD * D), lambda l, k: (l, 0, 0)),
        ],
        out_specs=pl.BlockSpec((1, MOD_ROWS, N_MOD * D), lambda l, k: (l, 0, 0)),
        out_shape=jax.ShapeDtypeStruct((DEPTH, MOD_ROWS, N_MOD * D), F32),
        compiler_params=_cparams(2),
        name="modulation",
    )(cond, w_ada, b_ada.reshape(DEPTH, 1, N_MOD * D))


def _rope_tables():
    t = np.arange(DEC_SEQ)
    row, col = t // GRID_W, t % GRID_W
    lane = np.arange(HD) % DH
    use_row = lane < DH // 2
    idx = (lane % (DH // 2)) % (DH // 4)
    inv = ROPE_THETA ** (-idx.astype(np.float64) / (DH // 4))
    pos = np.where(use_row[None, :], row[:, None], col[:, None]).astype(np.float64)
    ang = pos * inv[None, :]
    first = (lane % (DH // 2)) < DH // 4
    cos = np.cos(ang)
    sin = np.sin(ang)
    sin_minus = np.where(first[None, :], -sin, 0.0)
    sin_plus = np.where(first[None, :], 0.0, sin)
    return (jnp.asarray(cos, F32), jnp.asarray(sin_minus, F32), jnp.asarray(sin_plus, F32))


def _rope_tile(t, cos, sin_minus, sin_plus):
    return (t * cos + pltpu.roll(t, HD - DH // 4, 1) * sin_minus
            + pltpu.roll(t, DH // 4, 1) * sin_plus)


def _lambda(lam_ref, lam_init):
    lp = lam_ref[...]
    return (jnp.exp(jnp.sum(lp[0:1] * lp[1:2], axis=-1, keepdims=True))
            - jnp.exp(jnp.sum(lp[2:3] * lp[3:4], axis=-1, keepdims=True)) + lam_init)


def _with_ones(v):
    return jnp.concatenate([v, jnp.ones_like(v)], axis=1)


def _diff_attn_head(q, score_fns, values, lam, g_scaled):
    tq = q.shape[0]
    lane = lax.broadcasted_iota(jnp.int32, q.shape, 1)
    zero = jnp.zeros_like(q)
    qq = jnp.concatenate([jnp.where(lane < DH, q, zero), jnp.where(lane >= DH, q, zero)], axis=0)
    scores = [fn(qq) for fn in score_fns]
    yield
    mx = scores[0].max(axis=-1, keepdims=True)
    for s in scores[1:]:
        mx = jnp.maximum(mx, s.max(axis=-1, keepdims=True))
    yield
    acc = None
    for s, v in zip(scores, values):
        c = _dot(jnp.exp2(s - mx).astype(BF16), v)
        acc = c if acc is None else acc + c
        yield
    r = acc[:, 0:HD] / acc[:, HD:2 * HD]
    o = r[:tq] - lam * r[tq:]
    o = o * lax.rsqrt(jnp.mean(o * o, axis=-1, keepdims=True) + EPS) * g_scaled
    return o.astype(BF16)


def _attn_prompt_kernel(*refs, j, n_layers, lam_init):
    x_ref, mod_ref, g_ref, w_hbm, lam_ref, gsub_ref = refs[:6]
    kt_ref, v_hbm, o_ref, w_bf, stage, sem, v_buf, z_buf, v_sem = refs[-9:]
    owns_all = j == 0
    stream = _WeightStream(_column_jobs(w_hbm, j, w_bf, W_PIECE), stage, sem)
    per_mat = D // W_PIECE
    seq0 = pl.program_id(0) * SEQ_PER_TILE

    def v_copies(s):
        out = []
        for hh in range(HEADS):
            for jj in (range(n_layers) if owns_all else (j,)):
                src = v_buf.at[s, :, pl.ds(hh * HD, HD)] if jj == j else z_buf
                out.append(pltpu.make_async_copy(src, v_hbm.at[seq0 + s, jj, :, hh, :], v_sem.at[s, hh, jj]))
        return out

    def body(first):
        if first:
            stream.prime()
        m = mod_ref[pl.ds(0, 1), :]
        lam = _lambda(lam_ref, lam_init)
        g_scaled = gsub_ref[...] * (1.0 - lam_init)

        def sequence(s, loads):
            rows = slice(s * SEQ, (s + 1) * SEQ)
            h = _adanorm(x_ref[rows, :], g_ref[...], m[:, 0:D], m[:, D:2 * D]).astype(BF16)
            yield
            mats = []
            for n in range(3):
                if loads:
                    for p in range(per_mat):
                        stream.take(n * per_mat + p)
                mats.append(_dot(h, w_bf[:, n * D:(n + 1) * D]))
                yield
            q = (mats[0] * (DH ** -0.5 * LOG2E)).astype(BF16)
            k, v = mats[1], mats[2]
            kb, vb = k.astype(BF16), v.astype(BF16)
            if owns_all:
                for jj in range(n_layers):
                    kt_ref[s, jj] = k.T if jj == j else jnp.zeros((D, SEQ), F32)
            else:
                kt_ref[s] = k.T
            v_buf[s] = v
            for cp in v_copies(s):
                cp.start()
            yield
            for h0 in range(0, HEADS, ATT_P_WIDTH):
                heads = []
                for hh in range(h0, h0 + ATT_P_WIDTH):
                    cols = slice(hh * HD, (hh + 1) * HD)
                    kh = kb[:, cols]
                    heads.append(_diff_attn_head(q[:, cols], [lambda qq, kh=kh: _dot_nt(qq, kh)],
                                                 [_with_ones(vb[:, cols])], lam, g_scaled))
                outs = yield from _interleaved(heads, ATT_SKEW)
                for hh, o in zip(range(h0, h0 + ATT_P_WIDTH), outs):
                    o_ref[rows, hh * HD:(hh + 1) * HD] = o

        z_buf[...] = jnp.zeros_like(z_buf)
        _interleave([sequence(s, first and s == 0) for s in range(SEQ_PER_TILE)], skew=ATT_SEQ_SKEW)
        for s in range(SEQ_PER_TILE):
            for cp in v_copies(s):
                cp.wait()

    _first_step_or_not(body)


def _attn_prompt(x, mod_l, g, w_qkv, j, lam_p, g_sub, lam_init, prev=None, after=None):
    n_layers = w_qkv.shape[0]
    assert (prev is None) == (j == 0)
    tile = pl.BlockSpec((TM, D), lambda i: (i, 0))
    in_specs = [tile, _const_spec((MOD_ROWS, N_MOD * D)), _const_spec((1, D)), HBM_SPEC,
                _const_spec((4, DH)), _const_spec((1, HD))]
    args = [x, mod_l, g, w_qkv, lam_p, g_sub]
    if prev is None:
        kt_spec = pl.BlockSpec((SEQ_PER_TILE, n_layers, D, SEQ), lambda i: (i, 0, 0, 0))
        v_spec = pl.BlockSpec((SEQ_PER_TILE, n_layers, SEQ, D), lambda i: (i, 0, 0, 0))
        aliases = {}
    else:
        kt_spec = pl.BlockSpec((SEQ_PER_TILE, None, D, SEQ), lambda i: (i, j, 0, 0))
        v_spec = pl.BlockSpec((SEQ_PER_TILE, None, SEQ, D), lambda i: (i, j, 0, 0))
        aliases = {len(args): 0, len(args) + 1: 1}
        in_specs += [HBM_SPEC, HBM_SPEC]
        args += list(prev)
    if after is not None:
        in_specs.append(HBM_SPEC)
        args.append(after)
    kern = lambda *refs: _attn_prompt_kernel(*refs, j=j, n_layers=n_layers, lam_init=lam_init)
    return pl.pallas_call(
        kern,
        grid=(NT_P,),
        in_specs=in_specs,
        out_specs=[kt_spec, HBM_SPEC, tile],
        out_shape=[jax.ShapeDtypeStruct((BATCH, n_layers, D, SEQ), F32),
                   jax.ShapeDtypeStruct((BATCH, n_layers, SEQ, HEADS, HD), F32),
                   jax.ShapeDtypeStruct((N_P, D), BF16)],
        input_output_aliases=aliases,
        scratch_shapes=([pltpu.VMEM((D, 3 * D), BF16)] + _stream_scratch((D, W_PIECE), W_DEPTH)
                        + [pltpu.VMEM((SEQ_PER_TILE, SEQ, D), F32), pltpu.VMEM((SEQ, HD), F32),
                           pltpu.SemaphoreType.DMA((SEQ_PER_TILE, HEADS, n_layers))]),
        compiler_params=_cparams(1),
        name="attn_prompt",
    )(*args)


def _qkv_sample_kernel(x_ref, mod_ref, g_ref, w_hbm, cos_ref, sm_ref, sp_ref, q_ref, k_ref, v_ref,
                       w_bf, stage, sem, *, j):
    stream = _WeightStream(_column_jobs(w_hbm, j, w_bf, W_PIECE), stage, sem)
    per_mat = D // W_PIECE

    def body(first):
        if first:
            stream.prime()
        i = pl.program_id(0)
        m = mod_ref[pl.ds(1 + i // TILES_PER_DEC_SEQ, 1), :]

        def half(rows, loads):
            h = _adanorm(x_ref[rows, :], g_ref[...], m[:, 0:D], m[:, D:2 * D]).astype(BF16)
            cos, sm, sp = cos_ref[rows, :], sm_ref[rows, :], sp_ref[rows, :]
            yield
            for n, (o_ref, scale) in enumerate(((q_ref, DH ** -0.5 * LOG2E), (k_ref, None), (v_ref, None))):
                if loads:
                    for p in range(per_mat):
                        stream.take(n * per_mat + p)
                y = _dot(h, w_bf[:, n * D:(n + 1) * D])
                if o_ref is v_ref:
                    o_ref[rows, :] = y.astype(BF16)
                else:
                    if scale is not None:
                        y = y * scale
                    for hh in range(HEADS):
                        t = _rope_tile(y[:, hh * HD:(hh + 1) * HD], cos, sm, sp)
                        o_ref[rows, hh * HD:(hh + 1) * HD] = t.astype(BF16)
                yield

        _interleave([half(slice(r, r + TM // 2), first and r == 0) for r in (0, TM // 2)], skew=1)

    _first_step_or_not(body)


def _qkv_sample(x, tile0, mod_l, g, w_qkv, j, tabs):
    tile = pl.BlockSpec((TM, D), lambda i: (i, 0))
    tab = pl.BlockSpec((TM, HD), lambda i: (i % TILES_PER_DEC_SEQ, 0))
    kern = lambda *refs: _qkv_sample_kernel(*refs, j=j)
    return pl.pallas_call(
        kern,
        grid=(NT_S,),
        in_specs=[pl.BlockSpec((TM, D), lambda i: (i + tile0, 0)), _const_spec((MOD_ROWS, N_MOD * D)),
                  _const_spec((1, D)), HBM_SPEC, tab, tab, tab],
        out_specs=[tile, tile, tile],
        out_shape=[jax.ShapeDtypeStruct((N_S, D), BF16)] * 3,
        scratch_shapes=[pltpu.VMEM((D, 3 * D), BF16)] + _stream_scratch((D, W_PIECE), W_DEPTH),
        compiler_params=_cparams(1),
        name="qkv_sample",
    )(x, mod_l, g, w_qkv, *tabs)


def _attn_sample_kernel(*refs, j, lam_init):
    q_ref, kn_ref, vn_ref, kct_ref, vc_hbm, lam_ref, gsub_ref, o_ref, vc_buf, vc_sem = refs
    b, i = pl.program_id(0), pl.program_id(1)
    nb, nq = pl.num_programs(0), pl.num_programs(1)

    def vc_copies(req):
        slot = req % 2
        return [pltpu.make_async_copy(vc_hbm.at[req, j, :, hh, :], vc_buf.at[slot, hh], vc_sem.at[slot, hh])
                for hh in range(HEADS)]

    @pl.when(jnp.logical_and(b == 0, i == 0))
    def _():
        for cp in vc_copies(b):
            cp.start()

    @pl.when(i == 0)
    def _():
        for cp in vc_copies(b):
            cp.wait()

    @pl.when(jnp.logical_and(i == nq - 1, b + 1 < nb))
    def _():
        for cp in vc_copies(b + 1):
            cp.start()

    slot = b % 2
    lam = _lambda(lam_ref, lam_init)
    g_scaled = gsub_ref[...] * (1.0 - lam_init)
    for h0 in range(0, HEADS, ATT_HP):
        heads = []
        for hh in range(h0, h0 + ATT_HP):
            cols = slice(hh * HD, (hh + 1) * HD)
            kn = kn_ref[:, cols]
            kct = kct_ref[hh].astype(BF16)
            vc = vc_buf[slot, hh].astype(BF16)
            heads.append(_diff_attn_head(
                q_ref[:, cols], [lambda qq, kn=kn: _dot_nt(qq, kn), lambda qq, kct=kct: _dot(qq, kct)],
                [_with_ones(vn_ref[:, cols]), _with_ones(vc)], lam, g_scaled))
        for hh, o in zip(range(h0, h0 + ATT_HP), _interleave(heads, ATT_SKEW)):
            o_ref[:, hh * HD:(hh + 1) * HD] = o


def _attn_sample(q, kn, vn, cache_kt, cache_v, j, lam_p, g_sub, lam_init):
    nq = DEC_SEQ // ATT_TQ
    seq = pl.BlockSpec((DEC_SEQ, D), lambda b, i: (b, 0))
    blk = pl.BlockSpec((ATT_TQ, D), lambda b, i: (b * nq + i, 0))
    in_specs = [blk, seq, seq,
                pl.BlockSpec((None, None, HEADS, HD, PAST), lambda b, i: (b, j, 0, 0, 0)),
                HBM_SPEC,
                pl.BlockSpec((4, DH), lambda b, i: (0, 0)), pl.BlockSpec((1, HD), lambda b, i: (0, 0))]
    kern = lambda *refs: _attn_sample_kernel(*refs, j=j, lam_init=lam_init)
    return pl.pallas_call(
        kern,
        grid=(DEC_BATCH, nq),
        in_specs=in_specs,
        out_specs=blk,
        out_shape=jax.ShapeDtypeStruct((N_S, D), BF16),
        scratch_shapes=[pltpu.VMEM((2, HEADS, PAST, HD), F32), pltpu.SemaphoreType.DMA((2, HEADS))],
        compiler_params=_cparams(2),
        name="attn_sample",
    )(q, kn, vn, cache_kt, cache_v, lam_p, g_sub)


def _rec_proj_kernel(x_ref, mod_ref, g_ref, w_hbm, q_ref, v_ref, gate_ref, zf_ref, zb_ref,
                     w_bf, stage, sem, *, j):
    stream = _WeightStream(_column_jobs(w_hbm, j, w_bf, W_PIECE), stage, sem)
    per_mat = D // W_PIECE

    def body(first):
        if first:
            stream.prime()
        m = mod_ref[pl.ds(_mod_row(pl.program_id(0)), 1), :]

        def half(rows, loads):
            h = _adanorm(x_ref[rows, :], g_ref[...], m[:, 0:D], m[:, D:2 * D]).astype(BF16)
            yield
            for n, o_ref in enumerate((q_ref, v_ref, gate_ref, zf_ref, zb_ref)):
                if loads:
                    for p in range(per_mat):
                        stream.take(n * per_mat + p)
                y = _dot(h, w_bf[:, n * D:(n + 1) * D])
                if o_ref is q_ref or o_ref is gate_ref:
                    y = _silu(y)
                o_ref[rows, :] = y.astype(o_ref.dtype)
                yield

        _interleave([half(slice(r, r + TM // 2), first and r == 0) for r in (0, TM // 2)], skew=1)

    _first_step_or_not(body)


def _rec_proj(x, mod_l, g, w_in, j):
    tile = pl.BlockSpec((TM, D), lambda i: (i, 0))
    kern = lambda *refs: _rec_proj_kernel(*refs, j=j)
    return pl.pallas_call(
        kern,
        grid=(N_TOK // TM,),
        in_specs=[tile, _const_spec((MOD_ROWS, N_MOD * D)), _const_spec((1, D)), HBM_SPEC],
        out_specs=[tile] * 5,
        out_shape=[jax.ShapeDtypeStruct((N_TOK, D), BF16)] * 3 + [jax.ShapeDtypeStruct((N_TOK, D), F32)] * 2,
        scratch_shapes=[pltpu.VMEM((D, 5 * D), BF16)] + _stream_scratch((D, W_PIECE), W_DEPTH),
        compiler_params=_cparams(1),
        name="rec_proj",
    )(x, mod_l, g, w_in)


def _gla_tables():
    t = np.arange(GLA_GROUP)[:, None]
    s = np.arange(GLA_GROUP)[None, :]
    ids = np.full((GLA_GROUP, GLA_GROUP), -1, np.int32)
    for l, b in reversed(list(enumerate(GLA_LEVELS, 1))):
        ids = np.where((t // (2 * b) == s // (2 * b)) & (s < t), l, ids)
    ids = np.where((t // GLA_BLK == s // GLA_BLK) & (s <= t), 0, ids)
    ids = np.stack([ids, ids.T]).astype(np.int32)
    e = np.zeros((GLA_BLK * HD, GLA_GROUP), np.float32)
    for j in range(GLA_BLK):
        e[j * HD:(j + 1) * HD, j::GLA_BLK] = 1.0
    return jnp.asarray(ids), jnp.asarray(e)


def _bcast_row(x, blk, idx):
    n, w = x.shape
    r = x.reshape(n // blk, blk, w)[:, idx:idx + 1, :]
    return jnp.broadcast_to(r, (n // blk, blk, w)).reshape(n, w)


def _gla_group(q, v, z, lb, ids, e_mat, st, backward):
    n = GLA_GROUP
    sig = 0.5 * jnp.tanh(0.5 * z) + 0.5
    f = lb + (1.0 - lb) * sig
    k = (1.0 - lb) * (1.0 - sig)
    lf = jnp.log(f) * LOG2E
    row = lax.broadcasted_iota(jnp.int32, (n, n), 0)
    col = lax.broadcasted_iota(jnp.int32, (n, n), 1)
    tri = jnp.where((col >= row) if backward else (col <= row), 1.0, 0.0).astype(BF16)
    c2 = _dot(tri, jnp.concatenate(_split2(lf), axis=1))
    cum = c2[:, 0:HD] + c2[:, HD:2 * HD]
    ck = jnp.log(k) * LOG2E - cum
    yield

    prods = []
    for j in range(GLA_BLK):
        prods.append((q * jnp.exp2(jnp.minimum(cum + _bcast_row(ck, GLA_BLK, j), 0.0))).astype(BF16))
        yield
    half = n // 2
    ids_q = ids[0:half, 0:half]
    diag = lambda x, h: x[h * half:(h + 1) * half, h * half:(h + 1) * half]
    pair = _dot(jnp.concatenate(prods, axis=1), e_mat)
    quads = [jnp.where(ids_q == 0, diag(pair, h), 0.0) for h in range(2)]
    yield

    for l, b in enumerate(GLA_LEVELS[:-1], 1):
        edge = _bcast_row(cum, 2 * b, b if backward else b - 1)
        ql = (q * jnp.exp2(cum - edge)).astype(BF16)
        kl = jnp.exp2(edge + ck).astype(BF16)
        x = _dot_nt(ql, kl)
        quads = [jnp.where(ids_q == l, diag(x, h), quads[h]) for h in range(2)]
        yield

    assert GLA_LEVELS[-1] == half
    lo, hi = slice(0, half), slice(half, n)
    qs, ks = (lo, hi) if backward else (hi, lo)
    edge = cum[half:half + 1, :] if backward else cum[half - 1:half, :]
    cross = _dot_nt((q[qs] * jnp.exp2(cum[qs] - edge)).astype(BF16),
                    jnp.exp2(edge + ck[ks]).astype(BF16)).astype(BF16)
    o_halves = [_dot(quads[h].astype(BF16), v[h * half:(h + 1) * half]) for h in range(2)]
    o_halves[0 if backward else 1] += _dot(cross, v[ks])
    o = jnp.concatenate(o_halves, axis=0)
    yield

    total = cum[0:1, :] if backward else cum[n - 1:n, :]
    if st is not None:
        o = o + _dot_nt((q * jnp.exp2(cum)).astype(BF16), st.astype(BF16))
    kdec = jnp.exp2(total + ck).astype(BF16)
    upd = _dot_tn(v, kdec)
    st_new = upd if st is None else st * jnp.exp2(total) + upd
    return o, st_new


def _lower_bound(x, layer):
    rows = [x[l:l + 1, :] for l in range(DEPTH)]
    mx = rows[0]
    for r in rows[1:]:
        mx = jnp.maximum(mx, r)
    ex = [jnp.exp(r - mx) for r in rows]
    tot = ex[0]
    for e in ex[1:]:
        tot = tot + e
    acc = ex[0] * 0.0
    for l in range(1, layer + 1):
        acc = acc + ex[l]
    return acc / tot


def _gla_kernel(*refs, n_groups, layer, has_state, hp):
    if has_state:
        (q_ref, v_ref, gate_ref, zf_ref, zb_ref, lbl_ref, g_ref, ids_ref, e_ref, s0_ref,
         o_ref, acc_ref) = refs
    else:
        (q_ref, v_ref, gate_ref, zf_ref, zb_ref, lbl_ref, g_ref, ids_ref, e_ref,
         o_ref, sout_ref, acc_ref) = refs
    e_mat = e_ref[...]
    z_refs = (zf_ref, zb_ref)
    chains = [(hh, d) for hh in range(hp) for d in range(2)]
    cols = [slice(hh * HD, (hh + 1) * HD) for hh in range(hp)]
    lbs = {(hh, d): _lower_bound(lbl_ref[d, :, cols[hh]], layer) for hh, d in chains}

    def step(gi, sts):
        rows, gens = [], []
        for d in range(2):
            g = gi if d == 0 else n_groups - 1 - gi
            start = g * GLA_GROUP
            rows.append(pl.ds(start if isinstance(start, int) else pl.multiple_of(start, GLA_GROUP), GLA_GROUP))
        for c, (hh, d) in enumerate(chains):
            gens.append(_gla_group(q_ref[rows[d], cols[hh]].astype(F32), v_ref[rows[d], cols[hh]],
                                   z_refs[d][rows[d], cols[hh]], lbs[hh, d], ids_ref[d], e_mat, sts[c],
                                   backward=(d == 1)))
        new = []
        for (hh, d), (o, st) in zip(chains, _interleave(gens)):
            acc_ref[rows[d], cols[hh]] = acc_ref[rows[d], cols[hh]] + o
            new.append(st)
        return tuple(new)

    acc_ref[...] = jnp.zeros_like(acc_ref)
    if has_state:
        init = tuple(s0_ref[d, hh].T for hh, d in chains)
        lax.fori_loop(0, n_groups, step, init)
    else:
        sts = step(0, (None,) * len(chains))
        for (hh, d), st in zip(chains, sts):
            sout_ref[d, hh] = st.T
    for hh in range(hp):
        o = acc_ref[:, cols[hh]]
        y = o * lax.rsqrt(jnp.mean(o * o, axis=-1, keepdims=True) + EPS) * g_ref[...]
        o_ref[:, cols[hh]] = (y * gate_ref[:, cols[hh]].astype(F32)).astype(BF16)


def _gla(q, v, gate, zf, zb, lb_logits, g_out, tabs, *, row0, nb, ln, layer, s0=None):
    ids, e_mat = tabs
    assert ln % GLA_GROUP == 0 and (s0 is not None or ln == GLA_GROUP)
    b0 = row0 // ln
    hp = GLA_HP if s0 is None else GLA_HP_CARRY
    w = hp * HD
    seq = pl.BlockSpec((ln, w), lambda b, h: (b + b0, h))
    in_specs = [seq] * 5 + [
        pl.BlockSpec((2, DEPTH, w), lambda b, h: (0, 0, h)),
        pl.BlockSpec((1, HD), lambda b, h: (0, 0)),
        pl.BlockSpec((2, GLA_GROUP, GLA_GROUP), lambda b, h: (0, 0, 0)),
        pl.BlockSpec((GLA_BLK * HD, GLA_GROUP), lambda b, h: (0, 0)),
    ]
    args = [q, v, gate, zf, zb, lb_logits, g_out, ids, e_mat]
    out_specs = [pl.BlockSpec((ln, w), lambda b, h: (b, h))]
    out_shape = [jax.ShapeDtypeStruct((nb * ln, D), BF16)]
    state_spec = pl.BlockSpec((None, 2, hp, HD, HD), lambda b, h: (b, 0, h, 0, 0))
    if s0 is not None:
        in_specs.append(state_spec)
        args.append(s0)
    else:
        out_specs.append(state_spec)
        out_shape.append(jax.ShapeDtypeStruct((nb, 2, HEADS, HD, HD), F32))
    kern = lambda *refs: _gla_kernel(*refs, n_groups=ln // GLA_GROUP, layer=layer, has_state=s0 is not None,
                                      hp=hp)
    return pl.pallas_call(
        kern,
        grid=(nb, HEADS // hp),
        in_specs=in_specs,
        out_specs=out_specs,
        out_shape=out_shape,
        scratch_shapes=[pltpu.VMEM((ln, w), F32)],
        compiler_params=_cparams(2),
        name="gla_sample" if s0 is not None else "gla_prompt",
    )(*args)


def _dft_tables(n):
    c = np.arange(FOUR_DG)
    ang_c = 2.0 * np.pi * ((c[:, None] * c[None, :]) % FOUR_DG) / FOUR_DG
    cs = np.concatenate([np.cos(ang_c), np.sin(ang_c)], axis=1)
    t = np.arange(n)
    ang_n = 2.0 * np.pi * ((t[:, None] * t[None, :]) % n) / n
    cn = np.concatenate([np.cos(ang_n), -np.sin(ang_n)], axis=1) / math.sqrt(n * FOUR_DG)
    return jnp.asarray(cs, F32).astype(BF16), jnp.asarray(cn, F32).astype(BF16)


def _fourier_kernel(x_ref, mod_ref, g_ref, cs_ref, cn_ref, o_ref, *, row_of, ln):
    m = mod_ref[pl.ds(row_of(pl.program_id(0)), 1), :]
    h = _adanorm(x_ref[...], g_ref[...], m[:, 0:D], m[:, D:2 * D]).astype(BF16)
    cs = cs_ref[...]
    cn = cn_ref[...]
    seqs = [slice(r, r + ln) for r in range(0, h.shape[0], ln)]
    for g in range(FOUR_GROUPS):
        gc = slice(g * FOUR_DG, (g + 1) * FOUR_DG)
        xcs = _dot(h[:, gc], cs)
        rhs = jnp.concatenate([jnp.concatenate([xcs[rs, 0:FOUR_DG], xcs[rs, FOUR_DG:]], axis=0) for rs in seqs],
                              axis=1).astype(BF16)
        f = _dot(cn, rhs)
        for s, rs in enumerate(seqs):
            o_ref[rs, gc] = f[:, s * FOUR_DG:(s + 1) * FOUR_DG].astype(BF16)


def _fourier(x, mod_l, g, *, row0, nb, ln, per_step, row_of):
    cs, cn = _dft_tables(ln)
    rows = per_step * ln
    b0 = row0 // rows
    kern = lambda *refs: _fourier_kernel(*refs, row_of=row_of, ln=ln)
    return pl.pallas_call(
        kern,
        grid=(nb // per_step,),
        in_specs=[pl.BlockSpec((rows, D), lambda b: (b + b0, 0)), _const_spec((MOD_ROWS, N_MOD * D)),
                  _const_spec((1, D)), _const_spec((FOUR_DG, 2 * FOUR_DG)), _const_spec((ln, 2 * ln))],
        out_specs=pl.BlockSpec((rows, D), lambda b: (b, 0)),
        out_shape=jax.ShapeDtypeStruct((nb * ln, D), BF16),
        compiler_params=_cparams(1),
        name="fourier_%d" % ln,
    )(x, mod_l, g, cs, cn)


FF_DEPTH_COLS = 4
FF_DEPTH_ROWS = 2


def _post_ffn_kernel(*refs, layer, wo_index, split_x, final):
    refs = list(refs)
    x_refs = [refs.pop(0) for _ in range(2 if split_x else 1)]
    op_ref, os_ref, mod_ref, wo_hbm, g_ref, win_hbm, wout_hbm = refs[:7]
    refs = refs[7:]
    gf_ref = refs.pop(0) if final else None
    out_refs = [refs.pop(0) for _ in range(2 if final else 1)]
    wo_bf, win_bf, wout_bf, stage_c, sem_c, stage_r, sem_r = refs
    n_chunks = D_FF // FF_CHUNK
    wo_jobs = _column_jobs(wo_hbm, wo_index, wo_bf, FF_CHUNK)
    in_jobs = []
    for c in range(n_chunks):
        for c0 in (c * FF_CHUNK, D_FF + c * FF_CHUNK):
            in_jobs.append((win_hbm.at[layer, :, pl.ds(c0, FF_CHUNK)], win_bf.at[:, pl.ds(c0, FF_CHUNK)]))
    cols = _WeightStream(wo_jobs + in_jobs, stage_c, sem_c)
    rows = _WeightStream([(wout_hbm.at[layer, pl.ds(c * FF_CHUNK, FF_CHUNK), :],
                           wout_bf.at[pl.ds(c * FF_CHUNK, FF_CHUNK), :]) for c in range(n_chunks)],
                         stage_r, sem_r)

    def body(first):
        if first:
            cols.prime()
            rows.prime()
        i = pl.program_id(0)
        ctx = i < NT_P
        m = mod_ref[pl.ds(_mod_row(i), 1), :]
        x = jnp.where(ctx, x_refs[0][...], x_refs[1][...]) if split_x else x_refs[0][...]
        o = jnp.where(ctx, op_ref[...], os_ref[...])
        if first:
            for k in range(len(wo_jobs)):
                cols.take(k)
        x = x + m[:, 2 * D:3 * D] * _dot(o, wo_bf[...])
        h = _adanorm(x, g_ref[...], m[:, 3 * D:4 * D], m[:, 4 * D:5 * D]).astype(BF16)
        acc = None
        for c in range(n_chunks):
            if first:
                cols.take(len(wo_jobs) + 2 * c)
                cols.take(len(wo_jobs) + 2 * c + 1)
                rows.take(c)
            gt = _dot(h, win_bf[:, c * FF_CHUNK:(c + 1) * FF_CHUNK])
            up = _dot(h, win_bf[:, D_FF + c * FF_CHUNK:D_FF + (c + 1) * FF_CHUNK])
            part = _dot((_silu(gt) * up).astype(BF16), wout_bf[c * FF_CHUNK:(c + 1) * FF_CHUNK, :])
            acc = part if acc is None else acc + part
        x = x + m[:, 5 * D:6 * D] * acc
        if not final:
            out_refs[0][...] = x
            return
        y = x * lax.rsqrt(jnp.mean(x * x, axis=-1, keepdims=True) + EPS) * gf_ref[...]
        if first:
            out_refs[0][...] = y
            return

        @pl.when(ctx)
        def _():
            out_refs[0][...] = y

        @pl.when(jnp.logical_not(ctx))
        def _():
            out_refs[1][...] = y

    _first_step_or_not(body)


def _post_ffn(xs, op, os_, mod_l, w_o, wo_index, g, w_in, w_out, layer, g_final=None):
    tile = pl.BlockSpec((TM, D), lambda i: (i, 0))
    p_tile = pl.BlockSpec((TM, D), lambda i: (jnp.minimum(i, NT_P - 1), 0))
    s_tile = pl.BlockSpec((TM, D), lambda i: (jnp.maximum(i - NT_P, 0), 0))
    split_x = len(xs) == 2
    final = g_final is not None
    in_specs = ([p_tile, s_tile] if split_x else [tile]) + [
        p_tile, s_tile, _const_spec((MOD_ROWS, N_MOD * D)), HBM_SPEC, _const_spec((1, D)), HBM_SPEC, HBM_SPEC]
    args = list(xs) + [op, os_, mod_l, w_o, g, w_in, w_out]
    if final:
        in_specs.append(_const_spec((1, D)))
        args.append(g_final)
        out_specs = [p_tile, s_tile]
        out_shape = [jax.ShapeDtypeStruct((N_P, D), F32), jax.ShapeDtypeStruct((N_S, D), F32)]
    else:
        out_specs = tile
        out_shape = jax.ShapeDtypeStruct((N_TOK, D), F32)
    kern = lambda *refs: _post_ffn_kernel(*refs, layer=layer, wo_index=wo_index, split_x=split_x, final=final)
    return pl.pallas_call(
        kern,
        grid=(N_TOK // TM,),
        in_specs=in_specs,
        out_specs=out_specs,
        out_shape=out_shape,
        scratch_shapes=([pltpu.VMEM((D, D), BF16), pltpu.VMEM((D, 2 * D_FF), BF16), pltpu.VMEM((D_FF, D), BF16)]
                        + _stream_scratch((D, FF_CHUNK), FF_DEPTH_COLS)
                        + _stream_scratch((FF_CHUNK, D), FF_DEPTH_ROWS)),
        compiler_params=_cparams(1),
        name="post_ffn",
    )(*args)


def kernel(x_prompt, x_sample, cache_attn_k, cache_attn_v, state_hgrn, c, c_ctx, w_ada, b_ada, g_norm_mix,
           g_norm_ffn, w_qkv_attn, lam_attn, g_subln_attn, w_o_attn, w_in_rec, lb_logits_rec, g_out_rec,
           w_o_rec, w_four, w_ffn_in, w_ffn_out, g_final):
    xs = (x_prompt.reshape(N_P, D), x_sample.reshape(N_S, D))
    cond = jnp.concatenate([c_ctx.reshape(1, D), c, jnp.zeros((MOD_ROWS - 1 - DEC_BATCH, D), F32)], axis=0)
    mod = _modulation(cond, w_ada, b_ada)
    cache_kt = jnp.transpose(cache_attn_k, (0, 1, 3, 4, 5, 2)).reshape(DEC_BATCH, -1, HEADS, HD, PAST)
    rope_tabs = _rope_tables()
    gla_tabs = _gla_tables()
    gla_tabs = (gla_tabs[0], gla_tabs[1].astype(BF16))
    kv_ctx, new_s = None, []
    for i in range(DEPTH):
        kind, j = i % N_MIXERS, i // N_MIXERS
        g_mix = g_norm_mix[i].reshape(1, D)
        x_p = xs[0]
        x_s, s_tile0 = (xs[1], 0) if len(xs) == 2 else (xs[0], NT_P)
        if kind == 0:
            lam_init = 0.8 - 0.6 * math.exp(-0.3 * i)
            g_sub = g_subln_attn[j].reshape(1, HD)
            qs, ks, vs = _qkv_sample(x_s, s_tile0, mod[i], g_mix, w_qkv_attn, j, rope_tabs)
            os_ = _attn_sample(qs, ks, vs, cache_kt, cache_attn_v, j, lam_attn[j], g_sub, lam_init)
            kt_all, v_all, op = _attn_prompt(x_p, mod[i], g_mix, w_qkv_attn, j, lam_attn[j], g_sub, lam_init,
                                             prev=kv_ctx, after=qs)
            kv_ctx = (kt_all, v_all)
            w_o = w_o_attn
        elif kind == 1:
            proj = _rec_proj(xs[0], mod[i], g_mix, w_in_rec, j)
            g_out = g_out_rec[j].reshape(1, HD)
            op, s_ctx = _gla(*proj, lb_logits_rec, g_out, gla_tabs, row0=0, nb=BATCH, ln=SEQ, layer=i)
            new_s.append(s_ctx)
            os_ = _gla(*proj, lb_logits_rec, g_out, gla_tabs, row0=N_P, nb=DEC_BATCH, ln=DEC_SEQ, layer=i,
                       s0=state_hgrn[:, j])[0]
            w_o = w_o_rec
        else:
            op = _fourier(xs[0], mod[i], g_mix, row0=0, nb=BATCH, ln=SEQ, per_step=FOUR_CTX_PER_STEP,
                          row_of=lambda b: 0)
            os_ = _fourier(xs[0], mod[i], g_mix, row0=N_P, nb=DEC_BATCH, ln=DEC_SEQ, per_step=1,
                           row_of=lambda b: 1 + b)
            w_o = w_four
        out = _post_ffn(xs, op, os_, mod[i], w_o, j, g_norm_ffn[i].reshape(1, D), w_ffn_in, w_ffn_out, i,
                        g_final.reshape(1, D) if i == DEPTH - 1 else None)
        xs = tuple(out) if i == DEPTH - 1 else (out,)
    y_prompt = xs[0].reshape(BATCH, SEQ, D)
    y_sample = xs[1].reshape(DEC_BATCH, DEC_SEQ, D)
    new_k = jnp.transpose(kv_ctx[0].reshape(BATCH, -1, HEADS, 2, DH, SEQ), (0, 1, 5, 2, 3, 4))
    new_v = kv_ctx[1]
    return (y_prompt, y_sample, new_k, new_v, jnp.stack(new_s, axis=1))
```

```python
import math

import numpy as np
import jax
import jax.numpy as jnp
from jax import lax
from jax.experimental import pallas as pl
from jax.experimental.pallas import tpu as pltpu

F32 = jnp.float32
BF16 = jnp.bfloat16

D = 1024
BATCH = 16
SEQ = 256
DEPTH = 4
DEC_BATCH = 4
DEC_SEQ = 1024
PAST = 512
GRID_W = 64
N_MIXERS = 3
HEADS = 8
DH = 64
HD = 128
ROPE_THETA = 10000.0
D_FF = 2816
N_MOD = 6
EPS = 1e-6
LOG2E = 1.0 / math.log(2.0)
FOUR_GROUPS = 4
FOUR_DG = 256

N_P = BATCH * SEQ
N_S = DEC_BATCH * DEC_SEQ
N_TOK = N_P + N_S
TM = 512
NT_P = N_P // TM
NT_S = N_S // TM
TILES_PER_DEC_SEQ = DEC_SEQ // TM
SEQ_PER_TILE = TM // SEQ
MOD_ROWS = 8
FF_CHUNK = 256
ATT_TQ = 512
ATT_HP = 2
ATT_P_WIDTH = 2
ATT_SEQ_SKEW = 5
ATT_SKEW = 1
FOUR_CTX_PER_STEP = 4
GLA_HP = 8
GLA_HP_CARRY = 4
GLA_GROUP = 256
GLA_BLK = 8
GLA_LEVELS = (8, 16, 32, 64, 128)
VMEM_LIMIT = 60 * 1024 * 1024


def _cparams(n_axes):
    return pltpu.CompilerParams(dimension_semantics=("arbitrary",) * n_axes,
                                vmem_limit_bytes=VMEM_LIMIT)


def _const_spec(shape):
    nd = len(shape)
    return pl.BlockSpec(shape, lambda *_: (0,) * nd, pipeline_mode=pl.Buffered(1))


def _sigmoid(x):
    return 1.0 / (1.0 + jnp.exp(-x))


def _silu(x):
    return x * _sigmoid(x)


def _adanorm(x, g, shift, scale):
    ms = jnp.mean(x * x, axis=-1, keepdims=True)
    return x * lax.rsqrt(ms + EPS) * (g * (1.0 + scale)) + shift


def _mod_row(tile):
    return jnp.where(tile < NT_P, 0, 1 + (tile - NT_P) // TILES_PER_DEC_SEQ)


def _dot(a, b):
    return jnp.dot(a, b, preferred_element_type=F32)


def _dot_nt(a, b):
    return lax.dot_general(a, b, (((1,), (1,)), ((), ())), preferred_element_type=F32)


def _dot_tn(a, b):
    return lax.dot_general(a, b, (((0,), (0,)), ((), ())), preferred_element_type=F32)


def _split2(x):
    hi = x.astype(BF16)
    return hi, (x - hi.astype(F32)).astype(BF16)


def _interleaved(generators, skew=0):
    results = [None] * len(generators)
    active = list(range(len(generators)))
    rnd = 0
    while active:
        for i in list(active):
            if rnd < i * skew:
                continue
            try:
                next(generators[i])
            except StopIteration as stop:
                results[i] = stop.value
                active.remove(i)
        rnd += 1
        yield
    return results


def _interleave(generators, skew=0):
    rounds = _interleaved(generators, skew)
    while True:
        try:
            next(rounds)
        except StopIteration as stop:
            return stop.value


class _WeightStream:
    def __init__(self, jobs, stage_ref, sem_ref):
        self.jobs, self.stage, self.sem = jobs, stage_ref, sem_ref
        self.depth = stage_ref.shape[0]

    def _copy(self, k):
        slot = k % self.depth
        return pltpu.make_async_copy(self.jobs[k][0], self.stage.at[slot], self.sem.at[slot])

    def prime(self):
        for k in range(min(self.depth, len(self.jobs))):
            self._copy(k).start()

    def take(self, k):
        self._copy(k).wait()
        self.jobs[k][1][...] = self.stage[k % self.depth].astype(BF16)
        if k + self.depth < len(self.jobs):
            self._copy(k + self.depth).start()


def _stream_scratch(piece_shape, depth):
    return [pltpu.VMEM((depth,) + tuple(piece_shape), F32), pltpu.SemaphoreType.DMA((depth,))]


def _column_jobs(w_hbm, index, w_bf, width):
    return [(w_hbm.at[index, :, pl.ds(c, width)], w_bf.at[:, pl.ds(c, width)])
            for c in range(0, w_bf.shape[1], width)]


def _first_step_or_not(body):
    first = pl.program_id(0) == 0
    pl.when(first)(lambda: body(True))
    pl.when(jnp.logical_not(first))(lambda: body(False))


HBM_SPEC = pl.BlockSpec(memory_space=pl.ANY)
W_PIECE = 512
W_DEPTH = 3


MOD_TK = 512


def _mod_kernel(cond_ref, w_ref, b_ref, o_ref):
    kk = pl.program_id(1)
    s = _silu(cond_ref[...])
    s_hi = s.astype(BF16)
    s_lo = (s - s_hi.astype(F32)).astype(BF16)
    w = w_ref[0]
    w_hi = w.astype(BF16)
    w_lo = (w - w_hi.astype(F32)).astype(BF16)
    r = _dot(jnp.concatenate([s_hi, s_lo], axis=0), w_hi)
    part = r[:MOD_ROWS] + r[MOD_ROWS:] + _dot(s_hi, w_lo)

    @pl.when(kk == 0)
    def _():
        o_ref[0] = part + b_ref[0]

    @pl.when(kk != 0)
    def _():
        o_ref[0] = o_ref[0] + part


def _modulation(cond, w_ada, b_ada):
    return pl.pallas_call(
        _mod_kernel,
        grid=(DEPTH, D // MOD_TK),
        in_specs=[
            pl.BlockSpec((MOD_ROWS, MOD_TK), lambda l, k: (0, k)),
            pl.BlockSpec((1, MOD_TK, N_MOD * D), lambda l, k: (l, k, 0)),
            pl.BlockSpec((1, 1, N_MOD * D), lambda l, k: (l, 0, 0)),
        ],
        out_specs=pl.BlockSpec((1, MOD_ROWS, N_MOD * D), lambda l, k: (l, 0, 0)),
        out_shape=jax.ShapeDtypeStruct((DEPTH, MOD_ROWS, N_MOD * D), F32),
        compiler_params=_cparams(2),
        name="modulation",
    )(cond, w_ada, b_ada.reshape(DEPTH, 1, N_MOD * D))


def _rope_tables():
    t = np.arange(DEC_SEQ)
    row, col = t // GRID_W, t % GRID_W
    lane = np.arange(HD) % DH
    use_row = lane < DH // 2
    idx = (lane % (DH // 2)) % (DH // 4)
    inv = ROPE_THETA ** (-idx.astype(np.float64) / (DH // 4))
    pos = np.where(use_row[None, :], row[:, None], col[:, None]).astype(np.float64)
    ang = pos * inv[None, :]
    first = (lane % (DH // 2)) < DH // 4
    cos = np.cos(ang)
    sin = np.sin(ang)
    sin_minus = np.where(first[None, :], -sin, 0.0)
    sin_plus = np.where(first[None, :], 0.0, sin)
    return (jnp.asarray(cos, F32), jnp.asarray(sin_minus, F32), jnp.asarray(sin_plus, F32))


def _rope_tile(t, cos, sin_minus, sin_plus):
    return (t * cos + pltpu.roll(t, HD - DH // 4, 1) * sin_minus
            + pltpu.roll(t, DH // 4, 1) * sin_plus)


def _lambda(lam_ref, lam_init):
    lp = lam_ref[...]
    return (jnp.exp(jnp.sum(lp[0:1] * lp[1:2], axis=-1, keepdims=True))
            - jnp.exp(jnp.sum(lp[2:3] * lp[3:4], axis=-1, keepdims=True)) + lam_init)


def _with_ones(v):
    return jnp.concatenate([v, jnp.ones_like(v)], axis=1)


def _diff_attn_head(q, score_fns, values, lam, g_scaled):
    tq = q.shape[0]
    lane = lax.broadcasted_iota(jnp.int32, q.shape, 1)
    zero = jnp.zeros_like(q)
    qq = jnp.concatenate([jnp.where(lane < DH, q, zero), jnp.where(lane >= DH, q, zero)], axis=0)
    scores = [fn(qq) for fn in score_fns]
    yield
    mx = scores[0].max(axis=-1, keepdims=True)
    for s in scores[1:]:
        mx = jnp.maximum(mx, s.max(axis=-1, keepdims=True))
    yield
    acc = None
    for s, v in zip(scores, values):
        c = _dot(jnp.exp2(s - mx).astype(BF16), v)
        acc = c if acc is None else acc + c
        yield
    r = acc[:, 0:HD] / acc[:, HD:2 * HD]
    o = r[:tq] - lam * r[tq:]
    o = o * lax.rsqrt(jnp.mean(o * o, axis=-1, keepdims=True) + EPS) * g_scaled
    return o.astype(BF16)


def _attn_prompt_kernel(*refs, j, n_layers, lam_init):
    x_ref, mod_ref, g_ref, w_hbm, lam_ref, gsub_ref = refs[:6]
    kt_ref, v_hbm, o_ref, w_bf, stage, sem, v_buf, z_buf, v_sem = refs[-9:]
    owns_all = j == 0
    stream = _WeightStream(_column_jobs(w_hbm, j, w_bf, W_PIECE), stage, sem)
    per_mat = D // W_PIECE
    seq0 = pl.program_id(0) * SEQ_PER_TILE

    def v_copies(s):
        out = []
        for hh in range(HEADS):
            for jj in (range(n_layers) if owns_all else (j,)):
                src = v_buf.at[s, :, pl.ds(hh * HD, HD)] if jj == j else z_buf
                out.append(pltpu.make_async_copy(src, v_hbm.at[seq0 + s, jj, :, hh, :], v_sem.at[s, hh, jj]))
        return out

    def body(first):
        if first:
            stream.prime()
        m = mod_ref[pl.ds(0, 1), :]
        lam = _lambda(lam_ref, lam_init)
        g_scaled = gsub_ref[...] * (1.0 - lam_init)

        def sequence(s, loads):
            rows = slice(s * SEQ, (s + 1) * SEQ)
            h = _adanorm(x_ref[rows, :], g_ref[...], m[:, 0:D], m[:, D:2 * D]).astype(BF16)
            yield
            mats = []
            for n in range(3):
                if loads:
                    for p in range(per_mat):
                        stream.take(n * per_mat + p)
                mats.append(_dot(h, w_bf[:, n * D:(n + 1) * D]))
                yield
            q = (mats[0] * (DH ** -0.5 * LOG2E)).astype(BF16)
            k, v = mats[1], mats[2]
            kb, vb = k.astype(BF16), v.astype(BF16)
            if owns_all:
                for jj in range(n_layers):
                    kt_ref[s, jj] = k.T if jj == j else jnp.zeros((D, SEQ), F32)
            else:
                kt_ref[s] = k.T
            v_buf[s] = v
            for n, cp in enumerate(v_copies(s)):
                cp.start(priority=n % 2)
            yield
            for h0 in range(0, HEADS, ATT_P_WIDTH):
                heads = []
                for hh in range(h0, h0 + ATT_P_WIDTH):
                    cols = slice(hh * HD, (hh + 1) * HD)
                    kh = kb[:, cols]
                    heads.append(_diff_attn_head(q[:, cols], [lambda qq, kh=kh: _dot_nt(qq, kh)],
                                                 [_with_ones(vb[:, cols])], lam, g_scaled))
                outs = yield from _interleaved(heads, ATT_SKEW)
                for hh, o in zip(range(h0, h0 + ATT_P_WIDTH), outs):
                    o_ref[rows, hh * HD:(hh + 1) * HD] = o

        z_buf[...] = jnp.zeros_like(z_buf)
        _interleave([sequence(s, first and s == 0) for s in range(SEQ_PER_TILE)], skew=ATT_SEQ_SKEW)
        for s in range(SEQ_PER_TILE):
            for cp in v_copies(s):
                cp.wait()

    _first_step_or_not(body)


def _attn_prompt(x, mod_l, g, w_qkv, j, lam_p, g_sub, lam_init, prev=None, after=None):
    n_layers = w_qkv.shape[0]
    assert (prev is None) == (j == 0)
    tile = pl.BlockSpec((TM, D), lambda i: (i, 0))
    in_specs = [tile, _const_spec((MOD_ROWS, N_MOD * D)), _const_spec((1, D)), HBM_SPEC,
                _const_spec((4, DH)), _const_spec((1, HD))]
    args = [x, mod_l, g, w_qkv, lam_p, g_sub]
    if prev is None:
        kt_spec = pl.BlockSpec((SEQ_PER_TILE, n_layers, D, SEQ), lambda i: (i, 0, 0, 0))
        v_spec = pl.BlockSpec((SEQ_PER_TILE, n_layers, SEQ, D), lambda i: (i, 0, 0, 0))
        aliases = {}
    else:
        kt_spec = pl.BlockSpec((SEQ_PER_TILE, None, D, SEQ), lambda i: (i, j, 0, 0))
        v_spec = pl.BlockSpec((SEQ_PER_TILE, None, SEQ, D), lambda i: (i, j, 0, 0))
        aliases = {len(args): 0, len(args) + 1: 1}
        in_specs += [HBM_SPEC, HBM_SPEC]
        args += list(prev)
    if after is not None:
        in_specs.append(HBM_SPEC)
        args.append(after)
    kern = lambda *refs: _attn_prompt_kernel(*refs, j=j, n_layers=n_layers, lam_init=lam_init)
    return pl.pallas_call(
        kern,
        grid=(NT_P,),
        in_specs=in_specs,
        out_specs=[kt_spec, HBM_SPEC, tile],
        out_shape=[jax.ShapeDtypeStruct((BATCH, n_layers, D, SEQ), F32),
                   jax.ShapeDtypeStruct((BATCH, n_layers, SEQ, HEADS, HD), F32),
                   jax.ShapeDtypeStruct((N_P, D), BF16)],
        input_output_aliases=aliases,
        scratch_shapes=([pltpu.VMEM((D, 3 * D), BF16)] + _stream_scratch((D, W_PIECE), W_DEPTH)
                        + [pltpu.VMEM((SEQ_PER_TILE, SEQ, D), F32), pltpu.VMEM((SEQ, HD), F32),
                           pltpu.SemaphoreType.DMA((SEQ_PER_TILE, HEADS, n_layers))]),
        compiler_params=_cparams(1),
        name="attn_prompt",
    )(*args)


def _qkv_sample_kernel(x_ref, mod_ref, g_ref, w_hbm, cos_ref, sm_ref, sp_ref, q_ref, k_ref, v_ref,
                       w_bf, stage, sem, *, j):
    stream = _WeightStream(_column_jobs(w_hbm, j, w_bf, W_PIECE), stage, sem)
    per_mat = D // W_PIECE

    def body(first):
        if first:
            stream.prime()
        i = pl.program_id(0)
        m = mod_ref[pl.ds(1 + i // TILES_PER_DEC_SEQ, 1), :]

        def half(rows, loads):
            h = _adanorm(x_ref[rows, :], g_ref[...], m[:, 0:D], m[:, D:2 * D]).astype(BF16)
            cos, sm, sp = cos_ref[rows, :], sm_ref[rows, :], sp_ref[rows, :]
            yield
            for n, (o_ref, scale) in enumerate(((q_ref, DH ** -0.5 * LOG2E), (k_ref, None), (v_ref, None))):
                if loads:
                    for p in range(per_mat):
                        stream.take(n * per_mat + p)
                y = _dot(h, w_bf[:, n * D:(n + 1) * D])
                if o_ref is v_ref:
                    o_ref[rows, :] = y.astype(BF16)
                else:
                    if scale is not None:
                        y = y * scale
                    for hh in range(HEADS):
                        t = _rope_tile(y[:, hh * HD:(hh + 1) * HD], cos, sm, sp)
                        o_ref[rows, hh * HD:(hh + 1) * HD] = t.astype(BF16)
                yield

        _interleave([half(slice(r, r + TM // 2), first and r == 0) for r in (0, TM // 2)], skew=1)

    _first_step_or_not(body)


def _qkv_sample(x, tile0, mod_l, g, w_qkv, j, tabs):
    tile = pl.BlockSpec((TM, D), lambda i: (i, 0))
    tab = pl.BlockSpec((TM, HD), lambda i: (i % TILES_PER_DEC_SEQ, 0))
    kern = lambda *refs: _qkv_sample_kernel(*refs, j=j)
    return pl.pallas_call(
        kern,
        grid=(NT_S,),
        in_specs=[pl.BlockSpec((TM, D), lambda i: (i + tile0, 0)), _const_spec((MOD_ROWS, N_MOD * D)),
                  _const_spec((1, D)), HBM_SPEC, tab, tab, tab],
        out_specs=[tile, tile, tile],
        out_shape=[jax.ShapeDtypeStruct((N_S, D), BF16)] * 3,
        scratch_shapes=[pltpu.VMEM((D, 3 * D), BF16)] + _stream_scratch((D, W_PIECE), W_DEPTH),
        compiler_params=_cparams(1),
        name="qkv_sample",
    )(x, mod_l, g, w_qkv, *tabs)


def _attn_sample_kernel(*refs, j, lam_init):
    q_ref, kn_ref, vn_ref, kct_ref, vc_hbm, lam_ref, gsub_ref, o_ref, vc_buf, vc_sem = refs
    b, i = pl.program_id(0), pl.program_id(1)
    nb, nq = pl.num_programs(0), pl.num_programs(1)

    def vc_copies(req):
        slot = req % 2
        return [pltpu.make_async_copy(vc_hbm.at[req, j, :, hh, :], vc_buf.at[slot, hh], vc_sem.at[slot, hh])
                for hh in range(HEADS)]

    @pl.when(jnp.logical_and(b == 0, i == 0))
    def _():
        for n, cp in enumerate(vc_copies(b)):
            cp.start(priority=n % 2)

    @pl.when(i == 0)
    def _():
        for cp in vc_copies(b):
            cp.wait()

    @pl.when(jnp.logical_and(i == nq - 1, b + 1 < nb))
    def _():
        for n, cp in enumerate(vc_copies(b + 1)):
            cp.start(priority=n % 2)

    slot = b % 2
    lam = _lambda(lam_ref, lam_init)
    g_scaled = gsub_ref[...] * (1.0 - lam_init)
    for h0 in range(0, HEADS, ATT_HP):
        heads = []
        for hh in range(h0, h0 + ATT_HP):
            cols = slice(hh * HD, (hh + 1) * HD)
            kn = kn_ref[:, cols]
            kct = kct_ref[hh].astype(BF16)
            vc = vc_buf[slot, hh].astype(BF16)
            heads.append(_diff_attn_head(
                q_ref[:, cols], [lambda qq, kn=kn: _dot_nt(qq, kn), lambda qq, kct=kct: _dot(qq, kct)],
                [_with_ones(vn_ref[:, cols]), _with_ones(vc)], lam, g_scaled))
        for hh, o in zip(range(h0, h0 + ATT_HP), _interleave(heads, ATT_SKEW)):
            o_ref[:, hh * HD:(hh + 1) * HD] = o


def _attn_sample(q, kn, vn, cache_kt, cache_v, j, lam_p, g_sub, lam_init):
    nq = DEC_SEQ // ATT_TQ
    seq = pl.BlockSpec((DEC_SEQ, D), lambda b, i: (b, 0))
    blk = pl.BlockSpec((ATT_TQ, D), lambda b, i: (b * nq + i, 0))
    in_specs = [blk, seq, seq,
                pl.BlockSpec((None, None, HEADS, HD, PAST), lambda b, i: (b, j, 0, 0, 0)),
                HBM_SPEC,
                pl.BlockSpec((4, DH), lambda b, i: (0, 0)), pl.BlockSpec((1, HD), lambda b, i: (0, 0))]
    kern = lambda *refs: _attn_sample_kernel(*refs, j=j, lam_init=lam_init)
    return pl.pallas_call(
        kern,
        grid=(DEC_BATCH, nq),
        in_specs=in_specs,
        out_specs=blk,
        out_shape=jax.ShapeDtypeStruct((N_S, D), BF16),
        scratch_shapes=[pltpu.VMEM((2, HEADS, PAST, HD), F32), pltpu.SemaphoreType.DMA((2, HEADS))],
        compiler_params=_cparams(2),
        name="attn_sample",
    )(q, kn, vn, cache_kt, cache_v, lam_p, g_sub)


def _rec_proj_kernel(x_ref, mod_ref, g_ref, w_hbm, q_ref, v_ref, gate_ref, zf_ref, zb_ref,
                     w_bf, stage, sem, *, j):
    stream = _WeightStream(_column_jobs(w_hbm, j, w_bf, W_PIECE), stage, sem)
    per_mat = D // W_PIECE

    def body(first):
        if first:
            stream.prime()
        m = mod_ref[pl.ds(_mod_row(pl.program_id(0)), 1), :]

        def half(rows, loads):
            h = _adanorm(x_ref[rows, :], g_ref[...], m[:, 0:D], m[:, D:2 * D]).astype(BF16)
            yield
            for n, o_ref in enumerate((q_ref, v_ref, gate_ref, zf_ref, zb_ref)):
                if loads:
                    for p in range(per_mat):
                        stream.take(n * per_mat + p)
                y = _dot(h, w_bf[:, n * D:(n + 1) * D])
                if o_ref is q_ref or o_ref is gate_ref:
                    y = _silu(y)
                o_ref[rows, :] = y.astype(o_ref.dtype)
                yield

        _interleave([half(slice(r, r + TM // 2), first and r == 0) for r in (0, TM // 2)], skew=1)

    _first_step_or_not(body)


def _rec_proj(x, mod_l, g, w_in, j):
    tile = pl.BlockSpec((TM, D), lambda i: (i, 0))
    kern = lambda *refs: _rec_proj_kernel(*refs, j=j)
    return pl.pallas_call(
        kern,
        grid=(N_TOK // TM,),
        in_specs=[tile, _const_spec((MOD_ROWS, N_MOD * D)), _const_spec((1, D)), HBM_SPEC],
        out_specs=[tile] * 5,
        out_shape=[jax.ShapeDtypeStruct((N_TOK, D), BF16)] * 3 + [jax.ShapeDtypeStruct((N_TOK, D), F32)] * 2,
        scratch_shapes=[pltpu.VMEM((D, 5 * D), BF16)] + _stream_scratch((D, W_PIECE), W_DEPTH),
        compiler_params=_cparams(1),
        name="rec_proj",
    )(x, mod_l, g, w_in)


def _gla_tables():
    t = np.arange(GLA_GROUP)[:, None]
    s = np.arange(GLA_GROUP)[None, :]
    ids = np.full((GLA_GROUP, GLA_GROUP), -1, np.int32)
    for l, b in reversed(list(enumerate(GLA_LEVELS, 1))):
        ids = np.where((t // (2 * b) == s // (2 * b)) & (s < t), l, ids)
    ids = np.where((t // GLA_BLK == s // GLA_BLK) & (s <= t), 0, ids)
    ids = np.stack([ids, ids.T]).astype(np.int32)
    e = np.zeros((GLA_BLK * HD, GLA_GROUP), np.float32)
    for j in range(GLA_BLK):
        e[j * HD:(j + 1) * HD, j::GLA_BLK] = 1.0
    return jnp.asarray(ids), jnp.asarray(e)


def _bcast_row(x, blk, idx):
    n, w = x.shape
    r = x.reshape(n // blk, blk, w)[:, idx:idx + 1, :]
    return jnp.broadcast_to(r, (n // blk, blk, w)).reshape(n, w)


def _gla_group(q, v, z, lb, ids, e_mat, st, backward):
    n = GLA_GROUP
    sig = 0.5 * jnp.tanh(0.5 * z) + 0.5
    f = lb + (1.0 - lb) * sig
    k = (1.0 - lb) * (1.0 - sig)
    lf = jnp.log(f) * LOG2E
    row = lax.broadcasted_iota(jnp.int32, (n, n), 0)
    col = lax.broadcasted_iota(jnp.int32, (n, n), 1)
    tri = jnp.where((col >= row) if backward else (col <= row), 1.0, 0.0).astype(BF16)
    c2 = _dot(tri, jnp.concatenate(_split2(lf), axis=1))
    cum = c2[:, 0:HD] + c2[:, HD:2 * HD]
    ck = jnp.log(k) * LOG2E - cum
    yield

    prods = []
    for j in range(GLA_BLK):
        prods.append((q * jnp.exp2(jnp.minimum(cum + _bcast_row(ck, GLA_BLK, j), 0.0))).astype(BF16))
        yield
    half = n // 2
    ids_q = ids[0:half, 0:half]
    diag = lambda x, h: x[h * half:(h + 1) * half, h * half:(h + 1) * half]
    pair = _dot(jnp.concatenate(prods, axis=1), e_mat)
    quads = [jnp.where(ids_q == 0, diag(pair, h), 0.0) for h in range(2)]
    yield

    for l, b in enumerate(GLA_LEVELS[:-1], 1):
        edge = _bcast_row(cum, 2 * b, b if backward else b - 1)
        ql = (q * jnp.exp2(cum - edge)).astype(BF16)
        kl = jnp.exp2(edge + ck).astype(BF16)
        x = _dot_nt(ql, kl)
        quads = [jnp.where(ids_q == l, diag(x, h), quads[h]) for h in range(2)]
        yield

    assert GLA_LEVELS[-1] == half
    lo, hi = slice(0, half), slice(half, n)
    qs, ks = (lo, hi) if backward else (hi, lo)
    edge = cum[half:half + 1, :] if backward else cum[half - 1:half, :]
    cross = _dot_nt((q[qs] * jnp.exp2(cum[qs] - edge)).astype(BF16),
                    jnp.exp2(edge + ck[ks]).astype(BF16)).astype(BF16)
    o_halves = [_dot(quads[h].astype(BF16), v[h * half:(h + 1) * half]) for h in range(2)]
    o_halves[0 if backward else 1] += _dot(cross, v[ks])
    o = jnp.concatenate(o_halves, axis=0)
    yield

    total = cum[0:1, :] if backward else cum[n - 1:n, :]
    if st is not None:
        o = o + _dot_nt((q * jnp.exp2(cum)).astype(BF16), st.astype(BF16))
    kdec = jnp.exp2(total + ck).astype(BF16)
    upd = _dot_tn(v, kdec)
    st_new = upd if st is None else st * jnp.exp2(total) + upd
    return o, st_new


def _lower_bound(x, layer):
    rows = [x[l:l + 1, :] for l in range(DEPTH)]
    mx = rows[0]
    for r in rows[1:]:
        mx = jnp.maximum(mx, r)
    ex = [jnp.exp(r - mx) for r in rows]
    tot = ex[0]
    for e in ex[1:]:
        tot = tot + e
    acc = ex[0] * 0.0
    for l in range(1, layer + 1):
        acc = acc + ex[l]
    return acc / tot


def _gla_kernel(*refs, n_groups, layer, has_state, hp):
    if has_state:
        (q_ref, v_ref, gate_ref, zf_ref, zb_ref, lbl_ref, g_ref, ids_ref, e_ref, s0_ref,
         o_ref, acc_ref) = refs
    else:
        (q_ref, v_ref, gate_ref, zf_ref, zb_ref, lbl_ref, g_ref, ids_ref, e_ref,
         o_ref, sout_ref, acc_ref) = refs
    e_mat = e_ref[...]
    z_refs = (zf_ref, zb_ref)
    chains = [(hh, d) for hh in range(hp) for d in range(2)]
    cols = [slice(hh * HD, (hh + 1) * HD) for hh in range(hp)]
    lbs = {(hh, d): _lower_bound(lbl_ref[d, :, cols[hh]], layer) for hh, d in chains}

    def step(gi, sts):
        rows, gens = [], []
        for d in range(2):
            g = gi if d == 0 else n_groups - 1 - gi
            start = g * GLA_GROUP
            rows.append(pl.ds(start if isinstance(start, int) else pl.multiple_of(start, GLA_GROUP), GLA_GROUP))
        for c, (hh, d) in enumerate(chains):
            gens.append(_gla_group(q_ref[rows[d], cols[hh]].astype(F32), v_ref[rows[d], cols[hh]],
                                   z_refs[d][rows[d], cols[hh]], lbs[hh, d], ids_ref[d], e_mat, sts[c],
                                   backward=(d == 1)))
        new = []
        for (hh, d), (o, st) in zip(chains, _interleave(gens)):
            acc_ref[rows[d], cols[hh]] = acc_ref[rows[d], cols[hh]] + o
            new.append(st)
        return tuple(new)

    acc_ref[...] = jnp.zeros_like(acc_ref)
    if has_state:
        init = tuple(s0_ref[d, hh].T for hh, d in chains)
        lax.fori_loop(0, n_groups, step, init)
    else:
        sts = step(0, (None,) * len(chains))
        for (hh, d), st in zip(chains, sts):
            sout_ref[d, hh] = st.T
    for hh in range(hp):
        o = acc_ref[:, cols[hh]]
        y = o * lax.rsqrt(jnp.mean(o * o, axis=-1, keepdims=True) + EPS) * g_ref[...]
        o_ref[:, cols[hh]] = (y * gate_ref[:, cols[hh]].astype(F32)).astype(BF16)


def _gla(q, v, gate, zf, zb, lb_logits, g_out, tabs, *, row0, nb, ln, layer, s0=None):
    ids, e_mat = tabs
    assert ln % GLA_GROUP == 0 and (s0 is not None or ln == GLA_GROUP)
    b0 = row0 // ln
    hp = GLA_HP if s0 is None else GLA_HP_CARRY
    w = hp * HD
    seq = pl.BlockSpec((ln, w), lambda b, h: (b + b0, h))
    in_specs = [seq] * 5 + [
        pl.BlockSpec((2, DEPTH, w), lambda b, h: (0, 0, h)),
        pl.BlockSpec((1, HD), lambda b, h: (0, 0)),
        pl.BlockSpec((2, GLA_GROUP, GLA_GROUP), lambda b, h: (0, 0, 0)),
        pl.BlockSpec((GLA_BLK * HD, GLA_GROUP), lambda b, h: (0, 0)),
    ]
    args = [q, v, gate, zf, zb, lb_logits, g_out, ids, e_mat]
    out_specs = [pl.BlockSpec((ln, w), lambda b, h: (b, h))]
    out_shape = [jax.ShapeDtypeStruct((nb * ln, D), BF16)]
    state_spec = pl.BlockSpec((None, 2, hp, HD, HD), lambda b, h: (b, 0, h, 0, 0))
    if s0 is not None:
        in_specs.append(state_spec)
        args.append(s0)
    else:
        out_specs.append(state_spec)
        out_shape.append(jax.ShapeDtypeStruct((nb, 2, HEADS, HD, HD), F32))
    kern = lambda *refs: _gla_kernel(*refs, n_groups=ln // GLA_GROUP, layer=layer, has_state=s0 is not None,
                                      hp=hp)
    return pl.pallas_call(
        kern,
        grid=(nb, HEADS // hp),
        in_specs=in_specs,
        out_specs=out_specs,
        out_shape=out_shape,
        scratch_shapes=[pltpu.VMEM((ln, w), F32)],
        compiler_params=_cparams(2),
        name="gla_sample" if s0 is not None else "gla_prompt",
    )(*args)


def _dft_tables(n):
    c = np.arange(FOUR_DG)
    ang_c = 2.0 * np.pi * ((c[:, None] * c[None, :]) % FOUR_DG) / FOUR_DG
    cs = np.concatenate([np.cos(ang_c), np.sin(ang_c)], axis=1)
    t = np.arange(n)
    ang_n = 2.0 * np.pi * ((t[:, None] * t[None, :]) % n) / n
    cn = np.concatenate([np.cos(ang_n), -np.sin(ang_n)], axis=1) / math.sqrt(n * FOUR_DG)
    return jnp.asarray(cs, F32).astype(BF16), jnp.asarray(cn, F32).astype(BF16)


def _fourier_kernel(x_ref, mod_ref, g_ref, cs_ref, cn_ref, o_ref, *, row_of, ln):
    m = mod_ref[pl.ds(row_of(pl.program_id(0)), 1), :]
    h = _adanorm(x_ref[...], g_ref[...], m[:, 0:D], m[:, D:2 * D]).astype(BF16)
    cs = cs_ref[...]
    cn = cn_ref[...]
    seqs = [slice(r, r + ln) for r in range(0, h.shape[0], ln)]
    for g in range(FOUR_GROUPS):
        gc = slice(g * FOUR_DG, (g + 1) * FOUR_DG)
        xcs = _dot(h[:, gc], cs)
        rhs = jnp.concatenate([jnp.concatenate([xcs[rs, 0:FOUR_DG], xcs[rs, FOUR_DG:]], axis=0) for rs in seqs],
                              axis=1).astype(BF16)
        f = _dot(cn, rhs)
        for s, rs in enumerate(seqs):
            o_ref[rs, gc] = f[:, s * FOUR_DG:(s + 1) * FOUR_DG].astype(BF16)


def _fourier(x, mod_l, g, *, row0, nb, ln, per_step, row_of):
    cs, cn = _dft_tables(ln)
    rows = per_step * ln
    b0 = row0 // rows
    kern = lambda *refs: _fourier_kernel(*refs, row_of=row_of, ln=ln)
    return pl.pallas_call(
        kern,
        grid=(nb // per_step,),
        in_specs=[pl.BlockSpec((rows, D), lambda b: (b + b0, 0)), _const_spec((MOD_ROWS, N_MOD * D)),
                  _const_spec((1, D)), _const_spec((FOUR_DG, 2 * FOUR_DG)), _const_spec((ln, 2 * ln))],
        out_specs=pl.BlockSpec((rows, D), lambda b: (b, 0)),
        out_shape=jax.ShapeDtypeStruct((nb * ln, D), BF16),
        compiler_params=_cparams(1),
        name="fourier_%d" % ln,
    )(x, mod_l, g, cs, cn)


FF_DEPTH_COLS = 4
FF_DEPTH_ROWS = 2


def _post_ffn_kernel(*refs, layer, wo_index, split_x, final):
    refs = list(refs)
    x_refs = [refs.pop(0) for _ in range(2 if split_x else 1)]
    op_ref, os_ref, mod_ref, wo_hbm, g_ref, win_hbm, wout_hbm = refs[:7]
    refs = refs[7:]
    gf_ref = refs.pop(0) if final else None
    out_refs = [refs.pop(0) for _ in range(2 if final else 1)]
    wo_bf, win_bf, wout_bf, stage_c, sem_c, stage_r, sem_r = refs
    n_chunks = D_FF // FF_CHUNK
    wo_jobs = _column_jobs(wo_hbm, wo_index, wo_bf, FF_CHUNK)
    in_jobs = []
    for c in range(n_chunks):
        for c0 in (c * FF_CHUNK, D_FF + c * FF_CHUNK):
            in_jobs.append((win_hbm.at[layer, :, pl.ds(c0, FF_CHUNK)], win_bf.at[:, pl.ds(c0, FF_CHUNK)]))
    cols = _WeightStream(wo_jobs + in_jobs, stage_c, sem_c)
    rows = _WeightStream([(wout_hbm.at[layer, pl.ds(c * FF_CHUNK, FF_CHUNK), :],
                           wout_bf.at[pl.ds(c * FF_CHUNK, FF_CHUNK), :]) for c in range(n_chunks)],
                         stage_r, sem_r)

    def body(first):
        if first:
            cols.prime()
            rows.prime()
        i = pl.program_id(0)
        ctx = i < NT_P
        m = mod_ref[pl.ds(_mod_row(i), 1), :]
        x = jnp.where(ctx, x_refs[0][...], x_refs[1][...]) if split_x else x_refs[0][...]
        o = jnp.where(ctx, op_ref[...], os_ref[...])
        if first:
            for k in range(len(wo_jobs)):
                cols.take(k)
        x = x + m[:, 2 * D:3 * D] * _dot(o, wo_bf[...])
        h = _adanorm(x, g_ref[...], m[:, 3 * D:4 * D], m[:, 4 * D:5 * D]).astype(BF16)
        acc = None
        for c in range(n_chunks):
            if first:
                cols.take(len(wo_jobs) + 2 * c)
                cols.take(len(wo_jobs) + 2 * c + 1)
                rows.take(c)
            gt = _dot(h, win_bf[:, c * FF_CHUNK:(c + 1) * FF_CHUNK])
            up = _dot(h, win_bf[:, D_FF + c * FF_CHUNK:D_FF + (c + 1) * FF_CHUNK])
            part = _dot((_silu(gt) * up).astype(BF16), wout_bf[c * FF_CHUNK:(c + 1) * FF_CHUNK, :])
            acc = part if acc is None else acc + part
        x = x + m[:, 5 * D:6 * D] * acc
        if not final:
            out_refs[0][...] = x
            return
        y = x * lax.rsqrt(jnp.mean(x * x, axis=-1, keepdims=True) + EPS) * gf_ref[...]
        if first:
            out_refs[0][...] = y
            return

        @pl.when(ctx)
        def _():
            out_refs[0][...] = y

        @pl.when(jnp.logical_not(ctx))
        def _():
            out_refs[1][...] = y

    _first_step_or_not(body)


def _post_ffn(xs, op, os_, mod_l, w_o, wo_index, g, w_in, w_out, layer, g_final=None):
    tile = pl.BlockSpec((TM, D), lambda i: (i, 0))
    p_tile = pl.BlockSpec((TM, D), lambda i: (jnp.minimum(i, NT_P - 1), 0))
    s_tile = pl.BlockSpec((TM, D), lambda i: (jnp.maximum(i - NT_P, 0), 0))
    split_x = len(xs) == 2
    final = g_final is not None
    in_specs = ([p_tile, s_tile] if split_x else [tile]) + [
        p_tile, s_tile, _const_spec((MOD_ROWS, N_MOD * D)), HBM_SPEC, _const_spec((1, D)), HBM_SPEC, HBM_SPEC]
    args = list(xs) + [op, os_, mod_l, w_o, g, w_in, w_out]
    if final:
        in_specs.append(_const_spec((1, D)))
        args.append(g_final)
        out_specs = [p_tile, s_tile]
        out_shape = [jax.ShapeDtypeStruct((N_P, D), F32), jax.ShapeDtypeStruct((N_S, D), F32)]
    else:
        out_specs = tile
        out_shape = jax.ShapeDtypeStruct((N_TOK, D), F32)
    kern = lambda *refs: _post_ffn_kernel(*refs, layer=layer, wo_index=wo_index, split_x=split_x, final=final)
    return pl.pallas_call(
        kern,
        grid=(N_TOK // TM,),
        in_specs=in_specs,
        out_specs=out_specs,
        out_shape=out_shape,
        scratch_shapes=([pltpu.VMEM((D, D), BF16), pltpu.VMEM((D, 2 * D_FF), BF16), pltpu.VMEM((D_FF, D), BF16)]
                        + _stream_scratch((D, FF_CHUNK), FF_DEPTH_COLS)
                        + _stream_scratch((FF_CHUNK, D), FF_DEPTH_ROWS)),
        compiler_params=_cparams(1),
        name="post_ffn",
    )(*args)


def kernel(x_prompt, x_sample, cache_attn_k, cache_attn_v, state_hgrn, c, c_ctx, w_ada, b_ada, g_norm_mix,
           g_norm_ffn, w_qkv_attn, lam_attn, g_subln_attn, w_o_attn, w_in_rec, lb_logits_rec, g_out_rec,
           w_o_rec, w_four, w_ffn_in, w_ffn_out, g_final):
    xs = (x_prompt.reshape(N_P, D), x_sample.reshape(N_S, D))
    cond = jnp.concatenate([c_ctx.reshape(1, D), c, jnp.zeros((MOD_ROWS - 1 - DEC_BATCH, D), F32)], axis=0)
    mod = _modulation(cond, w_ada, b_ada)
    cache_kt = jnp.transpose(cache_attn_k, (0, 1, 3, 4, 5, 2)).reshape(DEC_BATCH, -1, HEADS, HD, PAST)
    rope_tabs = _rope_tables()
    gla_tabs = _gla_tables()
    gla_tabs = (gla_tabs[0], gla_tabs[1].astype(BF16))
    kv_ctx, new_s = None, []
    for i in range(DEPTH):
        kind, j = i % N_MIXERS, i // N_MIXERS
        g_mix = g_norm_mix[i].reshape(1, D)
        x_p = xs[0]
        x_s, s_tile0 = (xs[1], 0) if len(xs) == 2 else (xs[0], NT_P)
        if kind == 0:
            lam_init = 0.8 - 0.6 * math.exp(-0.3 * i)
            g_sub = g_subln_attn[j].reshape(1, HD)
            qs, ks, vs = _qkv_sample(x_s, s_tile0, mod[i], g_mix, w_qkv_attn, j, rope_tabs)
            os_ = _attn_sample(qs, ks, vs, cache_kt, cache_attn_v, j, lam_attn[j], g_sub, lam_init)
            kt_all, v_all, op = _attn_prompt(x_p, mod[i], g_mix, w_qkv_attn, j, lam_attn[j], g_sub, lam_init,
                                             prev=kv_ctx, after=qs)
            kv_ctx = (kt_all, v_all)
            w_o = w_o_attn
        elif kind == 1:
            proj = _rec_proj(xs[0], mod[i], g_mix, w_in_rec, j)
            g_out = g_out_rec[j].reshape(1, HD)
            op, s_ctx = _gla(*proj, lb_logits_rec, g_out, gla_tabs, row0=0, nb=BATCH, ln=SEQ, layer=i)
            new_s.append(s_ctx)
            os_ = _gla(*proj, lb_logits_rec, g_out, gla_tabs, row0=N_P, nb=DEC_BATCH, ln=DEC_SEQ, layer=i,
                       s0=state_hgrn[:, j])[0]
            w_o = w_o_rec
        else:
            op = _fourier(xs[0], mod[i], g_mix, row0=0, nb=BATCH, ln=SEQ, per_step=FOUR_CTX_PER_STEP,
                          row_of=lambda b: 0)
            os_ = _fourier(xs[0], mod[i], g_mix, row0=N_P, nb=DEC_BATCH, ln=DEC_SEQ, per_step=1,
                           row_of=lambda b: 1 + b)
            w_o = w_four
        out = _post_ffn(xs, op, os_, mod[i], w_o, j, g_norm_ffn[i].reshape(1, D), w_ffn_in, w_ffn_out, i,
                        g_final.reshape(1, D) if i == DEPTH - 1 else None)
        xs = tuple(out) if i == DEPTH - 1 else (out,)
    y_prompt = xs[0].reshape(BATCH, SEQ, D)
    y_sample = xs[1].reshape(DEC_BATCH, DEC_SEQ, D)
    new_k = jnp.transpose(kv_ctx[0].reshape(BATCH, -1, HEADS, 2, DH, SEQ), (0, 1, 5, 2, 3, 4))
    new_v = kv_ctx[1]
    return (y_prompt, y_sample, new_k, new_v, jnp.stack(new_s, axis=1))
```
